```python
import jax, jax.numpy as jnp
from jax import lax
import numpy as np

D_MODEL = 2048
BATCH = 8
SEQ = 2048
DEPTH = 1

N_HEADS = 16
HEAD_DIM = 128
ATTN_WIDTH = N_HEADS * HEAD_DIM
ROPE_THETA = 10000.0
MOBA_BLOCK = 256
MOBA_TOPK = 3
Q_CHUNK = 16
CONV_CHANNELS = D_MODEL
CONV_WIDTH = 31
N_GROUPS = 4
EXPERTS_PER_GROUP = 8
N_EXPERTS = N_GROUPS * EXPERTS_PER_GROUP
TOP_K_IN_GROUP = 2
D_FF_EXPERT = 1024
EXPERT_BLOCK = 128
LN_EPS = 1e-5
DN_ALPHA = (2.0 * DEPTH) ** 0.25
DN_BETA = (8.0 * DEPTH) ** -0.25
IN_WIDTH = 3 * ATTN_WIDTH + 2 * CONV_CHANNELS + 2 * D_MODEL

kernel_name = 'hybrid_moba_conformer_hiermoe_deepnorm'


def _layer_norm(x, g, b):
    xf = x.astype(jnp.float32)
    mu = jnp.mean(xf, axis=-1, keepdims=True)
    var = jnp.mean(jnp.square(xf - mu), axis=-1, keepdims=True)
    y = (xf - mu) * lax.rsqrt(var + LN_EPS)
    return (y * g.astype(jnp.float32) + b.astype(jnp.float32)).astype(x.dtype)


def _rope(t, pos):
    half = HEAD_DIM // 2
    inv_freq = jnp.power(ROPE_THETA, -jnp.arange(half, dtype=jnp.float32) * (2.0 / HEAD_DIM))
    ang = pos.astype(jnp.float32)[:, None] * inv_freq[None, :]
    cos, sin = jnp.cos(ang), jnp.sin(ang)
    tf = t.astype(jnp.float32)
    t1, t2 = tf[..., :half], tf[..., half:]
    return jnp.concatenate([t1 * cos - t2 * sin, t2 * cos + t1 * sin], axis=-1).astype(t.dtype)


def _moba_attention(q, k, v):
    B, H, S, Dh = q.shape
    nb = -(-S // MOBA_BLOCK)
    s_pad = nb * MOBA_BLOCK
    pad = ((0, 0), (0, 0), (0, s_pad - S), (0, 0))
    k_blk = jnp.pad(k, pad).reshape(B, H, nb, MOBA_BLOCK, Dh)
    v_blk = jnp.pad(v, pad).reshape(B, H, nb, MOBA_BLOCK, Dh)
    k_mean = jnp.mean(k_blk.astype(jnp.float32), axis=3)

    pos = jnp.arange(S)
    q_blk_id = pos // MOBA_BLOCK
    gate = jnp.einsum('bhsd,bhnd->bhsn', q.astype(jnp.float32), k_mean)
    past = jnp.arange(nb)[None, :] < q_blk_id[:, None]
    gate = jnp.where(past, gate, -jnp.inf)
    n_sel = min(MOBA_TOPK, nb)
    _, sel_idx = lax.top_k(gate, n_sel)
    sel_valid = jnp.arange(n_sel)[None, :] < q_blk_id[:, None]

    scale = Dh ** -0.5
    b_ix = jnp.arange(B)[:, None, None, None]
    h_ix = jnp.arange(H)[None, :, None, None]

    def chunk(c):
        t0 = c * Q_CHUNK
        q_c = lax.dynamic_slice_in_dim(q, t0, Q_CHUNK, axis=2)
        idx_c = lax.dynamic_slice_in_dim(sel_idx, t0, Q_CHUNK, axis=2)
        valid_c = lax.dynamic_slice_in_dim(sel_valid, t0, Q_CHUNK, axis=0)
        own = t0 // MOBA_BLOCK
        k_own = lax.dynamic_index_in_dim(k_blk, own, axis=2, keepdims=False)
        v_own = lax.dynamic_index_in_dim(v_blk, own, axis=2, keepdims=False)
        q_pos = t0 + jnp.arange(Q_CHUNK)
        k_pos = own * MOBA_BLOCK + jnp.arange(MOBA_BLOCK)
        causal = k_pos[None, :] <= q_pos[:, None]
        k_sel = k_blk[b_ix, h_ix, idx_c]
        v_sel = v_blk[b_ix, h_ix, idx_c]
        s_own = jnp.einsum('bhqd,bhkd->bhqk', q_c, k_own).astype(jnp.float32) * scale
        s_own = jnp.where(causal, s_own, -jnp.inf)
        s_sel = jnp.einsum('bhqd,bhqrkd->bhqrk', q_c, k_sel).astype(jnp.float32) * scale
        s_sel = jnp.where(valid_c[:, :, None], s_sel, -jnp.inf)
        s_all = jnp.concatenate([s_own, s_sel.reshape(B, H, Q_CHUNK, n_sel * MOBA_BLOCK)], axis=-1)
        p = jax.nn.softmax(s_all, axis=-1).astype(v.dtype)
        p_own = p[..., :MOBA_BLOCK]
        p_sel = p[..., MOBA_BLOCK:].reshape(B, H, Q_CHUNK, n_sel, MOBA_BLOCK)
        return (jnp.einsum('bhqk,bhkd->bhqd', p_own, v_own)
                + jnp.einsum('bhqrk,bhqrkd->bhqd', p_sel, v_sel))

    out = lax.map(chunk, jnp.arange(S // Q_CHUNK))
    return jnp.moveaxis(out, 0, 2).reshape(B, H, S, Dh)


def _conformer_conv(u, w_dw, b_dw, ln_g, ln_b, w_pw2, b_pw2):
    a, g = jnp.split(u, 2, axis=-1)
    h = a * jax.nn.sigmoid(g)
    h = lax.conv_general_dilated(
        h, w_dw[:, None, :].astype(h.dtype), window_strides=(1,),
        padding=[(CONV_WIDTH - 1, 0)],
        dimension_numbers=('NWC', 'WIO', 'NWC'),
        feature_group_count=h.shape[-1]) + b_dw
    h = jax.nn.silu(_layer_norm(h, ln_g, ln_b))
    return h @ w_pw2 + b_pw2


def _hier_moe(x, w_rg, b_rg, w_re, b_re, w1, w3, w2):
    B, S, D = x.shape
    T = B * S
    xt = x.reshape(T, D)
    g_logits = (xt @ w_rg).astype(jnp.float32) + b_rg.astype(jnp.float32)
    g_prob = jax.nn.softmax(g_logits, axis=-1)
    grp = jnp.argmax(g_logits, axis=-1)
    grp_w = jnp.take_along_axis(g_prob, grp[:, None], axis=-1)
    e_logits = jnp.einsum('td,gde->tge', xt, w_re).astype(jnp.float32) + b_re.astype(jnp.float32)
    e_logits = jnp.take_along_axis(e_logits, grp[:, None, None], axis=1)[:, 0]
    top_v, top_i = lax.top_k(e_logits, TOP_K_IN_GROUP)
    comb = jax.nn.softmax(top_v, axis=-1) * grp_w
    expert = grp[:, None] * EXPERTS_PER_GROUP + top_i

    A = T * TOP_K_IN_GROUP
    e_flat = expert.reshape(A).astype(jnp.int32)
    tok_flat = jnp.repeat(jnp.arange(T, dtype=jnp.int32), TOP_K_IN_GROUP)
    w_flat = comb.reshape(A)
    order = jnp.argsort(e_flat)
    e_sorted = e_flat[order]
    counts = jnp.zeros((N_EXPERTS,), jnp.int32).at[e_flat].add(1)
    starts = jnp.cumsum(counts) - counts
    padded = (counts + EXPERT_BLOCK - 1) // EXPERT_BLOCK * EXPERT_BLOCK
    pstarts = jnp.cumsum(padded) - padded
    pends = pstarts + padded
    dest = pstarts[e_sorted] + (jnp.arange(A, dtype=jnp.int32) - starts[e_sorted])
    n_blocks = -(-A // EXPERT_BLOCK) + N_EXPERTS
    P = n_blocks * EXPERT_BLOCK
    slot_tok = jnp.full((P,), T, jnp.int32).at[dest].set(tok_flat[order])
    slot_w = jnp.zeros((P,), w_flat.dtype).at[dest].set(w_flat[order])
    blk_expert = jnp.minimum(
        jnp.searchsorted(pends, jnp.arange(n_blocks, dtype=jnp.int32) * EXPERT_BLOCK, side='right'),
        N_EXPERTS - 1)
    x_pad = jnp.concatenate([xt, jnp.zeros((1, D), xt.dtype)], axis=0)

    def run_block(args):
        tok, w, e = args
        h = x_pad[tok]
        y = (jax.nn.silu(h @ w1[e]) * (h @ w3[e])) @ w2[e]
        return y * w[:, None].astype(y.dtype)

    y_slots = lax.map(run_block, (slot_tok.reshape(n_blocks, EXPERT_BLOCK),
                                  slot_w.reshape(n_blocks, EXPERT_BLOCK), blk_expert))
    y = jnp.zeros((T + 1, D), y_slots.dtype).at[slot_tok].add(y_slots.reshape(P, D))
    return y[:T].reshape(B, S, D).astype(x.dtype)


def setup_inputs(seed: int = 0) -> dict:
    key = jax.random.key(seed)
    ks = jax.random.split(key, 22)
    L, D, A, C = DEPTH, D_MODEL, ATTN_WIDTH, CONV_CHANNELS
    G, E, F = N_GROUPS, EXPERTS_PER_GROUP, D_FF_EXPERT

    def nrm(k, shape, s):
        return jax.random.normal(k, shape, jnp.float32) * s

    col_scale = jnp.concatenate([jnp.ones((2 * A,), jnp.float32),
                                 jnp.full((A + 2 * C,), DN_BETA, jnp.float32),
                                 jnp.ones((2 * D,), jnp.float32)]) * (D ** -0.5)
    return {
        'x': nrm(ks[0], (BATCH, SEQ, D), 1.0),
        'w_in': nrm(ks[1], (L, D, IN_WIDTH), 1.0) * col_scale,
        'b_in': nrm(ks[2], (L, IN_WIDTH), 0.02),
        'w_o_attn': nrm(ks[3], (L, A, D), A ** -0.5 * DN_BETA),
        'w_dw': nrm(ks[4], (L, CONV_WIDTH, C), CONV_WIDTH ** -0.5),
        'b_dw': nrm(ks[5], (L, C), 0.02),
        'conv_ln_g': 1.0 + nrm(ks[6], (L, C), 0.02),
        'conv_ln_b': nrm(ks[7], (L, C), 0.02),
        'w_pw2': nrm(ks[8], (L, C, D), C ** -0.5 * DN_BETA),
        'b_pw2': nrm(ks[9], (L, D), 0.02),
        'w_out': nrm(ks[10], (L, D, D), D ** -0.5 * DN_BETA),
        'ln1_g': 1.0 + nrm(ks[11], (L, D), 0.02),
        'ln1_b': nrm(ks[12], (L, D), 0.02),
        'w_rg': nrm(ks[13], (L, D, G), D ** -0.5),
        'b_rg': nrm(ks[14], (L, G), 0.01),
        'w_re': nrm(ks[15], (L, G, D, E), D ** -0.5),
        'b_re': nrm(ks[16], (L, G, E), 0.01),
        'w1': nrm(ks[17], (L, G * E, D, F), D ** -0.5 * DN_BETA),
        'w3': nrm(ks[18], (L, G * E, D, F), D ** -0.5 * DN_BETA),
        'w2': nrm(ks[19], (L, G * E, F, D), F ** -0.5 * DN_BETA),
        'ln2_g': 1.0 + nrm(ks[20], (L, D), 0.02),
        'ln2_b': nrm(ks[21], (L, D), 0.02),
    }


def reference(x, w_in, b_in, w_o_attn, w_dw, b_dw, conv_ln_g, conv_ln_b, w_pw2, b_pw2,
              w_out, ln1_g, ln1_b, w_rg, b_rg, w_re, b_re, w1, w3, w2, ln2_g, ln2_b):
    B, S, D = x.shape
    pos = jnp.arange(S)
    A, C = ATTN_WIDTH, CONV_CHANNELS
    for l in range(DEPTH):
        u = x @ w_in[l] + b_in[l]
        q, k, v, glu_in, gates = jnp.split(u, [A, 2 * A, 3 * A, 3 * A + 2 * C], axis=-1)

        def heads(t):
            return t.reshape(B, S, N_HEADS, HEAD_DIM).transpose(0, 2, 1, 3)

        qh = _rope(heads(q), pos)
        kh = _rope(heads(k), pos)
        o = _moba_attention(qh, kh, heads(v))
        y_attn = o.transpose(0, 2, 1, 3).reshape(B, S, A) @ w_o_attn[l]
        y_conv = _conformer_conv(glu_in, w_dw[l], b_dw[l], conv_ln_g[l], conv_ln_b[l],
                                 w_pw2[l], b_pw2[l])
        g_attn, g_conv = jnp.split(gates, 2, axis=-1)
        merged = jax.nn.sigmoid(g_attn) * y_attn + jax.nn.sigmoid(g_conv) * y_conv
        h = _layer_norm(DN_ALPHA * x + merged @ w_out[l], ln1_g[l], ln1_b[l])
        moe = _hier_moe(h, w_rg[l], b_rg[l], w_re[l], b_re[l], w1[l], w3[l], w2[l])
        x = _layer_norm(DN_ALPHA * h + moe, ln2_g[l], ln2_b[l])
    return x
```

```python
import functools

import jax
import jax.numpy as jnp
from jax import lax
from jax.experimental import pallas as pl
from jax.experimental.pallas import tpu as pltpu

F32 = jnp.float32
BF16 = jnp.bfloat16
U32 = jnp.uint32
I32 = jnp.int32

N_HEADS = 16
HEAD_DIM = 128
ROPE_THETA = 10000.0
MOBA_BLOCK = 256
MOBA_TOPK = 3
CONV_WIDTH = 31
N_GROUPS = 4
EXPERTS_PER_GROUP = 8
LN_EPS = 1e-5

LANES = 128
SUBLANES = 8
VMEM_LIMIT = 56 * 1024 * 1024

PROJ_TM = 1024
PROJ_TN = 1024
CONV_TS = 256
CONV_CW = 256
CONV_RC = 64
CONV_HALO = 32
POST_TM = 256
ROUTE_TT = 256
MOE_BM = 256


def _cparams(sem):
    return pltpu.CompilerParams(dimension_semantics=sem, vmem_limit_bytes=VMEM_LIMIT)


def _sigmoid(x):
    return 1.0 / (1.0 + jnp.exp(-x))


def _pack_bf16_pair(lo_f32, hi_f32):
    lo = lax.bitcast_convert_type(lo_f32.astype(BF16).astype(F32), U32) >> 16
    hi = lax.bitcast_convert_type(hi_f32.astype(BF16).astype(F32), U32)
    return hi | lo


def _unpack_lo(u):
    return lax.bitcast_convert_type(u << 16, F32)


def _unpack_hi(u):
    return lax.bitcast_convert_type(u & jnp.uint32(0xFFFF0000), F32)


def _inproj_body(x_ref, w_ref, wg_ref, b_ref, bg_ref, cos_ref, sin_ref, o_ref, xb_ref,
                 *, n_q, n_qk, n_qkv, n_glu, scale, head_dim):
    j = pl.program_id(1)

    @pl.when(j == 0)
    def _cast():
        xb_ref[...] = x_ref[...].astype(BF16)

    xb = xb_ref[...]
    acc = jnp.dot(xb, w_ref[...], preferred_element_type=F32) + b_ref[...]
    tn = acc.shape[1]

    @pl.when(j < n_qk)
    def _rope():
        s = jnp.where(j < n_q, scale, 1.0).astype(F32)
        cos = cos_ref[...] * s
        sin = sin_ref[...] * s
        for h in range(tn // head_dim):
            t = acc[:, h * head_dim:(h + 1) * head_dim]
            r = pltpu.roll(t, head_dim // 2, axis=1)
            o_ref[:, h * head_dim:(h + 1) * head_dim] = (t * cos + r * sin).astype(o_ref.dtype)

    @pl.when((j >= n_qk) & (j < n_qkv))
    def _plain():
        o_ref[...] = acc.astype(o_ref.dtype)

    @pl.when((j >= n_qkv) & (j < n_qkv + n_glu))
    def _glu():
        g = jnp.dot(xb, wg_ref[...], preferred_element_type=F32) + bg_ref[...]
        o_ref[...] = (acc * _sigmoid(g)).astype(o_ref.dtype)

    @pl.when(j >= n_qkv + n_glu)
    def _gate():
        o_ref[...] = _sigmoid(acc).astype(o_ref.dtype)


def _in_projection(x2, w_main, w_g, b_main, b_g, cos2, sin_s, *, seq, attn_w, conv_c,
                   head_dim, tm, tn):
    T, D = x2.shape
    W = w_main.shape[1]
    assert T % tm == 0 and seq % tm == 0 and W % tn == 0
    assert attn_w % tn == 0 and conv_c % tn == 0 and tn % head_dim == 0
    assert head_dim == LANES
    n_q = attn_w // tn
    n_glu = conv_c // tn
    n_qkv = 3 * n_q
    pos_tiles = seq // tm

    def g_idx(i, j):
        return (0, jnp.clip(j - n_qkv, 0, n_glu - 1))

    body = functools.partial(_inproj_body, n_q=n_q, n_qk=2 * n_q, n_qkv=n_qkv, n_glu=n_glu,
                             scale=float(head_dim) ** -0.5, head_dim=head_dim)
    return pl.pallas_call(
        body,
        grid=(T // tm, W // tn),
        in_specs=[
            pl.BlockSpec((tm, D), lambda i, j: (i, 0)),
            pl.BlockSpec((D, tn), lambda i, j: (0, j)),
            pl.BlockSpec((D, tn), g_idx),
            pl.BlockSpec((1, tn), lambda i, j: (0, j)),
            pl.BlockSpec((1, tn), g_idx),
            pl.BlockSpec((tm, head_dim), lambda i, j: (i % pos_tiles, 0)),
            pl.BlockSpec((tm, head_dim), lambda i, j: (i % pos_tiles, 0)),
        ],
        out_specs=pl.BlockSpec((tm, tn), lambda i, j: (i, j)),
        out_shape=jax.ShapeDtypeStruct((T, W), BF16),
        scratch_shapes=[pltpu.VMEM((tm, D), BF16)],
        compiler_params=_cparams(("parallel", "arbitrary")),
        name="in_projection",
    )(x2, w_main, w_g, b_main, b_g, cos2, sin_s)


def _attn_body(q_ref, k_ref, v_ref, o_ref, *, seq, blk, topk):
    nb = seq // blk
    dh = q_ref.shape[1]
    k = k_ref[...]
    v = v_ref[...]
    kmean = jnp.mean(k.astype(F32).reshape(nb, blk, dh), axis=1)
    kmean_p = jnp.concatenate([kmean, jnp.zeros((LANES - nb, dh), F32)], axis=0).astype(BF16)
    lane = lax.broadcasted_iota(I32, (blk, LANES), 1)
    row = lax.broadcasted_iota(I32, (blk, blk), 0)
    col = lax.broadcasted_iota(I32, (blk, blk), 1)
    causal = col <= row
    contract_last = (((1,), (1,)), ((), ()))
    neg_inf = jnp.float32(-jnp.inf)

    for i in range(nb):
        qi = q_ref[i * blk:(i + 1) * blk, :]
        nk = (i + 1) * blk
        s = lax.dot_general(qi, k[:nk], contract_last, preferred_element_type=F32)
        parts = []
        if i > topk:
            valid = lane < i
            gate = lax.dot_general(qi, kmean_p, contract_last, preferred_element_type=F32)
            gm = jnp.where(valid, gate, neg_inf)
            rank = jnp.zeros((blk, LANES), I32)
            for d in range(1, i):
                before = pltpu.roll(gm, d, axis=1)
                rank = rank + (before >= gm).astype(I32)
                after = pltpu.roll(gm, LANES - d, axis=1)
                rank = rank + (after > gm).astype(I32)
            bias = jnp.where(valid & (rank < topk), 0.0, neg_inf).astype(F32)
            for n in range(i):
                parts.append(s[:, n * blk:(n + 1) * blk] + bias[:, n:n + 1])
        elif i > 0:
            parts.append(s[:, :i * blk])
        parts.append(jnp.where(causal, s[:, i * blk:], neg_inf))
        sm = jnp.concatenate(parts, axis=1) if len(parts) > 1 else parts[0]
        m = jnp.max(sm, axis=1, keepdims=True)
        p = jnp.exp(sm - m)
        l = jnp.sum(p, axis=1, keepdims=True)
        o = jnp.dot(p.astype(BF16), v[:nk], preferred_element_type=F32) * (1.0 / l)
        o_ref[i * blk:(i + 1) * blk, :] = o.astype(o_ref.dtype)


def _moba_attention(u, *, batch, seq, n_heads, head_dim, blk, topk):
    T = u.shape[0]
    assert seq % blk == 0 and seq // blk <= LANES - SUBLANES and blk % SUBLANES == 0
    body = functools.partial(_attn_body, seq=seq, blk=blk, topk=topk)
    return pl.pallas_call(
        body,
        grid=(batch, n_heads),
        in_specs=[
            pl.BlockSpec((seq, head_dim), lambda b, h: (b, h)),
            pl.BlockSpec((seq, head_dim), lambda b, h: (b, n_heads + h)),
            pl.BlockSpec((seq, head_dim), lambda b, h: (b, 2 * n_heads + h)),
        ],
        out_specs=pl.BlockSpec((seq, head_dim), lambda b, h: (b, h)),
        out_shape=jax.ShapeDtypeStruct((T, n_heads * head_dim), BF16),
        compiler_params=_cparams(("parallel", "parallel")),
        name="moba_attention",
    )(u, u, u)


def _conv_body(x_ref, w_ref, bdw_ref, g_ref, b_ref, o_ref, win_ref, conv_ref,
               *, width, ts, cw, rc, halo, eps):
    nc = win_ref.shape[0]
    s_idx = pl.program_id(1)

    @pl.when(s_idx == 0)
    def _zero_halo():
        win_ref[:, 0:halo, :] = jnp.zeros((nc, halo, cw), F32)

    for c in range(nc):
        win_ref[c, halo:halo + ts, :] = x_ref[:, c * cw:(c + 1) * cw].astype(F32)

    first = halo - (width - 1)

    def chunk(c, carry):
        for r in range(ts // rc):
            acc = jnp.broadcast_to(bdw_ref[c], (rc, cw))
            for j in range(width):
                acc = acc + w_ref[c, j:j + 1, :] * win_ref[c, pl.ds(first + j + r * rc, rc), :]
            conv_ref[c, r * rc:(r + 1) * rc, :] = acc
        return carry

    lax.fori_loop(0, nc, chunk, 0)

    win_ref[:, 0:halo, :] = win_ref[:, ts:ts + halo, :]

    total = jnp.zeros((ts, 1), F32)
    for c in range(nc):
        total = total + jnp.sum(conv_ref[c], axis=1, keepdims=True)
    mu = total * (1.0 / (nc * cw))
    sq = jnp.zeros((ts, 1), F32)
    for c in range(nc):
        d = conv_ref[c] - mu
        sq = sq + jnp.sum(d * d, axis=1, keepdims=True)
    inv = lax.rsqrt(sq * (1.0 / (nc * cw)) + eps)
    for c in range(nc):
        y = (conv_ref[c] - mu) * inv * g_ref[c] + b_ref[c]
        o_ref[:, c * cw:(c + 1) * cw] = (y * _sigmoid(y)).astype(o_ref.dtype)


def _conv_branch(u, w_dw, b_dw, ln_g, ln_b, *, batch, seq, conv_c, col_block, ts, cw, rc, halo):
    T = u.shape[0]
    width = w_dw.shape[0]
    assert seq % ts == 0 and conv_c % cw == 0 and ts % rc == 0 and rc % SUBLANES == 0
    assert halo >= width - 1 and halo % SUBLANES == 0 and ts >= halo and cw % LANES == 0
    nc = conv_c // cw
    w_c = w_dw.reshape(width, nc, cw).transpose(1, 0, 2)
    as_chunks = lambda a: a.reshape(nc, 1, cw)
    n_s = seq // ts
    body = functools.partial(_conv_body, width=width, ts=ts, cw=cw, rc=rc, halo=halo, eps=LN_EPS)
    full3 = lambda b, s: (0, 0, 0)
    return pl.pallas_call(
        body,
        grid=(batch, n_s),
        in_specs=[
            pl.BlockSpec((ts, conv_c), lambda b, s: (b * n_s + s, col_block)),
            pl.BlockSpec((nc, width, cw), full3),
            pl.BlockSpec((nc, 1, cw), full3),
            pl.BlockSpec((nc, 1, cw), full3),
            pl.BlockSpec((nc, 1, cw), full3),
        ],
        out_specs=pl.BlockSpec((ts, conv_c), lambda b, s: (b * n_s + s, 0)),
        out_shape=jax.ShapeDtypeStruct((T, conv_c), BF16),
        scratch_shapes=[pltpu.VMEM((nc, halo + ts, cw), F32), pltpu.VMEM((nc, ts, cw), F32)],
        compiler_params=_cparams(("parallel", "arbitrary")),
        name="conv_branch",
    )(u, w_c, as_chunks(b_dw), as_chunks(ln_g), as_chunks(ln_b))


def _post_body(o_ref, hc_ref, ga_ref, gb_ref, x_ref, wo_ref, wp_ref, bp_ref, wout_ref,
               g1_ref, b1_ref, wrh_ref, wrl_ref, br_ref, h_ref, hp_ref, rt_ref,
               *, alpha, eps, n_groups, e_per_group):
    ya = jnp.dot(o_ref[...], wo_ref[...], preferred_element_type=F32)
    yc = jnp.dot(hc_ref[...], wp_ref[...], preferred_element_type=F32) + bp_ref[...]
    m = ga_ref[...].astype(F32) * ya + gb_ref[...].astype(F32) * yc
    z = alpha * x_ref[...] + jnp.dot(m.astype(BF16), wout_ref[...], preferred_element_type=F32)
    mu = jnp.mean(z, axis=1, keepdims=True)
    zc = z - mu
    var = jnp.mean(zc * zc, axis=1, keepdims=True)
    h = zc * lax.rsqrt(var + eps) * g1_ref[...] + b1_ref[...]
    h_ref[...] = h
    half = h.shape[1] // 2
    hp_ref[...] = _pack_bf16_pair(h[:, :half], h[:, half:])

    h_hi = h.astype(BF16)
    h_lo = (h - h_hi.astype(F32)).astype(BF16)
    logits = (jnp.dot(h_hi, wrh_ref[...], preferred_element_type=F32)
              + jnp.dot(h_lo, wrh_ref[...], preferred_element_type=F32)
              + jnp.dot(h_hi, wrl_ref[...], preferred_element_type=F32)
              + br_ref[...])
    tm = logits.shape[0]
    lane = lax.broadcasted_iota(I32, (tm, LANES), 1)
    neg_inf = jnp.float32(-jnp.inf)
    big = jnp.int32(LANES)

    def first_argmax(vals):
        top = jnp.max(vals, axis=1, keepdims=True)
        idx = jnp.min(jnp.where(vals == top, lane, big), axis=1, keepdims=True)
        return top, idx

    gl = jnp.where(lane < n_groups, logits, neg_inf)
    gmax, grp = first_argmax(gl)
    grp_w = 1.0 / jnp.sum(jnp.exp(gl - gmax), axis=1, keepdims=True)
    lo_lane = n_groups + grp * e_per_group
    el = jnp.where((lane >= lo_lane) & (lane < lo_lane + e_per_group), logits, neg_inf)
    v1, i1 = first_argmax(el)
    v2, i2 = first_argmax(jnp.where(lane == i1, neg_inf, el))
    t = jnp.exp(v2 - v1)
    p1 = 1.0 / (1.0 + t)
    c1 = p1 * grp_w
    c2 = (t * p1) * grp_w
    e1 = (i1 - n_groups).astype(F32)
    e2 = (i2 - n_groups).astype(F32)
    rt_ref[...] = jnp.where(lane == 0, e1, jnp.where(lane == 1, e2,
                            jnp.where(lane == 2, c1, jnp.where(lane == 3, c2, 0.0))))


def _post_block(o, hc, u, x2, wo, wp, bp, wout, g1, b1, wr_hi, wr_lo, br, *, gate_block,
                alpha, n_groups, e_per_group, tm):
    T, D = x2.shape
    A = o.shape[1]
    C = hc.shape[1]
    assert T % tm == 0 and D % (2 * LANES) == 0
    assert n_groups * (1 + e_per_group) <= LANES
    const = lambda i: (0, 0)
    resident = lambda shape: pl.BlockSpec(shape, const, pipeline_mode=pl.Buffered(1))
    body = functools.partial(_post_body, alpha=alpha, eps=LN_EPS, n_groups=n_groups,
                             e_per_group=e_per_group)
    return pl.pallas_call(
        body,
        grid=(T // tm,),
        in_specs=[
            pl.BlockSpec((tm, A), lambda i: (i, 0)),
            pl.BlockSpec((tm, C), lambda i: (i, 0)),
            pl.BlockSpec((tm, D), lambda i: (i, gate_block)),
            pl.BlockSpec((tm, D), lambda i: (i, gate_block + 1)),
            pl.BlockSpec((tm, D), lambda i: (i, 0)),
            resident((A, D)),
            resident((C, D)),
            resident((1, D)),
            resident((D, D)),
            resident((1, D)),
            resident((1, D)),
            resident((D, LANES)),
            resident((D, LANES)),
            resident((1, LANES)),
        ],
        out_specs=[
            pl.BlockSpec((tm, D), lambda i: (i, 0)),
            pl.BlockSpec((tm, D // 2), lambda i: (i, 0)),
            pl.BlockSpec((tm, LANES), lambda i: (i, 0)),
        ],
        out_shape=[
            jax.ShapeDtypeStruct((T, D), F32),
            jax.ShapeDtypeStruct((T, D // 2), U32),
            jax.ShapeDtypeStruct((T, LANES), F32),
        ],
        compiler_params=_cparams(("parallel",)),
        name="merge_project_route",
    )(o, hc, u, u, x2, wo, wp, bp, wout, g1, b1, wr_hi, wr_lo, br)


def _row_copy(src_ref, src_row, dst_ref, dst_row, sem):
    return pltpu.make_async_copy(src_ref.at[pl.ds(src_row, 1)], dst_ref.at[pl.ds(dst_row, 1)], sem)


def _dispatch_body(dest_ref, hp_ref, xs_in_ref, xs_ref, sem, *, tt):
    del xs_in_ref
    copies = []
    for r in range(tt):
        for kk in range(2):
            cp = _row_copy(hp_ref, r, xs_ref, dest_ref[0, 0, 2 * r + kk], sem)
            cp.start()
            copies.append(cp)
    for cp in copies:
        cp.wait()


def _dispatch(dest3, hp, n_slots, *, tt):
    T, W = hp.shape
    assert T % tt == 0
    zeros = jnp.zeros((n_slots, W), U32)
    body = functools.partial(_dispatch_body, tt=tt)
    return pl.pallas_call(
        body,
        grid=(T // tt,),
        in_specs=[
            pl.BlockSpec((1, 1, 2 * tt), lambda i: (i, 0, 0), memory_space=pltpu.SMEM),
            pl.BlockSpec((tt, W), lambda i: (i, 0)),
            pl.BlockSpec(memory_space=pl.ANY),
        ],
        out_specs=pl.BlockSpec(memory_space=pl.ANY),
        out_shape=jax.ShapeDtypeStruct((n_slots, W), U32),
        scratch_shapes=[pltpu.SemaphoreType.DMA(())],
        input_output_aliases={2: 0},
        compiler_params=_cparams(("arbitrary",)),
        name="moe_dispatch",
    )(dest3, hp, zeros)


def _moe_body(be_ref, nu_ref, xs_ref, w1_ref, w3_ref, w2_ref, ys_ref):
    del be_ref
    i = pl.program_id(0)

    @pl.when(i < nu_ref[0])
    def _compute():
        u = xs_ref[...]
        half = u.shape[1]
        x_lo = _unpack_lo(u).astype(BF16)
        x_hi = _unpack_hi(u).astype(BF16)
        a = (jnp.dot(x_lo, w1_ref[:half, :], preferred_element_type=F32)
             + jnp.dot(x_hi, w1_ref[half:, :], preferred_element_type=F32))
        b = (jnp.dot(x_lo, w3_ref[:half, :], preferred_element_type=F32)
             + jnp.dot(x_hi, w3_ref[half:, :], preferred_element_type=F32))
        hmid = (a * _sigmoid(a) * b).astype(BF16)
        y = jnp.dot(hmid, w2_ref[...], preferred_element_type=F32)
        ys_ref[...] = _pack_bf16_pair(y[:, :half], y[:, half:])

    @pl.when(i >= nu_ref[0])
    def _unused():
        ys_ref[...] = jnp.zeros(ys_ref.shape, U32)


def _moe_experts(blk_expert, n_used, xs, w1, w3, w2, *, bm):
    P, W = xs.shape
    _, D, F = w1.shape
    assert P % bm == 0 and D == 2 * W
    n_blocks = P // bm
    grid_spec = pltpu.PrefetchScalarGridSpec(
        num_scalar_prefetch=2,
        grid=(n_blocks,),
        in_specs=[
            pl.BlockSpec((bm, W), lambda i, be, nu: (i, 0)),
            pl.BlockSpec((None, D, F), lambda i, be, nu: (be[i], 0, 0)),
            pl.BlockSpec((None, D, F), lambda i, be, nu: (be[i], 0, 0)),
            pl.BlockSpec((None, F, D), lambda i, be, nu: (be[i], 0, 0)),
        ],
        out_specs=pl.BlockSpec((bm, W), lambda i, be, nu: (i, 0)),
    )
    return pl.pallas_call(
        _moe_body,
        grid_spec=grid_spec,
        out_shape=jax.ShapeDtypeStruct((P, W), U32),
        compiler_params=_cparams(("arbitrary",)),
        name="moe_experts",
    )(blk_expert, n_used, xs, w1, w3, w2)


def _combine_body(dest_ref, h_ref, rt_ref, g_ref, b_ref, ys_ref, o_ref, ybuf, sem,
                  *, tt, alpha, eps):
    copies = []
    for r in range(tt):
        for kk in range(2):
            cp = _row_copy(ys_ref, dest_ref[0, 0, 2 * r + kk], ybuf.at[kk], r, sem)
            cp.start()
            copies.append(cp)
    for cp in copies:
        cp.wait()
    u0 = ybuf[0]
    u1 = ybuf[1]
    half = u0.shape[1]
    c0 = rt_ref[:, 2:3]
    c1 = rt_ref[:, 3:4]
    z_lo = alpha * h_ref[:, :half] + (_unpack_lo(u0) * c0 + _unpack_lo(u1) * c1)
    z_hi = alpha * h_ref[:, half:] + (_unpack_hi(u0) * c0 + _unpack_hi(u1) * c1)
    n = 2 * half
    mu = (jnp.sum(z_lo, axis=1, keepdims=True) + jnp.sum(z_hi, axis=1, keepdims=True)) * (1.0 / n)
    d_lo = z_lo - mu
    d_hi = z_hi - mu
    var = (jnp.sum(d_lo * d_lo, axis=1, keepdims=True)
           + jnp.sum(d_hi * d_hi, axis=1, keepdims=True)) * (1.0 / n)
    inv = lax.rsqrt(var + eps)
    o_ref[:, :half] = d_lo * inv * g_ref[:, :half] + b_ref[:, :half]
    o_ref[:, half:] = d_hi * inv * g_ref[:, half:] + b_ref[:, half:]


def _combine(dest3, h, rt, ys, g2, b2, *, alpha, tt):
    T, D = h.shape
    W = ys.shape[1]
    assert T % tt == 0 and D == 2 * W
    body = functools.partial(_combine_body, tt=tt, alpha=alpha, eps=LN_EPS)
    return pl.pallas_call(
        body,
        grid=(T // tt,),
        in_specs=[
            pl.BlockSpec((1, 1, 2 * tt), lambda i: (i, 0, 0), memory_space=pltpu.SMEM),
            pl.BlockSpec((tt, D), lambda i: (i, 0)),
            pl.BlockSpec((tt, LANES), lambda i: (i, 0)),
            pl.BlockSpec((1, D), lambda i: (0, 0)),
            pl.BlockSpec((1, D), lambda i: (0, 0)),
            pl.BlockSpec(memory_space=pl.ANY),
        ],
        out_specs=pl.BlockSpec((tt, D), lambda i: (i, 0)),
        out_shape=jax.ShapeDtypeStruct((T, D), F32),
        scratch_shapes=[pltpu.VMEM((2, tt, W), U32), pltpu.SemaphoreType.DMA(())],
        compiler_params=_cparams(("arbitrary",)),
        name="moe_combine",
    )(dest3, h, rt, g2, b2, ys)


def _slot_plan(expert, n_experts, bm, n_blocks):
    e_flat = expert.reshape(-1)
    onehot = (e_flat[:, None] == jnp.arange(n_experts, dtype=I32)[None, :]).astype(I32)
    csum = jnp.cumsum(onehot, axis=0)
    counts = csum[-1]
    padded = (counts + bm - 1) // bm * bm
    pends = jnp.cumsum(padded)
    pstarts = pends - padded
    dest = jnp.sum(onehot * (csum - 1 + pstarts[None, :]), axis=1)
    blk_start = jnp.arange(n_blocks, dtype=I32) * bm
    blk_expert = jnp.sum((pends[None, :] <= blk_start[:, None]).astype(I32), axis=1)
    n_used = pends[-1] // bm
    last = jnp.minimum(blk_expert[jnp.maximum(n_used - 1, 0)], n_experts - 1)
    blk_expert = jnp.where(blk_start < pends[-1], blk_expert, last)
    return dest.astype(I32), blk_expert.astype(I32), n_used.reshape(1).astype(I32)


def _layer(x2, p, *, batch, seq, depth, n_heads, head_dim, moba_block, moba_topk,
           n_groups, e_per_group, tiles):
    T, D = x2.shape
    A = n_heads * head_dim
    C = p["w_dw"].shape[1]
    alpha = (2.0 * depth) ** 0.25
    tn = tiles["proj_tn"]
    assert A == C == D, "column-block addressing below assumes equal branch widths"

    w_in, b_in = p["w_in"], p["b_in"]
    w_main = jnp.concatenate([w_in[:, :3 * A + C], w_in[:, 3 * A + 2 * C:]], axis=1).astype(BF16)
    b_main = jnp.concatenate([b_in[:3 * A + C], b_in[3 * A + 2 * C:]])[None, :]
    w_g = w_in[:, 3 * A + C:3 * A + 2 * C].astype(BF16)
    b_g = b_in[3 * A + C:3 * A + 2 * C][None, :]

    half = head_dim // 2
    inv_freq = jnp.power(ROPE_THETA, -jnp.arange(half, dtype=F32) * (2.0 / head_dim))
    ang = jnp.arange(seq, dtype=F32)[:, None] * inv_freq[None, :]
    cos2 = jnp.concatenate([jnp.cos(ang), jnp.cos(ang)], axis=1)
    sin_s = jnp.concatenate([-jnp.sin(ang), jnp.sin(ang)], axis=1)

    u = _in_projection(x2, w_main, w_g, b_main, b_g, cos2, sin_s, seq=seq, attn_w=A, conv_c=C,
                       head_dim=head_dim, tm=tiles["proj_tm"], tn=tn)
    o = _moba_attention(u, batch=batch, seq=seq, n_heads=n_heads, head_dim=head_dim,
                        blk=moba_block, topk=moba_topk)
    hc = _conv_branch(u, p["w_dw"], p["b_dw"], p["conv_ln_g"], p["conv_ln_b"], batch=batch,
                      seq=seq, conv_c=C, col_block=3, ts=tiles["conv_ts"], cw=tiles["conv_cw"],
                      rc=tiles["conv_rc"], halo=CONV_HALO)

    n_experts = n_groups * e_per_group
    n_route = n_groups + n_experts
    w_route = jnp.concatenate(
        [p["w_rg"], p["w_re"].transpose(1, 0, 2).reshape(D, n_experts),
         jnp.zeros((D, LANES - n_route), F32)], axis=1)
    b_route = jnp.concatenate(
        [p["b_rg"], p["b_re"].reshape(n_experts), jnp.zeros((LANES - n_route,), F32)])[None, :]
    wr_hi = w_route.astype(BF16)
    wr_lo = (w_route - wr_hi.astype(F32)).astype(BF16)

    h, hp, rt = _post_block(
        o, hc, u, x2, p["w_o_attn"].astype(BF16), p["w_pw2"].astype(BF16), p["b_pw2"][None, :],
        p["w_out"].astype(BF16), p["ln1_g"][None, :], p["ln1_b"][None, :], wr_hi, wr_lo, b_route,
        gate_block=4, alpha=alpha, n_groups=n_groups, e_per_group=e_per_group,
        tm=tiles["post_tm"])

    bm = tiles["moe_bm"]
    tt = tiles["route_tt"]
    n_blocks = -(-2 * T // bm) + n_experts
    expert = rt[:, :2].astype(I32)
    dest, blk_expert, n_used = _slot_plan(expert, n_experts, bm, n_blocks)
    dest3 = dest.reshape(T // tt, 1, 2 * tt)

    xs = _dispatch(dest3, hp, n_blocks * bm, tt=tt)
    ys = _moe_experts(blk_expert, n_used, xs, p["w1"].astype(BF16), p["w3"].astype(BF16),
                      p["w2"].astype(BF16), bm=bm)
    return _combine(dest3, h, rt, ys, p["ln2_g"][None, :], p["ln2_b"][None, :], alpha=alpha, tt=tt)


_PARAM_NAMES = ("w_in", "b_in", "w_o_attn", "w_dw", "b_dw", "conv_ln_g", "conv_ln_b", "w_pw2",
                "b_pw2", "w_out", "ln1_g", "ln1_b", "w_rg", "b_rg", "w_re", "b_re", "w1", "w3",
                "w2", "ln2_g", "ln2_b")

_TILES = dict(proj_tm=PROJ_TM, proj_tn=PROJ_TN, conv_ts=CONV_TS, conv_cw=CONV_CW, conv_rc=CONV_RC,
              post_tm=POST_TM, route_tt=ROUTE_TT, moe_bm=MOE_BM)


def _forward(x, params, *, n_heads=N_HEADS, head_dim=HEAD_DIM, moba_block=MOBA_BLOCK,
             moba_topk=MOBA_TOPK, n_groups=N_GROUPS, e_per_group=EXPERTS_PER_GROUP, tiles=None):
    tiles = dict(_TILES, **(tiles or {}))
    B, S, D = x.shape
    depth = params["w_in"].shape[0]
    x2 = x.reshape(B * S, D)
    for l in range(depth):
        p = {k: v[l] for k, v in params.items()}
        x2 = _layer(x2, p, batch=B, seq=S, depth=depth, n_heads=n_heads, head_dim=head_dim,
                    moba_block=moba_block, moba_topk=moba_topk, n_groups=n_groups,
                    e_per_group=e_per_group, tiles=tiles)
    return x2.reshape(B, S, D)


def kernel(x, w_in, b_in, w_o_attn, w_dw, b_dw, conv_ln_g, conv_ln_b, w_pw2, b_pw2, w_out, ln1_g,
           ln1_b, w_rg, b_rg, w_re, b_re, w1, w3, w2, ln2_g, ln2_b):
    params = dict(zip(_PARAM_NAMES, (w_in, b_in, w_o_attn, w_dw, b_dw, conv_ln_g, conv_ln_b, w_pw2,
                                     b_pw2, w_out, ln1_g, ln1_b, w_rg, b_rg, w_re, b_re, w1, w3,
                                     w2, ln2_g, ln2_b)))
    return _forward(x, params)
```

```python
import functools

import jax
import jax.numpy as jnp
from jax import lax
from jax.experimental import pallas as pl
from jax.experimental.pallas import tpu as pltpu

F32 = jnp.float32
BF16 = jnp.bfloat16
U32 = jnp.uint32
I32 = jnp.int32

N_HEADS = 16
HEAD_DIM = 128
ROPE_THETA = 10000.0
MOBA_BLOCK = 256
MOBA_TOPK = 3
CONV_WIDTH = 31
N_GROUPS = 4
EXPERTS_PER_GROUP = 8
LN_EPS = 1e-5
LOG2_E = 1.4426950408889634

LANES = 128
SUBLANES = 8
VMEM_LIMIT = 56 * 1024 * 1024

PROJ_TM = 1024
PROJ_TN = 1024
ATTN_HPS = 2
ATTN_ONES_ROWS = 16
CONV_TS = 256
CONV_RT = 16
CONV_HALO = 32
POST_TM = 256
ROUTE_TT = 256
MOE_BM = 256


def _cparams(sem):
    return pltpu.CompilerParams(dimension_semantics=sem, vmem_limit_bytes=VMEM_LIMIT)


def _sigmoid(x):
    return 1.0 / (1.0 + jnp.exp(-x))


def _pack_bf16_pair(lo_f32, hi_f32):
    lo = lax.bitcast_convert_type(lo_f32.astype(BF16).astype(F32), U32) >> 16
    hi = lax.bitcast_convert_type(hi_f32.astype(BF16).astype(F32), U32)
    return hi | lo


def _unpack_lo(u):
    return lax.bitcast_convert_type(u << 16, F32)


def _unpack_hi(u):
    return lax.bitcast_convert_type(u & jnp.uint32(0xFFFF0000), F32)


def _inproj_body(x_ref, w_ref, wg_ref, b_ref, bg_ref, cos_ref, sin_ref, o_ref, xb_ref,
                 *, n_q, n_qk, n_qkv, n_glu, scale, head_dim):
    j = pl.program_id(1)

    @pl.when(j == 0)
    def _cast():
        xb_ref[...] = x_ref[...].astype(BF16)

    xb = xb_ref[...]
    acc = jnp.dot(xb, w_ref[...], preferred_element_type=F32) + b_ref[...]
    tn = acc.shape[1]

    @pl.when(j < n_qk)
    def _rope():
        s = jnp.where(j < n_q, scale, 1.0).astype(F32)
        cos = cos_ref[...] * s
        sin = sin_ref[...] * s
        for h in range(tn // head_dim):
            t = acc[:, h * head_dim:(h + 1) * head_dim]
            r = pltpu.roll(t, head_dim // 2, axis=1)
            o_ref[:, h * head_dim:(h + 1) * head_dim] = (t * cos + r * sin).astype(o_ref.dtype)

    @pl.when((j >= n_qk) & (j < n_qkv))
    def _plain():
        o_ref[...] = acc.astype(o_ref.dtype)

    @pl.when((j >= n_qkv) & (j < n_qkv + n_glu))
    def _glu():
        g = jnp.dot(xb, wg_ref[...], preferred_element_type=F32) + bg_ref[...]
        o_ref[...] = (acc * _sigmoid(g)).astype(o_ref.dtype)

    @pl.when(j >= n_qkv + n_glu)
    def _gate():
        o_ref[...] = _sigmoid(acc).astype(o_ref.dtype)


def _in_projection(x2, w_main, w_g, b_main, b_g, cos2, sin_s, *, seq, attn_w, conv_c,
                   head_dim, tm, tn):
    T, D = x2.shape
    W = w_main.shape[1]
    assert T % tm == 0 and seq % tm == 0 and W % tn == 0
    assert attn_w % tn == 0 and conv_c % tn == 0 and tn % head_dim == 0
    assert head_dim == LANES
    n_q = attn_w // tn
    n_glu = conv_c // tn
    n_qkv = 3 * n_q
    pos_tiles = seq // tm

    def g_idx(i, j):
        return (0, jnp.clip(j - n_qkv, 0, n_glu - 1))

    body = functools.partial(_inproj_body, n_q=n_q, n_qk=2 * n_q, n_qkv=n_qkv, n_glu=n_glu,
                             scale=float(head_dim) ** -0.5 * LOG2_E, head_dim=head_dim)
    return pl.pallas_call(
        body,
        grid=(T // tm, W // tn),
        in_specs=[
            pl.BlockSpec((tm, D), lambda i, j: (i, 0)),
            pl.BlockSpec((D, tn), lambda i, j: (0, j)),
            pl.BlockSpec((D, tn), g_idx),
            pl.BlockSpec((1, tn), lambda i, j: (0, j)),
            pl.BlockSpec((1, tn), g_idx),
            pl.BlockSpec((tm, head_dim), lambda i, j: (i % pos_tiles, 0)),
            pl.BlockSpec((tm, head_dim), lambda i, j: (i % pos_tiles, 0)),
        ],
        out_specs=pl.BlockSpec((tm, tn), lambda i, j: (i, j)),
        out_shape=jax.ShapeDtypeStruct((T, W), BF16),
        scratch_shapes=[pltpu.VMEM((tm, D), BF16)],
        compiler_params=_cparams(("parallel", "arbitrary")),
        name="in_projection",
    )(x2, w_main, w_g, b_main, b_g, cos2, sin_s)


def _attn_head(q, k, v, *, seq, blk, topk):
    nb = seq // blk
    nbp = -(-nb // SUBLANES) * SUBLANES
    dh = q.shape[1]
    contract_last = (((1,), (1,)), ((), ()))
    neg_inf = jnp.float32(-jnp.inf)

    kmean = jnp.mean(k.astype(F32).reshape(nb, blk, dh), axis=1)
    if nbp > nb:
        kmean = jnp.concatenate([kmean, jnp.zeros((nbp - nb, dh), F32)], axis=0)
    gate_t = lax.dot_general(kmean.astype(BF16), q, contract_last, preferred_element_type=F32)
    ones_rows = (lax.broadcasted_iota(I32, (ATTN_ONES_ROWS, seq), 0) == 0).astype(BF16)
    vt = jnp.concatenate([v.astype(F32).T.astype(BF16), ones_rows], axis=0)
    sub = lax.broadcasted_iota(I32, (nbp, blk), 0)
    key_r = lax.broadcasted_iota(I32, (blk, blk), 0)
    qry_c = lax.broadcasted_iota(I32, (blk, blk), 1)
    causal_t = key_r <= qry_c

    outs = []
    for i in range(nb):
        qi = q[i * blk:(i + 1) * blk]
        nk = (i + 1) * blk
        st = lax.dot_general(k[:nk], qi, contract_last, preferred_element_type=F32)
        parts = []
        if i > topk:
            gm = jnp.where(sub < i, gate_t[:, i * blk:(i + 1) * blk], neg_inf)
            rank = jnp.zeros((nbp, blk), I32)
            for m in range(i):
                gm_m = gm[m:m + 1, :]
                beats = (gm_m > gm) | ((gm_m == gm) & (sub > m))
                rank = rank + beats.astype(I32)
            bias_t = jnp.where((sub < i) & (rank < topk), 0.0, neg_inf).astype(F32)
            for n in range(i):
                parts.append(st[n * blk:(n + 1) * blk] + bias_t[n:n + 1, :])
        elif i > 0:
            parts.append(st[:i * blk])
        parts.append(jnp.where(causal_t, st[i * blk:], neg_inf))
        sm = jnp.concatenate(parts, axis=0) if len(parts) > 1 else parts[0]
        m = jnp.max(sm, axis=0, keepdims=True)
        p = jnp.exp2(sm - m).astype(BF16)
        ot = jnp.dot(vt[:, :nk], p, preferred_element_type=F32)
        outs.append((ot[:dh] * (1.0 / ot[dh:dh + 1])).T)
    return outs


def _attn_body(q_ref, k_ref, v_ref, o_ref, *, seq, blk, topk, dh):
    for hh in range(q_ref.shape[1] // dh):
        c0 = hh * dh
        outs = _attn_head(q_ref[:, c0:c0 + dh], k_ref[:, c0:c0 + dh], v_ref[:, c0:c0 + dh],
                          seq=seq, blk=blk, topk=topk)
        for i, o in enumerate(outs):
            o_ref[i * blk:(i + 1) * blk, c0:c0 + dh] = o.astype(o_ref.dtype)


def _moba_attention(u, *, batch, seq, n_heads, head_dim, blk, topk, hps):
    T = u.shape[0]
    assert seq % blk == 0 and blk % LANES == 0 and n_heads % hps == 0
    n_hb = n_heads // hps
    body = functools.partial(_attn_body, seq=seq, blk=blk, topk=topk, dh=head_dim)
    return pl.pallas_call(
        body,
        grid=(batch, n_hb),
        in_specs=[
            pl.BlockSpec((seq, hps * head_dim), lambda b, h: (b, h)),
            pl.BlockSpec((seq, hps * head_dim), lambda b, h: (b, n_hb + h)),
            pl.BlockSpec((seq, hps * head_dim), lambda b, h: (b, 2 * n_hb + h)),
        ],
        out_specs=pl.BlockSpec((seq, hps * head_dim), lambda b, h: (b, h)),
        out_shape=jax.ShapeDtypeStruct((T, n_heads * head_dim), BF16),
        compiler_params=_cparams(("parallel", "parallel")),
        name="moba_attention",
    )(u, u, u)


def _conv_body(x_ref, w_ref, bdw_ref, g_ref, b_ref, o_ref, win_ref, conv_ref,
               *, width, ts, rt, halo, eps):
    ng = win_ref.shape[0]
    sub = SUBLANES

    @pl.when(pl.program_id(1) == 0)
    def _zero_halo():
        win_ref[:, 0:halo * sub, :] = jnp.zeros((ng, halo * sub, LANES), F32)

    for g in range(ng):
        for s in range(sub):
            c0 = (g * sub + s) * LANES
            win_ref[g, pl.ds(halo * sub + s, ts, stride=sub), :] = x_ref[:, c0:c0 + LANES].astype(F32)

    first = halo - (width - 1)
    for g in range(ng):
        def chunk(r, carry, g=g):
            t0 = r * rt
            acc = jnp.broadcast_to(bdw_ref[g][None], (rt, sub, LANES))
            for j in range(width):
                start = pl.multiple_of((t0 + first + j) * sub, sub)
                slab = win_ref[g, pl.ds(start, rt * sub), :].reshape(rt, sub, LANES)
                acc = acc + slab * w_ref[g, j][None]
            conv_ref[g, pl.ds(pl.multiple_of(t0 * sub, sub), rt * sub), :] = acc.reshape(rt * sub, LANES)
            return carry

        lax.fori_loop(0, ts // rt, chunk, 0)

    win_ref[:, 0:halo * sub, :] = win_ref[:, ts * sub:(ts + halo) * sub, :]

    def channel_chunks():
        for g in range(ng):
            for s in range(sub):
                yield (g * sub + s) * LANES, conv_ref[g, pl.ds(s, ts, stride=sub), :]

    n = ng * sub * LANES
    total = jnp.zeros((ts, 1), F32)
    for _, y in channel_chunks():
        total = total + jnp.sum(y, axis=1, keepdims=True)
    mu = total * (1.0 / n)
    sq = jnp.zeros((ts, 1), F32)
    for _, y in channel_chunks():
        d = y - mu
        sq = sq + jnp.sum(d * d, axis=1, keepdims=True)
    inv = lax.rsqrt(sq * (1.0 / n) + eps)
    for c0, y in channel_chunks():
        z = (y - mu) * inv * g_ref[:, c0:c0 + LANES] + b_ref[:, c0:c0 + LANES]
        o_ref[:, c0:c0 + LANES] = (z * _sigmoid(z)).astype(o_ref.dtype)


def _conv_branch(u, w_dw, b_dw, ln_g, ln_b, *, batch, seq, conv_c, col_block, ts, rt, halo):
    T = u.shape[0]
    width = w_dw.shape[0]
    gw = SUBLANES * LANES
    assert seq % ts == 0 and conv_c % gw == 0 and ts % rt == 0
    assert halo >= width - 1 and ts >= halo
    ng = conv_c // gw
    w_t = w_dw.reshape(width, ng, SUBLANES, LANES).transpose(1, 0, 2, 3)
    n_s = seq // ts
    body = functools.partial(_conv_body, width=width, ts=ts, rt=rt, halo=halo, eps=LN_EPS)
    return pl.pallas_call(
        body,
        grid=(batch, n_s),
        in_specs=[
            pl.BlockSpec((ts, conv_c), lambda b, s: (b * n_s + s, col_block)),
            pl.BlockSpec((ng, width, SUBLANES, LANES), lambda b, s: (0, 0, 0, 0)),
            pl.BlockSpec((ng, SUBLANES, LANES), lambda b, s: (0, 0, 0)),
            pl.BlockSpec((1, conv_c), lambda b, s: (0, 0)),
            pl.BlockSpec((1, conv_c), lambda b, s: (0, 0)),
        ],
        out_specs=pl.BlockSpec((ts, conv_c), lambda b, s: (b * n_s + s, 0)),
        out_shape=jax.ShapeDtypeStruct((T, conv_c), BF16),
        scratch_shapes=[pltpu.VMEM((ng, (halo + ts) * SUBLANES, LANES), F32),
                        pltpu.VMEM((ng, ts * SUBLANES, LANES), F32)],
        compiler_params=_cparams(("parallel", "arbitrary")),
        name="conv_branch",
    )(u, w_t, b_dw.reshape(ng, SUBLANES, LANES), ln_g[None, :], ln_b[None, :])


def _post_body(o_ref, hc_ref, ga_ref, gb_ref, x_ref, wo_ref, wp_ref, bp_ref, wout_ref,
               g1_ref, b1_ref, wrh_ref, wrl_ref, br_ref, h_ref, hp_ref, rt_ref,
               *, alpha, eps, n_groups, e_per_group):
    ya = jnp.dot(o_ref[...], wo_ref[...], preferred_element_type=F32)
    yc = jnp.dot(hc_ref[...], wp_ref[...], preferred_element_type=F32) + bp_ref[...]
    m = ga_ref[...].astype(F32) * ya + gb_ref[...].astype(F32) * yc
    z = alpha * x_ref[...] + jnp.dot(m.astype(BF16), wout_ref[...], preferred_element_type=F32)
    mu = jnp.mean(z, axis=1, keepdims=True)
    zc = z - mu
    var = jnp.mean(zc * zc, axis=1, keepdims=True)
    h = zc * lax.rsqrt(var + eps) * g1_ref[...] + b1_ref[...]
    h_ref[...] = h
    half = h.shape[1] // 2
    hp_ref[...] = _pack_bf16_pair(h[:, :half], h[:, half:])

    h_hi = h.astype(BF16)
    h_lo = (h - h_hi.astype(F32)).astype(BF16)
    logits = (jnp.dot(h_hi, wrh_ref[...], preferred_element_type=F32)
              + jnp.dot(h_lo, wrh_ref[...], preferred_element_type=F32)
              + jnp.dot(h_hi, wrl_ref[...], preferred_element_type=F32)
              + br_ref[...])
    tm = logits.shape[0]
    lane = lax.broadcasted_iota(I32, (tm, LANES), 1)
    neg_inf = jnp.float32(-jnp.inf)
    big = jnp.int32(LANES)

    def first_argmax(vals):
        top = jnp.max(vals, axis=1, keepdims=True)
        idx = jnp.min(jnp.where(vals == top, lane, big), axis=1, keepdims=True)
        return top, idx

    gl = jnp.where(lane < n_groups, logits, neg_inf)
    gmax, grp = first_argmax(gl)
    grp_w = 1.0 / jnp.sum(jnp.exp(gl - gmax), axis=1, keepdims=True)
    lo_lane = n_groups + grp * e_per_group
    el = jnp.where((lane >= lo_lane) & (lane < lo_lane + e_per_group), logits, neg_inf)
    v1, i1 = first_argmax(el)
    v2, i2 = first_argmax(jnp.where(lane == i1, neg_inf, el))
    t = jnp.exp(v2 - v1)
    p1 = 1.0 / (1.0 + t)
    c1 = p1 * grp_w
    c2 = (t * p1) * grp_w
    e1 = (i1 - n_groups).astype(F32)
    e2 = (i2 - n_groups).astype(F32)
    rt_ref[...] = jnp.where(lane == 0, e1, jnp.where(lane == 1, e2,
                            jnp.where(lane == 2, c1, jnp.where(lane == 3, c2, 0.0))))


def _post_block(o, hc, u, x2, wo, wp, bp, wout, g1, b1, wr_hi, wr_lo, br, *, gate_block,
                alpha, n_groups, e_per_group, tm):
    T, D = x2.shape
    A = o.shape[1]
    C = hc.shape[1]
    assert T % tm == 0 and D % (2 * LANES) == 0
    assert n_groups * (1 + e_per_group) <= LANES
    const = lambda i: (0, 0)
    resident = lambda shape: pl.BlockSpec(shape, const, pipeline_mode=pl.Buffered(1))
    body = functools.partial(_post_body, alpha=alpha, eps=LN_EPS, n_groups=n_groups,
                             e_per_group=e_per_group)
    return pl.pallas_call(
        body,
        grid=(T // tm,),
        in_specs=[
            pl.BlockSpec((tm, A), lambda i: (i, 0)),
            pl.BlockSpec((tm, C), lambda i: (i, 0)),
            pl.BlockSpec((tm, D), lambda i: (i, gate_block)),
            pl.BlockSpec((tm, D), lambda i: (i, gate_block + 1)),
            pl.BlockSpec((tm, D), lambda i: (i, 0)),
            resident((A, D)),
            resident((C, D)),
            resident((1, D)),
            resident((D, D)),
            resident((1, D)),
            resident((1, D)),
            resident((D, LANES)),
            resident((D, LANES)),
            resident((1, LANES)),
        ],
        out_specs=[
            pl.BlockSpec((tm, D), lambda i: (i, 0)),
            pl.BlockSpec((tm, D // 2), lambda i: (i, 0)),
            pl.BlockSpec((tm, LANES), lambda i: (i, 0)),
        ],
        out_shape=[
            jax.ShapeDtypeStruct((T, D), F32),
            jax.ShapeDtypeStruct((T, D // 2), U32),
            jax.ShapeDtypeStruct((T, LANES), F32),
        ],
        compiler_params=_cparams(("parallel",)),
        name="merge_project_route",
    )(o, hc, u, u, x2, wo, wp, bp, wout, g1, b1, wr_hi, wr_lo, br)


def _row_copy(src_ref, src_row, dst_ref, dst_row, sem):
    return pltpu.make_async_copy(src_ref.at[pl.ds(src_row, 1)], dst_ref.at[pl.ds(dst_row, 1)], sem)


def _dispatch_body(dest_ref, hp_ref, xs_in_ref, xs_ref, sem, *, tt):
    del xs_in_ref
    copies = []
    for r in range(tt):
        for kk in range(2):
            cp = _row_copy(hp_ref, r, xs_ref, dest_ref[0, 0, 2 * r + kk], sem)
            cp.start()
            copies.append(cp)
    for cp in copies:
        cp.wait()


def _dispatch(dest3, hp, n_slots, *, tt):
    T, W = hp.shape
    assert T % tt == 0
    zeros = jnp.zeros((n_slots, W), U32)
    body = functools.partial(_dispatch_body, tt=tt)
    return pl.pallas_call(
        body,
        grid=(T // tt,),
        in_specs=[
            pl.BlockSpec((1, 1, 2 * tt), lambda i: (i, 0, 0), memory_space=pltpu.SMEM),
            pl.BlockSpec((tt, W), lambda i: (i, 0)),
            pl.BlockSpec(memory_space=pl.ANY),
        ],
        out_specs=pl.BlockSpec(memory_space=pl.ANY),
        out_shape=jax.ShapeDtypeStruct((n_slots, W), U32),
        scratch_shapes=[pltpu.SemaphoreType.DMA(())],
        input_output_aliases={2: 0},
        compiler_params=_cparams(("arbitrary",)),
        name="moe_dispatch",
    )(dest3, hp, zeros)


def _moe_body(be_ref, nu_ref, xs_ref, w1_ref, w3_ref, w2_ref, ys_ref):
    del be_ref
    i = pl.program_id(0)

    @pl.when(i < nu_ref[0])
    def _compute():
        u = xs_ref[...]
        half = u.shape[1]
        x_lo = _unpack_lo(u).astype(BF16)
        x_hi = _unpack_hi(u).astype(BF16)
        a = (jnp.dot(x_lo, w1_ref[:half, :], preferred_element_type=F32)
             + jnp.dot(x_hi, w1_ref[half:, :], preferred_element_type=F32))
        b = (jnp.dot(x_lo, w3_ref[:half, :], preferred_element_type=F32)
             + jnp.dot(x_hi, w3_ref[half:, :], preferred_element_type=F32))
        hmid = (a * _sigmoid(a) * b).astype(BF16)
        y = jnp.dot(hmid, w2_ref[...], preferred_element_type=F32)
        ys_ref[...] = _pack_bf16_pair(y[:, :half], y[:, half:])

    @pl.when(i >= nu_ref[0])
    def _unused():
        ys_ref[...] = jnp.zeros(ys_ref.shape, U32)


def _moe_experts(blk_expert, n_used, xs, w1, w3, w2, *, bm):
    P, W = xs.shape
    _, D, F = w1.shape
    assert P % bm == 0 and D == 2 * W
    n_blocks = P // bm
    grid_spec = pltpu.PrefetchScalarGridSpec(
        num_scalar_prefetch=2,
        grid=(n_blocks,),
        in_specs=[
            pl.BlockSpec((bm, W), lambda i, be, nu: (i, 0)),
            pl.BlockSpec((None, D, F), lambda i, be, nu: (be[i], 0, 0)),
            pl.BlockSpec((None, D, F), lambda i, be, nu: (be[i], 0, 0)),
            pl.BlockSpec((None, F, D), lambda i, be, nu: (be[i], 0, 0)),
        ],
        out_specs=pl.BlockSpec((bm, W), lambda i, be, nu: (i, 0)),
    )
    return pl.pallas_call(
        _moe_body,
        grid_spec=grid_spec,
        out_shape=jax.ShapeDtypeStruct((P, W), U32),
        compiler_params=_cparams(("arbitrary",)),
        name="moe_experts",
    )(blk_expert, n_used, xs, w1, w3, w2)


def _combine_body(dest_ref, h_ref, rt_ref, g_ref, b_ref, ys_ref, o_ref, ybuf, sem,
                  *, tt, alpha, eps):
    copies = []
    for r in range(tt):
        for kk in range(2):
            cp = _row_copy(ys_ref, dest_ref[0, 0, 2 * r + kk], ybuf.at[kk], r, sem)
            cp.start()
            copies.append(cp)
    for cp in copies:
        cp.wait()
    u0 = ybuf[0]
    u1 = ybuf[1]
    half = u0.shape[1]
    c0 = rt_ref[:, 2:3]
    c1 = rt_ref[:, 3:4]
    z_lo = alpha * h_ref[:, :half] + (_unpack_lo(u0) * c0 + _unpack_lo(u1) * c1)
    z_hi = alpha * h_ref[:, half:] + (_unpack_hi(u0) * c0 + _unpack_hi(u1) * c1)
    n = 2 * half
    mu = (jnp.sum(z_lo, axis=1, keepdims=True) + jnp.sum(z_hi, axis=1, keepdims=True)) * (1.0 / n)
    d_lo = z_lo - mu
    d_hi = z_hi - mu
    var = (jnp.sum(d_lo * d_lo, axis=1, keepdims=True)
           + jnp.sum(d_hi * d_hi, axis=1, keepdims=True)) * (1.0 / n)
    inv = lax.rsqrt(var + eps)
    o_ref[:, :half] = d_lo * inv * g_ref[:, :half] + b_ref[:, :half]
    o_ref[:, half:] = d_hi * inv * g_ref[:, half:] + b_ref[:, half:]


def _combine(dest3, h, rt, ys, g2, b2, *, alpha, tt):
    T, D = h.shape
    W = ys.shape[1]
    assert T % tt == 0 and D == 2 * W
    body = functools.partial(_combine_body, tt=tt, alpha=alpha, eps=LN_EPS)
    return pl.pallas_call(
        body,
        grid=(T // tt,),
        in_specs=[
            pl.BlockSpec((1, 1, 2 * tt), lambda i: (i, 0, 0), memory_space=pltpu.SMEM),
            pl.BlockSpec((tt, D), lambda i: (i, 0)),
            pl.BlockSpec((tt, LANES), lambda i: (i, 0)),
            pl.BlockSpec((1, D), lambda i: (0, 0)),
            pl.BlockSpec((1, D), lambda i: (0, 0)),
            pl.BlockSpec(memory_space=pl.ANY),
        ],
        out_specs=pl.BlockSpec((tt, D), lambda i: (i, 0)),
        out_shape=jax.ShapeDtypeStruct((T, D), F32),
        scratch_shapes=[pltpu.VMEM((2, tt, W), U32), pltpu.SemaphoreType.DMA(())],
        compiler_params=_cparams(("arbitrary",)),
        name="moe_combine",
    )(dest3, h, rt, g2, b2, ys)


def _slot_plan(expert, n_experts, bm, n_blocks):
    e_flat = expert.reshape(-1)
    onehot = (e_flat[:, None] == jnp.arange(n_experts, dtype=I32)[None, :]).astype(I32)
    csum = jnp.cumsum(onehot, axis=0)
    counts = csum[-1]
    padded = (counts + bm - 1) // bm * bm
    pends = jnp.cumsum(padded)
    pstarts = pends - padded
    dest = jnp.sum(onehot * (csum - 1 + pstarts[None, :]), axis=1)
    blk_start = jnp.arange(n_blocks, dtype=I32) * bm
    blk_expert = jnp.sum((pends[None, :] <= blk_start[:, None]).astype(I32), axis=1)
    n_used = pends[-1] // bm
    last = jnp.minimum(blk_expert[jnp.maximum(n_used - 1, 0)], n_experts - 1)
    blk_expert = jnp.where(blk_start < pends[-1], blk_expert, last)
    return dest.astype(I32), blk_expert.astype(I32), n_used.reshape(1).astype(I32)


def _layer(x2, p, *, batch, seq, depth, n_heads, head_dim, moba_block, moba_topk,
           n_groups, e_per_group, tiles):
    T, D = x2.shape
    A = n_heads * head_dim
    C = p["w_dw"].shape[1]
    alpha = (2.0 * depth) ** 0.25
    tn = tiles["proj_tn"]
    assert A == C == D, "column-block addressing below assumes equal branch widths"

    w_in, b_in = p["w_in"], p["b_in"]
    w_main = jnp.concatenate([w_in[:, :3 * A + C], w_in[:, 3 * A + 2 * C:]], axis=1).astype(BF16)
    b_main = jnp.concatenate([b_in[:3 * A + C], b_in[3 * A + 2 * C:]])[None, :]
    w_g = w_in[:, 3 * A + C:3 * A + 2 * C].astype(BF16)
    b_g = b_in[3 * A + C:3 * A + 2 * C][None, :]

    half = head_dim // 2
    inv_freq = jnp.power(ROPE_THETA, -jnp.arange(half, dtype=F32) * (2.0 / head_dim))
    ang = jnp.arange(seq, dtype=F32)[:, None] * inv_freq[None, :]
    cos2 = jnp.concatenate([jnp.cos(ang), jnp.cos(ang)], axis=1)
    sin_s = jnp.concatenate([-jnp.sin(ang), jnp.sin(ang)], axis=1)

    u = _in_projection(x2, w_main, w_g, b_main, b_g, cos2, sin_s, seq=seq, attn_w=A, conv_c=C,
                       head_dim=head_dim, tm=tiles["proj_tm"], tn=tn)
    o = _moba_attention(u, batch=batch, seq=seq, n_heads=n_heads, head_dim=head_dim,
                        blk=moba_block, topk=moba_topk, hps=ATTN_HPS)
    hc = _conv_branch(u, p["w_dw"], p["b_dw"], p["conv_ln_g"], p["conv_ln_b"], batch=batch,
                      seq=seq, conv_c=C, col_block=3, ts=tiles["conv_ts"], rt=tiles["conv_rt"],
                      halo=CONV_HALO)

    n_experts = n_groups * e_per_group
    n_route = n_groups + n_experts
    w_route = jnp.concatenate(
        [p["w_rg"], p["w_re"].transpose(1, 0, 2).reshape(D, n_experts),
         jnp.zeros((D, LANES - n_route), F32)], axis=1)
    b_route = jnp.concatenate(
        [p["b_rg"], p["b_re"].reshape(n_experts), jnp.zeros((LANES - n_route,), F32)])[None, :]
    wr_hi = w_route.astype(BF16)
    wr_lo = (w_route - wr_hi.astype(F32)).astype(BF16)

    h, hp, rt = _post_block(
        o, hc, u, x2, p["w_o_attn"].astype(BF16), p["w_pw2"].astype(BF16), p["b_pw2"][None, :],
        p["w_out"].astype(BF16), p["ln1_g"][None, :], p["ln1_b"][None, :], wr_hi, wr_lo, b_route,
        gate_block=4, alpha=alpha, n_groups=n_groups, e_per_group=e_per_group,
        tm=tiles["post_tm"])

    bm = tiles["moe_bm"]
    tt = tiles["route_tt"]
    n_blocks = -(-2 * T // bm) + n_experts
    expert = rt[:, :2].astype(I32)
    dest, blk_expert, n_used = _slot_plan(expert, n_experts, bm, n_blocks)
    dest3 = dest.reshape(T // tt, 1, 2 * tt)

    xs = _dispatch(dest3, hp, n_blocks * bm, tt=tt)
    ys = _moe_experts(blk_expert, n_used, xs, p["w1"].astype(BF16), p["w3"].astype(BF16),
                      p["w2"].astype(BF16), bm=bm)
    return _combine(dest3, h, rt, ys, p["ln2_g"][None, :], p["ln2_b"][None, :], alpha=alpha, tt=tt)


_PARAM_NAMES = ("w_in", "b_in", "w_o_attn", "w_dw", "b_dw", "conv_ln_g", "conv_ln_b", "w_pw2",
                "b_pw2", "w_out", "ln1_g", "ln1_b", "w_rg", "b_rg", "w_re", "b_re", "w1", "w3",
                "w2", "ln2_g", "ln2_b")

_TILES = dict(proj_tm=PROJ_TM, proj_tn=PROJ_TN, conv_ts=CONV_TS, conv_rt=CONV_RT,
              post_tm=POST_TM, route_tt=ROUTE_TT, moe_bm=MOE_BM)


def _forward(x, params, *, n_heads=N_HEADS, head_dim=HEAD_DIM, moba_block=MOBA_BLOCK,
             moba_topk=MOBA_TOPK, n_groups=N_GROUPS, e_per_group=EXPERTS_PER_GROUP, tiles=None):
    tiles = dict(_TILES, **(tiles or {}))
    B, S, D = x.shape
    depth = params["w_in"].shape[0]
    x2 = x.reshape(B * S, D)
    for l in range(depth):
        p = {k: v[l] for k, v in params.items()}
        x2 = _layer(x2, p, batch=B, seq=S, depth=depth, n_heads=n_heads, head_dim=head_dim,
                    moba_block=moba_block, moba_topk=moba_topk, n_groups=n_groups,
                    e_per_group=e_per_group, tiles=tiles)
    return x2.reshape(B, S, D)


def kernel(x, w_in, b_in, w_o_attn, w_dw, b_dw, conv_ln_g, conv_ln_b, w_pw2, b_pw2, w_out, ln1_g,
           ln1_b, w_rg, b_rg, w_re, b_re, w1, w3, w2, ln2_g, ln2_b):
    params = dict(zip(_PARAM_NAMES, (w_in, b_in, w_o_attn, w_dw, b_dw, conv_ln_g, conv_ln_b, w_pw2,
                                     b_pw2, w_out, ln1_g, ln1_b, w_rg, b_rg, w_re, b_re, w1, w3,
                                     w2, ln2_g, ln2_b)))
    return _forward(x, params)
```

```python
import functools

import jax
import jax.numpy as jnp
from jax import lax
from jax.experimental import pallas as pl
from jax.experimental.pallas import tpu as pltpu

F32 = jnp.float32
BF16 = jnp.bfloat16
U32 = jnp.uint32
I32 = jnp.int32

N_HEADS = 16
HEAD_DIM = 128
ROPE_THETA = 10000.0
MOBA_BLOCK = 256
MOBA_TOPK = 3
CONV_WIDTH = 31
N_GROUPS = 4
EXPERTS_PER_GROUP = 8
LN_EPS = 1e-5
LOG2_E = 1.4426950408889634

LANES = 128
SUBLANES = 8
VMEM_LIMIT = 56 * 1024 * 1024

PROJ_TM = 1024
PROJ_TN = 1024
ATTN_HPS = 2
ATTN_ONES_ROWS = 16
CONV_TS = 256
CONV_RT = 16
CONV_HALO = 32
POST_TM = 256
COMB_TT = 512
MOE_BM = 256
MOE_NCH = 4


def _cparams(sem):
    return pltpu.CompilerParams(dimension_semantics=sem, vmem_limit_bytes=VMEM_LIMIT)


def _sigmoid(x):
    return 1.0 / (1.0 + jnp.exp(-x))


def _pack_bf16_pair(lo_f32, hi_f32):
    lo = lax.bitcast_convert_type(lo_f32.astype(BF16).astype(F32), U32) >> 16
    hi = lax.bitcast_convert_type(hi_f32.astype(BF16).astype(F32), U32)
    return hi | lo


def _unpack_lo(u):
    return lax.bitcast_convert_type(u << 16, F32)


def _unpack_hi(u):
    return lax.bitcast_convert_type(u & jnp.uint32(0xFFFF0000), F32)


def _store_tile_rows(ref, lead, x):
    n = x.shape[0]
    for s in range(SUBLANES):
        ref[(*lead, pl.ds(s, n, stride=SUBLANES), slice(None))] = x[:, s * LANES:(s + 1) * LANES]


def _load_tile_rows(ref, lead, n):
    return jnp.concatenate(
        [ref[(*lead, pl.ds(s, n, stride=SUBLANES), slice(None))] for s in range(SUBLANES)], axis=1)


def _inproj_body(x_ref, w_ref, wg_ref, b_ref, bg_ref, cos_ref, sin_ref, o_ref, xb_ref,
                 *, n_q, n_qk, n_qkv, n_glu, scale, head_dim):
    j = pl.program_id(1)
    tn = o_ref.shape[1]

    @pl.when(j == 0)
    def _cast():
        xb_ref[...] = x_ref[...].astype(BF16)

    def project(w, b):
        return jnp.dot(xb_ref[...], w[...], preferred_element_type=F32) + b[...]

    @pl.when(j < n_qk)
    def _rope():
        acc = project(w_ref, b_ref)
        s = jnp.where(j < n_q, scale, 1.0).astype(F32)
        cos = cos_ref[...] * s
        sin = sin_ref[...] * s
        for h in range(tn // head_dim):
            t = acc[:, h * head_dim:(h + 1) * head_dim]
            r = pltpu.roll(t, head_dim // 2, axis=1)
            o_ref[:, h * head_dim:(h + 1) * head_dim] = (t * cos + r * sin).astype(o_ref.dtype)

    @pl.when((j >= n_qk) & (j < n_qkv))
    def _plain():
        o_ref[...] = project(w_ref, b_ref).astype(o_ref.dtype)

    @pl.when((j >= n_qkv) & (j < n_qkv + n_glu))
    def _glu():
        a = project(w_ref, b_ref)
        g = project(wg_ref, bg_ref)
        o_ref[...] = (a * _sigmoid(g)).astype(o_ref.dtype)

    @pl.when(j >= n_qkv + n_glu)
    def _gate():
        o_ref[...] = _sigmoid(project(w_ref, b_ref)).astype(o_ref.dtype)


def _in_projection(x2, w_main, w_g, b_main, b_g, cos2, sin_s, *, seq, attn_w, conv_c,
                   head_dim, tm, tn):
    T, D = x2.shape
    W = w_main.shape[1]
    assert T % tm == 0 and seq % tm == 0 and W % tn == 0
    assert attn_w % tn == 0 and conv_c % tn == 0 and tn % head_dim == 0
    assert head_dim == LANES
    n_q = attn_w // tn
    n_glu = conv_c // tn
    n_qkv = 3 * n_q
    pos_tiles = seq // tm

    def g_idx(i, j):
        return (0, jnp.clip(j - n_qkv, 0, n_glu - 1))

    body = functools.partial(_inproj_body, n_q=n_q, n_qk=2 * n_q, n_qkv=n_qkv, n_glu=n_glu,
                             scale=float(head_dim) ** -0.5 * LOG2_E, head_dim=head_dim)
    return pl.pallas_call(
        body,
        grid=(T // tm, W // tn),
        in_specs=[
            pl.BlockSpec((tm, D), lambda i, j: (i, 0)),
            pl.BlockSpec((D, tn), lambda i, j: (0, j)),
            pl.BlockSpec((D, tn), g_idx),
            pl.BlockSpec((1, tn), lambda i, j: (0, j)),
            pl.BlockSpec((1, tn), g_idx),
            pl.BlockSpec((tm, head_dim), lambda i, j: (i % pos_tiles, 0)),
            pl.BlockSpec((tm, head_dim), lambda i, j: (i % pos_tiles, 0)),
        ],
        out_specs=pl.BlockSpec((tm, tn), lambda i, j: (i, j)),
        out_shape=jax.ShapeDtypeStruct((T, W), BF16),
        scratch_shapes=[pltpu.VMEM((tm, D), BF16)],
        compiler_params=_cparams(("parallel", "arbitrary")),
        name="in_projection",
    )(x2, w_main, w_g, b_main, b_g, cos2, sin_s)


def _attn_head(q, k, v, *, seq, blk, topk):
    nb = seq // blk
    nbp = -(-nb // SUBLANES) * SUBLANES
    dh = q.shape[1]
    contract_last = (((1,), (1,)), ((), ()))
    neg_inf = jnp.float32(-jnp.inf)

    kmean = jnp.mean(k.astype(F32).reshape(nb, blk, dh), axis=1)
    if nbp > nb:
        kmean = jnp.concatenate([kmean, jnp.zeros((nbp - nb, dh), F32)], axis=0)
    gate_t = lax.dot_general(kmean.astype(BF16), q, contract_last, preferred_element_type=F32)
    ones_rows = (lax.broadcasted_iota(I32, (ATTN_ONES_ROWS, seq), 0) == 0).astype(BF16)
    vt = jnp.concatenate([v.astype(F32).T.astype(BF16), ones_rows], axis=0)
    sub = lax.broadcasted_iota(I32, (nbp, blk), 0)
    key_r = lax.broadcasted_iota(I32, (blk, blk), 0)
    qry_c = lax.broadcasted_iota(I32, (blk, blk), 1)
    causal_t = key_r <= qry_c

    outs = []
    for i in range(nb):
        qi = q[i * blk:(i + 1) * blk]
        nk = (i + 1) * blk
        st = lax.dot_general(k[:nk], qi, contract_last, preferred_element_type=F32)
        parts = []
        if i > topk:
            gm = jnp.where(sub < i, gate_t[:, i * blk:(i + 1) * blk], neg_inf)
            rank = jnp.zeros((nbp, blk), I32)
            for m in range(i):
                gm_m = gm[m:m + 1, :]
                beats = (gm_m > gm) | ((gm_m == gm) & (sub > m))
                rank = rank + beats.astype(I32)
            bias_t = jnp.where((sub < i) & (rank < topk), 0.0, neg_inf).astype(F32)
            for n in range(i):
                parts.append(st[n * blk:(n + 1) * blk] + bias_t[n:n + 1, :])
        elif i > 0:
            parts.append(st[:i * blk])
        parts.append(jnp.where(causal_t, st[i * blk:], neg_inf))
        sm = jnp.concatenate(parts, axis=0) if len(parts) > 1 else parts[0]
        m = jnp.max(sm, axis=0, keepdims=True)
        p = jnp.exp2(sm - m).astype(BF16)
        ot = jnp.dot(vt[:, :nk], p, preferred_element_type=F32)
        outs.append((ot[:dh] * (1.0 / ot[dh:dh + 1])).T)
    return outs


def _attn_body(q_ref, k_ref, v_ref, o_ref, *, seq, blk, topk, dh):
    for hh in range(q_ref.shape[1] // dh):
        c0 = hh * dh
        outs = _attn_head(q_ref[:, c0:c0 + dh], k_ref[:, c0:c0 + dh], v_ref[:, c0:c0 + dh],
                          seq=seq, blk=blk, topk=topk)
        for i, o in enumerate(outs):
            o_ref[i * blk:(i + 1) * blk, c0:c0 + dh] = o.astype(o_ref.dtype)


def _moba_attention(u, *, batch, seq, n_heads, head_dim, blk, topk, hps):
    T = u.shape[0]
    assert seq % blk == 0 and blk % LANES == 0 and n_heads % hps == 0
    n_hb = n_heads // hps
    body = functools.partial(_attn_body, seq=seq, blk=blk, topk=topk, dh=head_dim)
    return pl.pallas_call(
        body,
        grid=(batch, n_hb),
        in_specs=[
            pl.BlockSpec((seq, hps * head_dim), lambda b, h: (b, h)),
            pl.BlockSpec((seq, hps * head_dim), lambda b, h: (b, n_hb + h)),
            pl.BlockSpec((seq, hps * head_dim), lambda b, h: (b, 2 * n_hb + h)),
        ],
        out_specs=pl.BlockSpec((seq, hps * head_dim), lambda b, h: (b, h)),
        out_shape=jax.ShapeDtypeStruct((T, n_heads * head_dim), BF16),
        compiler_params=_cparams(("parallel", "parallel")),
        name="moba_attention",
    )(u, u, u)


def _conv_body(x_ref, w_ref, bdw_ref, g_ref, b_ref, o_ref, win_ref, conv_ref,
               *, width, ts, rt, halo, eps):
    ng = win_ref.shape[0]
    sub = SUBLANES

    @pl.when(pl.program_id(1) == 0)
    def _zero_halo():
        win_ref[:, 0:halo * sub, :] = jnp.zeros((ng, halo * sub, LANES), F32)

    for g in range(ng):
        for s in range(sub):
            c0 = (g * sub + s) * LANES
            win_ref[g, pl.ds(halo * sub + s, ts, stride=sub), :] = x_ref[:, c0:c0 + LANES].astype(F32)

    first = halo - (width - 1)
    for g in range(ng):
        def chunk(r, carry, g=g):
            t0 = r * rt
            acc = jnp.broadcast_to(bdw_ref[g][None], (rt, sub, LANES))
            for j in range(width):
                start = pl.multiple_of((t0 + first + j) * sub, sub)
                slab = win_ref[g, pl.ds(start, rt * sub), :].reshape(rt, sub, LANES)
                acc = acc + slab * w_ref[g, j][None]
            conv_ref[g, pl.ds(pl.multiple_of(t0 * sub, sub), rt * sub), :] = acc.reshape(rt * sub, LANES)
            return carry

        lax.fori_loop(0, ts // rt, chunk, 0)

    win_ref[:, 0:halo * sub, :] = win_ref[:, ts * sub:(ts + halo) * sub, :]

    def channel_chunks():
        for g in range(ng):
            for s in range(sub):
                yield (g * sub + s) * LANES, conv_ref[g, pl.ds(s, ts, stride=sub), :]

    n = ng * sub * LANES
    total = jnp.zeros((ts, 1), F32)
    for _, y in channel_chunks():
        total = total + jnp.sum(y, axis=1, keepdims=True)
    mu = total * (1.0 / n)
    sq = jnp.zeros((ts, 1), F32)
    for _, y in channel_chunks():
        d = y - mu
        sq = sq + jnp.sum(d * d, axis=1, keepdims=True)
    inv = lax.rsqrt(sq * (1.0 / n) + eps)
    for c0, y in channel_chunks():
        z = (y - mu) * inv * g_ref[:, c0:c0 + LANES] + b_ref[:, c0:c0 + LANES]
        o_ref[:, c0:c0 + LANES] = (z * _sigmoid(z)).astype(o_ref.dtype)


def _conv_branch(u, w_dw, b_dw, ln_g, ln_b, *, batch, seq, conv_c, col_block, ts, rt, halo):
    T = u.shape[0]
    width = w_dw.shape[0]
    gw = SUBLANES * LANES
    assert seq % ts == 0 and conv_c % gw == 0 and ts % rt == 0
    assert halo >= width - 1 and ts >= halo
    ng = conv_c // gw
    w_t = w_dw.reshape(width, ng, SUBLANES, LANES).transpose(1, 0, 2, 3)
    n_s = seq // ts
    body = functools.partial(_conv_body, width=width, ts=ts, rt=rt, halo=halo, eps=LN_EPS)
    return pl.pallas_call(
        body,
        grid=(batch, n_s),
        in_specs=[
            pl.BlockSpec((ts, conv_c), lambda b, s: (b * n_s + s, col_block)),
            pl.BlockSpec((ng, width, SUBLANES, LANES), lambda b, s: (0, 0, 0, 0)),
            pl.BlockSpec((ng, SUBLANES, LANES), lambda b, s: (0, 0, 0)),
            pl.BlockSpec((1, conv_c), lambda b, s: (0, 0)),
            pl.BlockSpec((1, conv_c), lambda b, s: (0, 0)),
        ],
        out_specs=pl.BlockSpec((ts, conv_c), lambda b, s: (b * n_s + s, 0)),
        out_shape=jax.ShapeDtypeStruct((T, conv_c), BF16),
        scratch_shapes=[pltpu.VMEM((ng, (halo + ts) * SUBLANES, LANES), F32),
                        pltpu.VMEM((ng, ts * SUBLANES, LANES), F32)],
        compiler_params=_cparams(("parallel", "arbitrary")),
        name="conv_branch",
    )(u, w_t, b_dw.reshape(ng, SUBLANES, LANES), ln_g[None, :], ln_b[None, :])


def _post_body(o_ref, hc_ref, ga_ref, gb_ref, x_ref, wo_ref, wp_ref, bp_ref, wout_ref,
               g1_ref, b1_ref, wrh_ref, wrl_ref, br_ref, h_ref, hp_ref, rt_ref,
               *, alpha, eps, n_groups, e_per_group):
    ya = jnp.dot(o_ref[...], wo_ref[...], preferred_element_type=F32)
    yc = jnp.dot(hc_ref[...], wp_ref[...], preferred_element_type=F32) + bp_ref[...]
    m = ga_ref[...].astype(F32) * ya + gb_ref[...].astype(F32) * yc
    z = alpha * x_ref[...] + jnp.dot(m.astype(BF16), wout_ref[...], preferred_element_type=F32)
    mu = jnp.mean(z, axis=1, keepdims=True)
    zc = z - mu
    var = jnp.mean(zc * zc, axis=1, keepdims=True)
    h = zc * lax.rsqrt(var + eps) * g1_ref[...] + b1_ref[...]
    h_ref[...] = h
    half = h.shape[1] // 2
    _store_tile_rows(hp_ref, (), _pack_bf16_pair(h[:, :half], h[:, half:]))

    h_hi = h.astype(BF16)
    h_lo = (h - h_hi.astype(F32)).astype(BF16)
    logits = (jnp.dot(h_hi, wrh_ref[...], preferred_element_type=F32)
              + jnp.dot(h_lo, wrh_ref[...], preferred_element_type=F32)
              + jnp.dot(h_hi, wrl_ref[...], preferred_element_type=F32)
              + br_ref[...])
    tm = logits.shape[0]
    lane = lax.broadcasted_iota(I32, (tm, LANES), 1)
    neg_inf = jnp.float32(-jnp.inf)
    big = jnp.int32(LANES)

    def first_argmax(vals):
        top = jnp.max(vals, axis=1, keepdims=True)
        idx = jnp.min(jnp.where(vals == top, lane, big), axis=1, keepdims=True)
        return top, idx

    gl = jnp.where(lane < n_groups, logits, neg_inf)
    gmax, grp = first_argmax(gl)
    grp_w = 1.0 / jnp.sum(jnp.exp(gl - gmax), axis=1, keepdims=True)
    lo_lane = n_groups + grp * e_per_group
    el = jnp.where((lane >= lo_lane) & (lane < lo_lane + e_per_group), logits, neg_inf)
    v1, i1 = first_argmax(el)
    v2, i2 = first_argmax(jnp.where(lane == i1, neg_inf, el))
    t = jnp.exp(v2 - v1)
    p1 = 1.0 / (1.0 + t)
    c1 = p1 * grp_w
    c2 = (t * p1) * grp_w
    e1 = (i1 - n_groups).astype(F32)
    e2 = (i2 - n_groups).astype(F32)
    rt_ref[...] = jnp.where(lane == 0, e1, jnp.where(lane == 1, e2,
                            jnp.where(lane == 2, c1, jnp.where(lane == 3, c2, 0.0))))


def _post_block(o, hc, u, x2, wo, wp, bp, wout, g1, b1, wr_hi, wr_lo, br, *, gate_block,
                alpha, n_groups, e_per_group, tm):
    T, D = x2.shape
    A = o.shape[1]
    C = hc.shape[1]
    assert T % tm == 0 and D == 2 * SUBLANES * LANES, "packed rows are stored as one (8, 128) tile"
    assert n_groups * (1 + e_per_group) <= LANES
    const = lambda i: (0, 0)
    resident = lambda shape: pl.BlockSpec(shape, const, pipeline_mode=pl.Buffered(1))
    body = functools.partial(_post_body, alpha=alpha, eps=LN_EPS, n_groups=n_groups,
                             e_per_group=e_per_group)
    return pl.pallas_call(
        body,
        grid=(T // tm,),
        in_specs=[
            pl.BlockSpec((tm, A), lambda i: (i, 0)),
            pl.BlockSpec((tm, C), lambda i: (i, 0)),
            pl.BlockSpec((tm, D), lambda i: (i, gate_block)),
            pl.BlockSpec((tm, D), lambda i: (i, gate_block + 1)),
            pl.BlockSpec((tm, D), lambda i: (i, 0)),
            resident((A, D)),
            resident((C, D)),
            resident((1, D)),
            resident((D, D)),
            resident((1, D)),
            resident((1, D)),
            resident((D, LANES)),
            resident((D, LANES)),
            resident((1, LANES)),
        ],
        out_specs=[
            pl.BlockSpec((tm, D), lambda i: (i, 0)),
            pl.BlockSpec((tm * SUBLANES, LANES), lambda i: (i, 0)),
            pl.BlockSpec((tm, LANES), lambda i: (i, 0)),
        ],
        out_shape=[
            jax.ShapeDtypeStruct((T, D), F32),
            jax.ShapeDtypeStruct((T * SUBLANES, LANES), U32),
            jax.ShapeDtypeStruct((T, LANES), F32),
        ],
        compiler_params=_cparams(("parallel",)),
        name="merge_project_route",
    )(o, hc, u, u, x2, wo, wp, bp, wout, g1, b1, wr_hi, wr_lo, br)


def _row_copy(src_ref, src_row8, dst_ref, dst_row8, sem):
    aligned = lambda v: v if isinstance(v, int) else pl.multiple_of(v, SUBLANES)
    return pltpu.make_async_copy(src_ref.at[pl.ds(aligned(src_row8), SUBLANES)],
                                 dst_ref.at[pl.ds(aligned(dst_row8), SUBLANES)], sem)


def _moe_body(es_ref, ec_ref, plan_ref, hp_ref, w1_ref, w3_ref, w2_ref, y2_ref,
              w1b, w3b, w2b, xbuf, ybuf, idx, gsem, ssem, isem, *, n_experts, nch, bm, spare_row0):
    g = pl.program_id(0)
    c = pl.program_id(1)
    fc = w1_ref.shape[1]
    half = SUBLANES * LANES
    n_total = es_ref[n_experts - 1] + ec_ref[n_experts - 1]

    def idx_copy(q):
        return pltpu.make_async_copy(plan_ref.at[q], idx.at[q % 3], isem.at[q % 3])

    def gather_start(q):
        for r in range(bm):
            _row_copy(hp_ref, idx[q % 3, 0, r], xbuf.at[q % 3], r * SUBLANES, gsem.at[q % 3]).start()

    def gather_wait(q):
        for r in range(bm):
            _row_copy(hp_ref, 0, xbuf.at[q % 3], r * SUBLANES, gsem.at[q % 3]).wait()

    def scatter_start(q):
        for r in range(bm):
            _row_copy(ybuf.at[q % 2], r * SUBLANES, y2_ref, idx[q % 3, 1, r], ssem.at[q % 2]).start()

    def scatter_wait(q):
        for r in range(bm):
            _row_copy(ybuf.at[q % 2], r * SUBLANES, y2_ref, 0, ssem.at[q % 2]).wait()

    @pl.when((g == 0) & (c == 0))
    def _prime():
        ybuf[...] = jnp.zeros(ybuf.shape, U32)
        spare = [pltpu.make_async_copy(
            ybuf.at[s], y2_ref.at[pl.ds((spare_row0 + s * bm) * SUBLANES, bm * SUBLANES)], ssem.at[s])
            for s in range(2)]
        for cp in spare:
            cp.start()
        for cp in spare:
            cp.wait()
        for q in range(3):
            idx_copy(q).start()
        for q in range(2):
            idx_copy(q).wait()
            gather_start(q)

    @pl.when(g < n_experts)
    def _cast_next_expert_chunk():
        slot = g % 2
        w1b[slot, c] = w1_ref[...].astype(BF16)
        w3b[slot, c] = w3_ref[...].astype(BF16)
        w2b[slot, pl.ds(pl.multiple_of(c * fc, fc), fc), :] = w2_ref[...].astype(BF16)

    @pl.when(g >= 1)
    def _compute_previous_expert_share():
        e = g - 1
        slot = e % 2
        nb = ec_ref[e]
        base = es_ref[e]

        def block(r, carry):
            q = base + r
            gather_wait(q)

            @pl.when(q >= 2)
            def _():
                scatter_wait(q - 2)

            u = _load_tile_rows(xbuf, (q % 3,), bm)
            x_lo = _unpack_lo(u).astype(BF16)
            x_hi = _unpack_hi(u).astype(BF16)
            hs = []
            for cc in range(nch):
                a = (jnp.dot(x_lo, w1b[slot, cc, :half, :], preferred_element_type=F32)
                     + jnp.dot(x_hi, w1b[slot, cc, half:, :], preferred_element_type=F32))
                b = (jnp.dot(x_lo, w3b[slot, cc, :half, :], preferred_element_type=F32)
                     + jnp.dot(x_hi, w3b[slot, cc, half:, :], preferred_element_type=F32))
                hs.append((a * _sigmoid(a) * b).astype(BF16))
            y = jnp.dot(jnp.concatenate(hs, axis=1), w2b[slot], preferred_element_type=F32)
            _store_tile_rows(ybuf, (q % 2,), _pack_bf16_pair(y[:, :half], y[:, half:]))
            scatter_start(q)
            idx_copy(q + 2).wait()
            gather_start(q + 2)
            idx_copy(q + 3).start()
            return carry

        lax.fori_loop((nb * c) // nch, (nb * (c + 1)) // nch, block, 0)

    @pl.when((g == n_experts) & (c == nch - 1))
    def _drain():
        idx_copy(n_total + 2).wait()
        gather_wait(n_total)
        gather_wait(n_total + 1)

        @pl.when(n_total >= 1)
        def _():
            scatter_wait(n_total - 1)

        @pl.when(n_total >= 2)
        def _():
            scatter_wait(n_total - 2)


def _moe_experts(eb_start, eb_count, plan, hp, w1, w3, w2, *, n_tokens, bm, nch):
    n_out_rows = 2 * n_tokens + 2 * bm
    E, D, F = w1.shape
    assert D == 2 * SUBLANES * LANES and hp.shape[1] == LANES
    assert F % nch == 0 and plan.shape[1:] == (2, bm)
    fc = F // nch

    def w_in_idx(g, c, es, ec):
        return (jnp.minimum(g, E - 1), 0, jnp.where(g < E, c, nch - 1))

    def w_out_idx(g, c, es, ec):
        return (jnp.minimum(g, E - 1), jnp.where(g < E, c, nch - 1), 0)

    grid_spec = pltpu.PrefetchScalarGridSpec(
        num_scalar_prefetch=2,
        grid=(E + 1, nch),
        in_specs=[
            pl.BlockSpec(memory_space=pl.ANY),
            pl.BlockSpec(memory_space=pl.ANY),
            pl.BlockSpec((None, D, fc), w_in_idx),
            pl.BlockSpec((None, D, fc), w_in_idx),
            pl.BlockSpec((None, fc, D), w_out_idx),
        ],
        out_specs=pl.BlockSpec(memory_space=pl.ANY),
        scratch_shapes=[
            pltpu.VMEM((2, nch, D, fc), BF16),
            pltpu.VMEM((2, nch, D, fc), BF16),
            pltpu.VMEM((2, F, D), BF16),
            pltpu.VMEM((3, bm * SUBLANES, LANES), U32),
            pltpu.VMEM((2, bm * SUBLANES, LANES), U32),
            pltpu.SMEM((3, 2, bm), I32),
            pltpu.SemaphoreType.DMA((3,)),
            pltpu.SemaphoreType.DMA((2,)),
            pltpu.SemaphoreType.DMA((3,)),
        ],
    )
    body = functools.partial(_moe_body, n_experts=E, nch=nch, bm=bm, spare_row0=2 * n_tokens)
    return pl.pallas_call(
        body,
        grid_spec=grid_spec,
        out_shape=jax.ShapeDtypeStruct((n_out_rows * SUBLANES, LANES), U32),
        compiler_params=_cparams(("arbitrary", "arbitrary")),
        name="moe_experts",
    )(eb_start, eb_count, plan, hp, w1, w3, w2)


def _combine_body(h_ref, rt_ref, g_ref, b_ref, y0_ref, y1_ref, o_ref, *, alpha, eps):
    tt = h_ref.shape[0]
    u0 = _load_tile_rows(y0_ref, (), tt)
    u1 = _load_tile_rows(y1_ref, (), tt)
    half = u0.shape[1]
    c0 = rt_ref[:, 2:3]
    c1 = rt_ref[:, 3:4]
    z_lo = alpha * h_ref[:, :half] + (_unpack_lo(u0) * c0 + _unpack_lo(u1) * c1)
    z_hi = alpha * h_ref[:, half:] + (_unpack_hi(u0) * c0 + _unpack_hi(u1) * c1)
    n = 2 * half
    mu = (jnp.sum(z_lo, axis=1, keepdims=True) + jnp.sum(z_hi, axis=1, keepdims=True)) * (1.0 / n)
    d_lo = z_lo - mu
    d_hi = z_hi - mu
    var = (jnp.sum(d_lo * d_lo, axis=1, keepdims=True)
           + jnp.sum(d_hi * d_hi, axis=1, keepdims=True)) * (1.0 / n)
    inv = lax.rsqrt(var + eps)
    o_ref[:, :half] = d_lo * inv * g_ref[:, :half] + b_ref[:, :half]
    o_ref[:, half:] = d_hi * inv * g_ref[:, half:] + b_ref[:, half:]


def _combine(h, rt, y2, g2, b2, *, alpha, tt):
    T, D = h.shape
    assert T % tt == 0 and D == 2 * SUBLANES * LANES and y2.shape[1] == LANES
    body = functools.partial(_combine_body, alpha=alpha, eps=LN_EPS)
    return pl.pallas_call(
        body,
        grid=(T // tt,),
        in_specs=[
            pl.BlockSpec((tt, D), lambda i: (i, 0)),
            pl.BlockSpec((tt, LANES), lambda i: (i, 0)),
            pl.BlockSpec((1, D), lambda i: (0, 0)),
            pl.BlockSpec((1, D), lambda i: (0, 0)),
            pl.BlockSpec((tt * SUBLANES, LANES), lambda i: (i, 0)),
            pl.BlockSpec((tt * SUBLANES, LANES), lambda i: (i + T // tt, 0)),
        ],
        out_specs=pl.BlockSpec((tt, D), lambda i: (i, 0)),
        out_shape=jax.ShapeDtypeStruct((T, D), F32),
        compiler_params=_cparams(("parallel",)),
        name="moe_combine",
    )(h, rt, g2, b2, y2, y2)


def _slot_plan(expert, n_experts, bm, n_plan_blocks):
    T = expert.shape[0]
    e_flat = expert.reshape(-1)
    onehot = (e_flat[:, None] == jnp.arange(n_experts, dtype=I32)[None, :]).astype(I32)
    csum = jnp.cumsum(onehot, axis=0)
    counts = csum[-1]
    eb_count = (counts + bm - 1) // bm
    eb_start = jnp.cumsum(eb_count) - eb_count
    slot = jnp.sum(onehot * (csum - 1 + (eb_start * bm)[None, :]), axis=1)
    n_slots = n_plan_blocks * bm
    tok = jnp.arange(2 * T, dtype=I32) // 2
    k = jnp.arange(2 * T, dtype=I32) % 2
    src = jnp.zeros((n_slots,), I32).at[slot].set(tok, unique_indices=True)
    p = jnp.arange(n_slots, dtype=I32)
    spare = 2 * T + ((p // bm) % 2) * bm + p % bm
    dst = spare.at[slot].set(k * T + tok, unique_indices=True)
    plan = jnp.stack([src.reshape(n_plan_blocks, bm), dst.reshape(n_plan_blocks, bm)], axis=1)
    return eb_start.astype(I32), eb_count.astype(I32), plan * SUBLANES


def _layer(x2, p, *, batch, seq, depth, n_heads, head_dim, moba_block, moba_topk,
           n_groups, e_per_group, tiles):
    T, D = x2.shape
    A = n_heads * head_dim
    C = p["w_dw"].shape[1]
    alpha = (2.0 * depth) ** 0.25
    tn = tiles["proj_tn"]
    assert A == C == D, "column-block addressing below assumes equal branch widths"

    w_in, b_in = p["w_in"], p["b_in"]
    w_main = jnp.concatenate([w_in[:, :3 * A + C], w_in[:, 3 * A + 2 * C:]], axis=1).astype(BF16)
    b_main = jnp.concatenate([b_in[:3 * A + C], b_in[3 * A + 2 * C:]])[None, :]
    w_g = w_in[:, 3 * A + C:3 * A + 2 * C].astype(BF16)
    b_g = b_in[3 * A + C:3 * A + 2 * C][None, :]

    half = head_dim // 2
    inv_freq = jnp.power(ROPE_THETA, -jnp.arange(half, dtype=F32) * (2.0 / head_dim))
    ang = jnp.arange(seq, dtype=F32)[:, None] * inv_freq[None, :]
    cos2 = jnp.concatenate([jnp.cos(ang), jnp.cos(ang)], axis=1)
    sin_s = jnp.concatenate([-jnp.sin(ang), jnp.sin(ang)], axis=1)

    u = _in_projection(x2, w_main, w_g, b_main, b_g, cos2, sin_s, seq=seq, attn_w=A, conv_c=C,
                       head_dim=head_dim, tm=tiles["proj_tm"], tn=tn)
    o = _moba_attention(u, batch=batch, seq=seq, n_heads=n_heads, head_dim=head_dim,
                        blk=moba_block, topk=moba_topk, hps=ATTN_HPS)
    hc = _conv_branch(u, p["w_dw"], p["b_dw"], p["conv_ln_g"], p["conv_ln_b"], batch=batch,
                      seq=seq, conv_c=C, col_block=3, ts=tiles["conv_ts"], rt=tiles["conv_rt"],
                      halo=CONV_HALO)

    n_experts = n_groups * e_per_group
    n_route = n_groups + n_experts
    w_route = jnp.concatenate(
        [p["w_rg"], p["w_re"].transpose(1, 0, 2).reshape(D, n_experts),
         jnp.zeros((D, LANES - n_route), F32)], axis=1)
    b_route = jnp.concatenate(
        [p["b_rg"], p["b_re"].reshape(n_experts), jnp.zeros((LANES - n_route,), F32)])[None, :]
    wr_hi = w_route.astype(BF16)
    wr_lo = (w_route - wr_hi.astype(F32)).astype(BF16)

    h, hp, rt = _post_block(
        o, hc, u, x2, p["w_o_attn"].astype(BF16), p["w_pw2"].astype(BF16), p["b_pw2"][None, :],
        p["w_out"].astype(BF16), p["ln1_g"][None, :], p["ln1_b"][None, :], wr_hi, wr_lo, b_route,
        gate_block=4, alpha=alpha, n_groups=n_groups, e_per_group=e_per_group,
        tm=tiles["post_tm"])

    bm = tiles["moe_bm"]
    n_plan_blocks = -(-2 * T // bm) + n_experts + 3
    expert = rt[:, :2].astype(I32)
    eb_start, eb_count, plan = _slot_plan(expert, n_experts, bm, n_plan_blocks)
    y2 = _moe_experts(eb_start, eb_count, plan, hp, p["w1"], p["w3"], p["w2"],
                      n_tokens=T, bm=bm, nch=tiles["moe_nch"])
    return _combine(h, rt, y2, p["ln2_g"][None, :], p["ln2_b"][None, :], alpha=alpha,
                    tt=tiles["comb_tt"])


_PARAM_NAMES = ("w_in", "b_in", "w_o_attn", "w_dw", "b_dw", "conv_ln_g", "conv_ln_b", "w_pw2",
                "b_pw2", "w_out", "ln1_g", "ln1_b", "w_rg", "b_rg", "w_re", "b_re", "w1", "w3",
                "w2", "ln2_g", "ln2_b")

_TILES = dict(proj_tm=PROJ_TM, proj_tn=PROJ_TN, conv_ts=CONV_TS, conv_rt=CONV_RT,
              post_tm=POST_TM, comb_tt=COMB_TT, moe_bm=MOE_BM, moe_nch=MOE_NCH)


def _forward(x, params, *, n_heads=N_HEADS, head_dim=HEAD_DIM, moba_block=MOBA_BLOCK,
             moba_topk=MOBA_TOPK, n_groups=N_GROUPS, e_per_group=EXPERTS_PER_GROUP, tiles=None):
    tiles = dict(_TILES, **(tiles or {}))
    B, S, D = x.shape
    depth = params["w_in"].shape[0]
    x2 = x.reshape(B * S, D)
    for l in range(depth):
        p = {k: v[l] for k, v in params.items()}
        x2 = _layer(x2, p, batch=B, seq=S, depth=depth, n_heads=n_heads, head_dim=head_dim,
                    moba_block=moba_block, moba_topk=moba_topk, n_groups=n_groups,
                    e_per_group=e_per_group, tiles=tiles)
    return x2.reshape(B, S, D)


def kernel(x, w_in, b_in, w_o_attn, w_dw, b_dw, conv_ln_g, conv_ln_b, w_pw2, b_pw2, w_out, ln1_g,
           ln1_b, w_rg, b_rg, w_re, b_re, w1, w3, w2, ln2_g, ln2_b):
    params = dict(zip(_PARAM_NAMES, (w_in, b_in, w_o_attn, w_dw, b_dw, conv_ln_g, conv_ln_b, w_pw2,
                                     b_pw2, w_out, ln1_g, ln1_b, w_rg, b_rg, w_re, b_re, w1, w3,
                                     w2, ln2_g, ln2_b)))
    return _forward(x, params)
```

```python
import functools

import jax
import jax.numpy as jnp
from jax import lax
from jax.experimental import pallas as pl
from jax.experimental.pallas import tpu as pltpu

F32 = jnp.float32
BF16 = jnp.bfloat16
U32 = jnp.uint32
I32 = jnp.int32

N_HEADS = 16
HEAD_DIM = 128
ROPE_THETA = 10000.0
MOBA_BLOCK = 256
MOBA_TOPK = 3
CONV_WIDTH = 31
N_GROUPS = 4
EXPERTS_PER_GROUP = 8
LN_EPS = 1e-5
LOG2_E = 1.4426950408889634

LANES = 128
SUBLANES = 8
VMEM_LIMIT = 56 * 1024 * 1024

PROJ_TM = 1024
PROJ_TN = 1024
ATTN_HPS = 2
ATTN_ONES_ROWS = 16
CONV_TS = 256
CONV_RT = 16
CONV_HALO = 32
POST_TM = 256
COMB_TT = 512
MOE_BM = 256
MOE_NCH = 4


def _cparams(sem):
    return pltpu.CompilerParams(dimension_semantics=sem, vmem_limit_bytes=VMEM_LIMIT)


def _sigmoid(x):
    return 1.0 / (1.0 + jnp.exp(-x))


def _pack_bf16_pair(lo_f32, hi_f32):
    lo = lax.bitcast_convert_type(lo_f32.astype(BF16).astype(F32), U32) >> 16
    hi = lax.bitcast_convert_type(hi_f32.astype(BF16).astype(F32), U32)
    return hi | lo


def _unpack_lo(u):
    return lax.bitcast_convert_type(u << 16, F32)


def _unpack_hi(u):
    return lax.bitcast_convert_type(u & jnp.uint32(0xFFFF0000), F32)


def _store_tile_rows(ref, lead, x):
    n = x.shape[0]
    for s in range(SUBLANES):
        ref[(*lead, pl.ds(s, n, stride=SUBLANES), slice(None))] = x[:, s * LANES:(s + 1) * LANES]


def _load_tile_rows(ref, lead, n):
    return jnp.concatenate(
        [ref[(*lead, pl.ds(s, n, stride=SUBLANES), slice(None))] for s in range(SUBLANES)], axis=1)


def _inproj_body(x_ref, w_ref, wg_ref, b_ref, bg_ref, cos_ref, sin_ref, o_ref, xb_ref,
                 *, n_q, n_qk, n_qkv, n_glu, scale, head_dim):
    j = pl.program_id(1)
    tn = o_ref.shape[1]

    @pl.when(j == 0)
    def _cast():
        xb_ref[...] = x_ref[...].astype(BF16)

    def project(w, b):
        return jnp.dot(xb_ref[...], w[...], preferred_element_type=F32) + b[...]

    @pl.when(j < n_qk)
    def _rope():
        acc = project(w_ref, b_ref)
        s = jnp.where(j < n_q, scale, 1.0).astype(F32)
        cos = cos_ref[...] * s
        sin = sin_ref[...] * s
        for h in range(tn // head_dim):
            t = acc[:, h * head_dim:(h + 1) * head_dim]
            r = pltpu.roll(t, head_dim // 2, axis=1)
            o_ref[:, h * head_dim:(h + 1) * head_dim] = (t * cos + r * sin).astype(o_ref.dtype)

    @pl.when((j >= n_qk) & (j < n_qkv))
    def _plain():
        o_ref[...] = project(w_ref, b_ref).astype(o_ref.dtype)

    @pl.when((j >= n_qkv) & (j < n_qkv + n_glu))
    def _glu():
        a = project(w_ref, b_ref)
        g = project(wg_ref, bg_ref)
        o_ref[...] = (a * _sigmoid(g)).astype(o_ref.dtype)

    @pl.when(j >= n_qkv + n_glu)
    def _gate():
        o_ref[...] = _sigmoid(project(w_ref, b_ref)).astype(o_ref.dtype)


def _in_projection(x2, w_all, b_all, cos2, sin_s, *, seq, attn_w, conv_c, head_dim, tm, tn):
    T, D = x2.shape
    W = w_all.shape[1] - conv_c
    assert T % tm == 0 and seq % tm == 0 and W % tn == 0
    assert attn_w % tn == 0 and conv_c % tn == 0 and tn % head_dim == 0
    assert head_dim == LANES
    n_q = attn_w // tn
    n_glu = conv_c // tn
    n_qkv = 3 * n_q
    pos_tiles = seq // tm

    def main_idx(i, j):
        return (0, jnp.where(j < n_qkv + n_glu, j, j + n_glu))

    def g_idx(i, j):
        return (0, n_qkv + n_glu + jnp.clip(j - n_qkv, 0, n_glu - 1))

    body = functools.partial(_inproj_body, n_q=n_q, n_qk=2 * n_q, n_qkv=n_qkv, n_glu=n_glu,
                             scale=float(head_dim) ** -0.5 * LOG2_E, head_dim=head_dim)
    return pl.pallas_call(
        body,
        grid=(T // tm, W // tn),
        in_specs=[
            pl.BlockSpec((tm, D), lambda i, j: (i, 0)),
            pl.BlockSpec((D, tn), main_idx),
            pl.BlockSpec((D, tn), g_idx),
            pl.BlockSpec((1, tn), main_idx),
            pl.BlockSpec((1, tn), g_idx),
            pl.BlockSpec((tm, head_dim), lambda i, j: (i % pos_tiles, 0)),
            pl.BlockSpec((tm, head_dim), lambda i, j: (i % pos_tiles, 0)),
        ],
        out_specs=pl.BlockSpec((tm, tn), lambda i, j: (i, j)),
        out_shape=jax.ShapeDtypeStruct((T, W), BF16),
        scratch_shapes=[pltpu.VMEM((tm, D), BF16)],
        compiler_params=_cparams(("parallel", "arbitrary")),
        name="in_projection",
    )(x2, w_all, w_all, b_all, b_all, cos2, sin_s)


def _attn_head(q, k, v, *, seq, blk, topk):
    nb = seq // blk
    nbp = -(-nb // SUBLANES) * SUBLANES
    dh = q.shape[1]
    contract_last = (((1,), (1,)), ((), ()))
    neg_inf = jnp.float32(-jnp.inf)

    kmean = jnp.mean(k.astype(F32).reshape(nb, blk, dh), axis=1)
    if nbp > nb:
        kmean = jnp.concatenate([kmean, jnp.zeros((nbp - nb, dh), F32)], axis=0)
    gate_t = lax.dot_general(kmean.astype(BF16), q, contract_last, preferred_element_type=F32)
    ones_rows = (lax.broadcasted_iota(I32, (ATTN_ONES_ROWS, seq), 0) == 0).astype(BF16)
    vt = jnp.concatenate([v.astype(F32).T.astype(BF16), ones_rows], axis=0)
    sub = lax.broadcasted_iota(I32, (nbp, blk), 0)
    key_r = lax.broadcasted_iota(I32, (blk, blk), 0)
    qry_c = lax.broadcasted_iota(I32, (blk, blk), 1)
    causal_t = key_r <= qry_c

    outs = []
    for i in range(nb):
        qi = q[i * blk:(i + 1) * blk]
        nk = (i + 1) * blk
        st = lax.dot_general(k[:nk], qi, contract_last, preferred_element_type=F32)
        parts = []
        if i > topk:
            gm = jnp.where(sub < i, gate_t[:, i * blk:(i + 1) * blk], neg_inf)
            rank = jnp.zeros((nbp, blk), I32)
            for m in range(i):
                gm_m = gm[m:m + 1, :]
                beats = (gm_m > gm) | ((gm_m == gm) & (sub > m))
                rank = rank + beats.astype(I32)
            bias_t = jnp.where((sub < i) & (rank < topk), 0.0, neg_inf).astype(F32)
            for n in range(i):
                parts.append(st[n * blk:(n + 1) * blk] + bias_t[n:n + 1, :])
        elif i > 0:
            parts.append(st[:i * blk])
        parts.append(jnp.where(causal_t, st[i * blk:], neg_inf))
        sm = jnp.concatenate(parts, axis=0) if len(parts) > 1 else parts[0]
        m = jnp.max(sm, axis=0, keepdims=True)
        p = jnp.exp2(sm - m).astype(BF16)
        ot = jnp.dot(vt[:, :nk], p, preferred_element_type=F32)
        outs.append((ot[:dh] * (1.0 / ot[dh:dh + 1])).T)
    return outs


def _attn_body(q_ref, k_ref, v_ref, o_ref, *, seq, blk, topk, dh):
    for hh in range(q_ref.shape[1] // dh):
        c0 = hh * dh
        outs = _attn_head(q_ref[:, c0:c0 + dh], k_ref[:, c0:c0 + dh], v_ref[:, c0:c0 + dh],
                          seq=seq, blk=blk, topk=topk)
        for i, o in enumerate(outs):
            o_ref[i * blk:(i + 1) * blk, c0:c0 + dh] = o.astype(o_ref.dtype)


def _moba_attention(u, *, batch, seq, n_heads, head_dim, blk, topk, hps):
    T = u.shape[0]
    assert seq % blk == 0 and blk % LANES == 0 and n_heads % hps == 0
    n_hb = n_heads // hps
    body = functools.partial(_attn_body, seq=seq, blk=blk, topk=topk, dh=head_dim)
    return pl.pallas_call(
        body,
        grid=(batch, n_hb),
        in_specs=[
            pl.BlockSpec((seq, hps * head_dim), lambda b, h: (b, h)),
            pl.BlockSpec((seq, hps * head_dim), lambda b, h: (b, n_hb + h)),
            pl.BlockSpec((seq, hps * head_dim), lambda b, h: (b, 2 * n_hb + h)),
        ],
        out_specs=pl.BlockSpec((seq, hps * head_dim), lambda b, h: (b, h)),
        out_shape=jax.ShapeDtypeStruct((T, n_heads * head_dim), BF16),
        compiler_params=_cparams(("parallel", "parallel")),
        name="moba_attention",
    )(u, u, u)


def _conv_body(x_ref, w_ref, bdw_ref, g_ref, b_ref, o_ref, win_ref, conv_ref,
               *, width, ts, rt, halo, eps):
    ng = win_ref.shape[0]
    sub = SUBLANES

    @pl.when(pl.program_id(1) == 0)
    def _zero_halo():
        win_ref[:, 0:halo * sub, :] = jnp.zeros((ng, halo * sub, LANES), F32)

    for g in range(ng):
        for s in range(sub):
            c0 = (g * sub + s) * LANES
            win_ref[g, pl.ds(halo * sub + s, ts, stride=sub), :] = x_ref[:, c0:c0 + LANES].astype(F32)

    first = halo - (width - 1)
    for g in range(ng):
        def chunk(r, carry, g=g):
            t0 = r * rt
            acc = jnp.broadcast_to(bdw_ref[g][None], (rt, sub, LANES))
            for j in range(width):
                start = pl.multiple_of((t0 + first + j) * sub, sub)
                slab = win_ref[g, pl.ds(start, rt * sub), :].reshape(rt, sub, LANES)
                acc = acc + slab * w_ref[g, j][None]
            conv_ref[g, pl.ds(pl.multiple_of(t0 * sub, sub), rt * sub), :] = acc.reshape(rt * sub, LANES)
            return carry

        lax.fori_loop(0, ts // rt, chunk, 0)

    win_ref[:, 0:halo * sub, :] = win_ref[:, ts * sub:(ts + halo) * sub, :]

    def channel_chunks():
        for g in range(ng):
            for s in range(sub):
                yield (g * sub + s) * LANES, conv_ref[g, pl.ds(s, ts, stride=sub), :]

    n = ng * sub * LANES
    total = jnp.zeros((ts, 1), F32)
    for _, y in channel_chunks():
        total = total + jnp.sum(y, axis=1, keepdims=True)
    mu = total * (1.0 / n)
    sq = jnp.zeros((ts, 1), F32)
    for _, y in channel_chunks():
        d = y - mu
        sq = sq + jnp.sum(d * d, axis=1, keepdims=True)
    inv = lax.rsqrt(sq * (1.0 / n) + eps)
    for c0, y in channel_chunks():
        z = (y - mu) * inv * g_ref[:, c0:c0 + LANES] + b_ref[:, c0:c0 + LANES]
        o_ref[:, c0:c0 + LANES] = (z * _sigmoid(z)).astype(o_ref.dtype)


def _conv_branch(u, w_dw, b_dw, ln_g, ln_b, *, batch, seq, conv_c, col_block, ts, rt, halo):
    T = u.shape[0]
    width = w_dw.shape[0]
    gw = SUBLANES * LANES
    assert seq % ts == 0 and conv_c % gw == 0 and ts % rt == 0
    assert halo >= width - 1 and ts >= halo
    ng = conv_c // gw
    w_t = w_dw.reshape(width, ng, SUBLANES, LANES).transpose(1, 0, 2, 3)
    n_s = seq // ts
    body = functools.partial(_conv_body, width=width, ts=ts, rt=rt, halo=halo, eps=LN_EPS)
    return pl.pallas_call(
        body,
        grid=(batch, n_s),
        in_specs=[
            pl.BlockSpec((ts, conv_c), lambda b, s: (b * n_s + s, col_block)),
            pl.BlockSpec((ng, width, SUBLANES, LANES), lambda b, s: (0, 0, 0, 0)),
            pl.BlockSpec((ng, SUBLANES, LANES), lambda b, s: (0, 0, 0)),
            pl.BlockSpec((1, conv_c), lambda b, s: (0, 0)),
            pl.BlockSpec((1, conv_c), lambda b, s: (0, 0)),
        ],
        out_specs=pl.BlockSpec((ts, conv_c), lambda b, s: (b * n_s + s, 0)),
        out_shape=jax.ShapeDtypeStruct((T, conv_c), BF16),
        scratch_shapes=[pltpu.VMEM((ng, (halo + ts) * SUBLANES, LANES), F32),
                        pltpu.VMEM((ng, ts * SUBLANES, LANES), F32)],
        compiler_params=_cparams(("parallel", "arbitrary")),
        name="conv_branch",
    )(u, w_t, b_dw.reshape(ng, SUBLANES, LANES), ln_g[None, :], ln_b[None, :])


def _post_body(o_ref, hc_ref, ga_ref, gb_ref, x_ref, wo_ref, wp_ref, bp_ref, wout_ref,
               g1_ref, b1_ref, wrh_ref, wrl_ref, br_ref, h_ref, hp_ref, rt_ref,
               *, alpha, eps, n_groups, e_per_group):
    ya = jnp.dot(o_ref[...], wo_ref[...], preferred_element_type=F32)
    yc = jnp.dot(hc_ref[...], wp_ref[...], preferred_element_type=F32) + bp_ref[...]
    m = ga_ref[...].astype(F32) * ya + gb_ref[...].astype(F32) * yc
    z = alpha * x_ref[...] + jnp.dot(m.astype(BF16), wout_ref[...], preferred_element_type=F32)
    mu = jnp.mean(z, axis=1, keepdims=True)
    zc = z - mu
    var = jnp.mean(zc * zc, axis=1, keepdims=True)
    h = zc * lax.rsqrt(var + eps) * g1_ref[...] + b1_ref[...]
    h_ref[...] = h
    half = h.shape[1] // 2
    _store_tile_rows(hp_ref, (), _pack_bf16_pair(h[:, :half], h[:, half:]))

    h_hi = h.astype(BF16)
    h_lo = (h - h_hi.astype(F32)).astype(BF16)
    logits = (jnp.dot(h_hi, wrh_ref[...], preferred_element_type=F32)
              + jnp.dot(h_lo, wrh_ref[...], preferred_element_type=F32)
              + jnp.dot(h_hi, wrl_ref[...], preferred_element_type=F32)
              + br_ref[...])
    tm = logits.shape[0]
    lane = lax.broadcasted_iota(I32, (tm, LANES), 1)
    neg_inf = jnp.float32(-jnp.inf)
    big = jnp.int32(LANES)

    def first_argmax(vals):
        top = jnp.max(vals, axis=1, keepdims=True)
        idx = jnp.min(jnp.where(vals == top, lane, big), axis=1, keepdims=True)
        return top, idx

    gl = jnp.where(lane < n_groups, logits, neg_inf)
    gmax, grp = first_argmax(gl)
    grp_w = 1.0 / jnp.sum(jnp.exp(gl - gmax), axis=1, keepdims=True)
    lo_lane = n_groups + grp * e_per_group
    el = jnp.where((lane >= lo_lane) & (lane < lo_lane + e_per_group), logits, neg_inf)
    v1, i1 = first_argmax(el)
    v2, i2 = first_argmax(jnp.where(lane == i1, neg_inf, el))
    t = jnp.exp(v2 - v1)
    p1 = 1.0 / (1.0 + t)
    c1 = p1 * grp_w
    c2 = (t * p1) * grp_w
    e1 = (i1 - n_groups).astype(F32)
    e2 = (i2 - n_groups).astype(F32)
    rt_ref[...] = jnp.where(lane == 0, e1, jnp.where(lane == 1, e2,
                            jnp.where(lane == 2, c1, jnp.where(lane == 3, c2, 0.0))))


def _post_block(o, hc, u, x2, wo, wp, bp, wout, g1, b1, wr_hi, wr_lo, br, *, gate_block,
                alpha, n_groups, e_per_group, tm):
    T, D = x2.shape
    A = o.shape[1]
    C = hc.shape[1]
    assert T % tm == 0 and D == 2 * SUBLANES * LANES, "packed rows are stored as one (8, 128) tile"
    assert n_groups * (1 + e_per_group) <= LANES
    const = lambda i: (0, 0)
    resident = lambda shape: pl.BlockSpec(shape, const, pipeline_mode=pl.Buffered(1))
    body = functools.partial(_post_body, alpha=alpha, eps=LN_EPS, n_groups=n_groups,
                             e_per_group=e_per_group)
    return pl.pallas_call(
        body,
        grid=(T // tm,),
        in_specs=[
            pl.BlockSpec((tm, A), lambda i: (i, 0)),
            pl.BlockSpec((tm, C), lambda i: (i, 0)),
            pl.BlockSpec((tm, D), lambda i: (i, gate_block)),
            pl.BlockSpec((tm, D), lambda i: (i, gate_block + 1)),
            pl.BlockSpec((tm, D), lambda i: (i, 0)),
            resident((A, D)),
            resident((C, D)),
            resident((1, D)),
            resident((D, D)),
            resident((1, D)),
            resident((1, D)),
            resident((D, LANES)),
            resident((D, LANES)),
            resident((1, LANES)),
        ],
        out_specs=[
            pl.BlockSpec((tm, D), lambda i: (i, 0)),
            pl.BlockSpec((tm * SUBLANES, LANES), lambda i: (i, 0)),
            pl.BlockSpec((tm, LANES), lambda i: (i, 0)),
        ],
        out_shape=[
            jax.ShapeDtypeStruct((T, D), F32),
            jax.ShapeDtypeStruct((T * SUBLANES, LANES), U32),
            jax.ShapeDtypeStruct((T, LANES), F32),
        ],
        compiler_params=_cparams(("parallel",)),
        name="merge_project_route",
    )(o, hc, u, u, x2, wo, wp, bp, wout, g1, b1, wr_hi, wr_lo, br)


def _row_copy(src_ref, src_row8, dst_ref, dst_row8, sem):
    aligned = lambda v: v if isinstance(v, int) else pl.multiple_of(v, SUBLANES)
    return pltpu.make_async_copy(src_ref.at[pl.ds(aligned(src_row8), SUBLANES)],
                                 dst_ref.at[pl.ds(aligned(dst_row8), SUBLANES)], sem)


def _moe_body(es_ref, ec_ref, plan_ref, hp_ref, w1_ref, w3_ref, w2_ref, y2_ref,
              w1b, w3b, w2b, xbuf, ybuf, idx, gsem, ssem, isem, *, n_experts, nch, bm, spare_row0):
    g = pl.program_id(0)
    c = pl.program_id(1)
    fc = w1_ref.shape[1]
    half = SUBLANES * LANES
    n_total = es_ref[n_experts - 1] + ec_ref[n_experts - 1]

    def idx_copy(q):
        return pltpu.make_async_copy(plan_ref.at[q], idx.at[q % 3], isem.at[q % 3])

    def gather_start(q):
        for r in range(bm):
            _row_copy(hp_ref, idx[q % 3, 0, r], xbuf.at[q % 3], r * SUBLANES, gsem.at[q % 3]).start()

    def gather_wait(q):
        for r in range(bm):
            _row_copy(hp_ref, 0, xbuf.at[q % 3], r * SUBLANES, gsem.at[q % 3]).wait()

    def scatter_start(q):
        for r in range(bm):
            _row_copy(ybuf.at[q % 2], r * SUBLANES, y2_ref, idx[q % 3, 1, r], ssem.at[q % 2]).start()

    def scatter_wait(q):
        for r in range(bm):
            _row_copy(ybuf.at[q % 2], r * SUBLANES, y2_ref, 0, ssem.at[q % 2]).wait()

    @pl.when((g == 0) & (c == 0))
    def _prime():
        ybuf[...] = jnp.zeros(ybuf.shape, U32)
        spare = [pltpu.make_async_copy(
            ybuf.at[s], y2_ref.at[pl.ds((spare_row0 + s * bm) * SUBLANES, bm * SUBLANES)], ssem.at[s])
            for s in range(2)]
        for cp in spare:
            cp.start()
        for cp in spare:
            cp.wait()
        for q in range(3):
            idx_copy(q).start()
        for q in range(2):
            idx_copy(q).wait()
            gather_start(q)

    @pl.when(g < n_experts)
    def _cast_next_expert_chunk():
        slot = g % 2
        w1b[slot, c] = w1_ref[...].astype(BF16)
        w3b[slot, c] = w3_ref[...].astype(BF16)
        w2b[slot, pl.ds(pl.multiple_of(c * fc, fc), fc), :] = w2_ref[...].astype(BF16)

    @pl.when(g >= 1)
    def _compute_previous_expert_share():
        e = g - 1
        slot = e % 2
        nb = ec_ref[e]
        base = es_ref[e]

        def block(r, carry):
            q = base + r
            gather_wait(q)

            @pl.when(q >= 2)
            def _():
                scatter_wait(q - 2)

            u = _load_tile_rows(xbuf, (q % 3,), bm)
            x_lo = _unpack_lo(u).astype(BF16)
            x_hi = _unpack_hi(u).astype(BF16)
            hs = []
            for cc in range(nch):
                a = (jnp.dot(x_lo, w1b[slot, cc, :half, :], preferred_element_type=F32)
                     + jnp.dot(x_hi, w1b[slot, cc, half:, :], preferred_element_type=F32))
                b = (jnp.dot(x_lo, w3b[slot, cc, :half, :], preferred_element_type=F32)
                     + jnp.dot(x_hi, w3b[slot, cc, half:, :], preferred_element_type=F32))
                hs.append((a * _sigmoid(a) * b).astype(BF16))
            y = jnp.dot(jnp.concatenate(hs, axis=1), w2b[slot], preferred_element_type=F32)
            _store_tile_rows(ybuf, (q % 2,), _pack_bf16_pair(y[:, :half], y[:, half:]))
            scatter_start(q)
            idx_copy(q + 2).wait()
            gather_start(q + 2)
            idx_copy(q + 3).start()
            return carry

        lax.fori_loop((nb * c) // nch, (nb * (c + 1)) // nch, block, 0)

    @pl.when((g == n_experts) & (c == nch - 1))
    def _drain():
        idx_copy(n_total + 2).wait()
        gather_wait(n_total)
        gather_wait(n_total + 1)

        @pl.when(n_total >= 1)
        def _():
            scatter_wait(n_total - 1)

        @pl.when(n_total >= 2)
        def _():
            scatter_wait(n_total - 2)


def _moe_experts(eb_start, eb_count, plan, hp, w1, w3, w2, *, n_tokens, bm, nch):
    n_out_rows = 2 * n_tokens + 2 * bm
    E, D, F = w1.shape
    assert D == 2 * SUBLANES * LANES and hp.shape[1] == LANES
    assert F % nch == 0 and plan.shape[1:] == (2, bm)
    fc = F // nch

    def w_in_idx(g, c, es, ec):
        return (jnp.minimum(g, E - 1), 0, jnp.where(g < E, c, nch - 1))

    def w_out_idx(g, c, es, ec):
        return (jnp.minimum(g, E - 1), jnp.where(g < E, c, nch - 1), 0)

    grid_spec = pltpu.PrefetchScalarGridSpec(
        num_scalar_prefetch=2,
        grid=(E + 1, nch),
        in_specs=[
            pl.BlockSpec(memory_space=pl.ANY),
            pl.BlockSpec(memory_space=pl.ANY),
            pl.BlockSpec((None, D, fc), w_in_idx),
            pl.BlockSpec((None, D, fc), w_in_idx),
            pl.BlockSpec((None, fc, D), w_out_idx),
        ],
        out_specs=pl.BlockSpec(memory_space=pl.ANY),
        scratch_shapes=[
            pltpu.VMEM((2, nch, D, fc), BF16),
            pltpu.VMEM((2, nch, D, fc), BF16),
            pltpu.VMEM((2, F, D), BF16),
            pltpu.VMEM((3, bm * SUBLANES, LANES), U32),
            pltpu.VMEM((2, bm * SUBLANES, LANES), U32),
            pltpu.SMEM((3, 2, bm), I32),
            pltpu.SemaphoreType.DMA((3,)),
            pltpu.SemaphoreType.DMA((2,)),
            pltpu.SemaphoreType.DMA((3,)),
        ],
    )
    body = functools.partial(_moe_body, n_experts=E, nch=nch, bm=bm, spare_row0=2 * n_tokens)
    return pl.pallas_call(
        body,
        grid_spec=grid_spec,
        out_shape=jax.ShapeDtypeStruct((n_out_rows * SUBLANES, LANES), U32),
        compiler_params=_cparams(("arbitrary", "arbitrary")),
        name="moe_experts",
    )(eb_start, eb_count, plan, hp, w1, w3, w2)


def _combine_body(h_ref, rt_ref, g_ref, b_ref, y0_ref, y1_ref, o_ref, *, alpha, eps):
    tt = h_ref.shape[0]
    u0 = _load_tile_rows(y0_ref, (), tt)
    u1 = _load_tile_rows(y1_ref, (), tt)
    half = u0.shape[1]
    c0 = rt_ref[:, 2:3]
    c1 = rt_ref[:, 3:4]
    z_lo = alpha * h_ref[:, :half] + (_unpack_lo(u0) * c0 + _unpack_lo(u1) * c1)
    z_hi = alpha * h_ref[:, half:] + (_unpack_hi(u0) * c0 + _unpack_hi(u1) * c1)
    n = 2 * half
    mu = (jnp.sum(z_lo, axis=1, keepdims=True) + jnp.sum(z_hi, axis=1, keepdims=True)) * (1.0 / n)
    d_lo = z_lo - mu
    d_hi = z_hi - mu
    var = (jnp.sum(d_lo * d_lo, axis=1, keepdims=True)
           + jnp.sum(d_hi * d_hi, axis=1, keepdims=True)) * (1.0 / n)
    inv = lax.rsqrt(var + eps)
    o_ref[:, :half] = d_lo * inv * g_ref[:, :half] + b_ref[:, :half]
    o_ref[:, half:] = d_hi * inv * g_ref[:, half:] + b_ref[:, half:]


def _combine(h, rt, y2, g2, b2, *, alpha, tt):
    T, D = h.shape
    assert T % tt == 0 and D == 2 * SUBLANES * LANES and y2.shape[1] == LANES
    body = functools.partial(_combine_body, alpha=alpha, eps=LN_EPS)
    return pl.pallas_call(
        body,
        grid=(T // tt,),
        in_specs=[
            pl.BlockSpec((tt, D), lambda i: (i, 0)),
            pl.BlockSpec((tt, LANES), lambda i: (i, 0)),
            pl.BlockSpec((1, D), lambda i: (0, 0)),
            pl.BlockSpec((1, D), lambda i: (0, 0)),
            pl.BlockSpec((tt * SUBLANES, LANES), lambda i: (i, 0)),
            pl.BlockSpec((tt * SUBLANES, LANES), lambda i: (i + T // tt, 0)),
        ],
        out_specs=pl.BlockSpec((tt, D), lambda i: (i, 0)),
        out_shape=jax.ShapeDtypeStruct((T, D), F32),
        compiler_params=_cparams(("parallel",)),
        name="moe_combine",
    )(h, rt, g2, b2, y2, y2)


def _slot_plan(expert, n_experts, bm, n_plan_blocks):
    T = expert.shape[0]
    e_flat = expert.reshape(-1)
    onehot = (e_flat[:, None] == jnp.arange(n_experts, dtype=I32)[None, :]).astype(I32)
    csum = jnp.cumsum(onehot, axis=0)
    counts = csum[-1]
    eb_count = (counts + bm - 1) // bm
    eb_start = jnp.cumsum(eb_count) - eb_count
    slot = jnp.sum(onehot * (csum - 1 + (eb_start * bm)[None, :]), axis=1)
    n_slots = n_plan_blocks * bm
    tok = jnp.arange(2 * T, dtype=I32) // 2
    k = jnp.arange(2 * T, dtype=I32) % 2
    p = jnp.arange(n_slots, dtype=I32)
    spare = 2 * T + ((p // bm) % 2) * bm + p % bm
    dst = spare.at[slot].set(k * T + tok, unique_indices=True)
    src = jnp.where(dst < 2 * T, dst % T, 0)
    plan = jnp.stack([src.reshape(n_plan_blocks, bm), dst.reshape(n_plan_blocks, bm)], axis=1)
    return eb_start.astype(I32), eb_count.astype(I32), plan * SUBLANES


def _layer(x2, p, *, batch, seq, depth, n_heads, head_dim, moba_block, moba_topk,
           n_groups, e_per_group, tiles):
    T, D = x2.shape
    A = n_heads * head_dim
    C = p["w_dw"].shape[1]
    alpha = (2.0 * depth) ** 0.25
    tn = tiles["proj_tn"]
    assert A == C == D, "column-block addressing below assumes equal branch widths"

    w_in, b_in = p["w_in"], p["b_in"]

    half = head_dim // 2
    inv_freq = jnp.power(ROPE_THETA, -jnp.arange(half, dtype=F32) * (2.0 / head_dim))
    ang = jnp.arange(seq, dtype=F32)[:, None] * inv_freq[None, :]
    cos2 = jnp.concatenate([jnp.cos(ang), jnp.cos(ang)], axis=1)
    sin_s = jnp.concatenate([-jnp.sin(ang), jnp.sin(ang)], axis=1)

    u = _in_projection(x2, w_in.astype(BF16), b_in[None, :], cos2, sin_s, seq=seq, attn_w=A, conv_c=C,
                       head_dim=head_dim, tm=tiles["proj_tm"], tn=tn)
    o = _moba_attention(u, batch=batch, seq=seq, n_heads=n_heads, head_dim=head_dim,
                        blk=moba_block, topk=moba_topk, hps=ATTN_HPS)
    hc = _conv_branch(u, p["w_dw"], p["b_dw"], p["conv_ln_g"], p["conv_ln_b"], batch=batch,
                      seq=seq, conv_c=C, col_block=3, ts=tiles["conv_ts"], rt=tiles["conv_rt"],
                      halo=CONV_HALO)

    n_experts = n_groups * e_per_group
    n_route = n_groups + n_experts
    w_route = jnp.concatenate(
        [p["w_rg"], p["w_re"].transpose(1, 0, 2).reshape(D, n_experts),
         jnp.zeros((D, LANES - n_route), F32)], axis=1)
    b_route = jnp.concatenate(
        [p["b_rg"], p["b_re"].reshape(n_experts), jnp.zeros((LANES - n_route,), F32)])[None, :]
    wr_hi = w_route.astype(BF16)
    wr_lo = (w_route - wr_hi.astype(F32)).astype(BF16)

    h, hp, rt = _post_block(
        o, hc, u, x2, p["w_o_attn"].astype(BF16), p["w_pw2"].astype(BF16), p["b_pw2"][None, :],
        p["w_out"].astype(BF16), p["ln1_g"][None, :], p["ln1_b"][None, :], wr_hi, wr_lo, b_route,
        gate_block=4, alpha=alpha, n_groups=n_groups, e_per_group=e_per_group,
        tm=tiles["post_tm"])

    bm = tiles["moe_bm"]
    n_plan_blocks = -(-2 * T // bm) + n_experts + 3
    expert = rt[:, :2].astype(I32)
    eb_start, eb_count, plan = _slot_plan(expert, n_experts, bm, n_plan_blocks)
    y2 = _moe_experts(eb_start, eb_count, plan, hp, p["w1"], p["w3"], p["w2"],
                      n_tokens=T, bm=bm, nch=tiles["moe_nch"])
    return _combine(h, rt, y2, p["ln2_g"][None, :], p["ln2_b"][None, :], alpha=alpha,
                    tt=tiles["comb_tt"])


_PARAM_NAMES = ("w_in", "b_in", "w_o_attn", "w_dw", "b_dw", "conv_ln_g", "conv_ln_b", "w_pw2",
                "b_pw2", "w_out", "ln1_g", "ln1_b", "w_rg", "b_rg", "w_re", "b_re", "w1", "w3",
                "w2", "ln2_g", "ln2_b")

_TILES = dict(proj_tm=PROJ_TM, proj_tn=PROJ_TN, conv_ts=CONV_TS, conv_rt=CONV_RT,
              post_tm=POST_TM, comb_tt=COMB_TT, moe_bm=MOE_BM, moe_nch=MOE_NCH)


def _forward(x, params, *, n_heads=N_HEADS, head_dim=HEAD_DIM, moba_block=MOBA_BLOCK,
             moba_topk=MOBA_TOPK, n_groups=N_GROUPS, e_per_group=EXPERTS_PER_GROUP, tiles=None):
    tiles = dict(_TILES, **(tiles or {}))
    B, S, D = x.shape
    depth = params["w_in"].shape[0]
    x2 = x.reshape(B * S, D)
    for l in range(depth):
        p = {k: v[l] for k, v in params.items()}
        x2 = _layer(x2, p, batch=B, seq=S, depth=depth, n_heads=n_heads, head_dim=head_dim,
                    moba_block=moba_block, moba_topk=moba_topk, n_groups=n_groups,
                    e_per_group=e_per_group, tiles=tiles)
    return x2.reshape(B, S, D)


def kernel(x, w_in, b_in, w_o_attn, w_dw, b_dw, conv_ln_g, conv_ln_b, w_pw2, b_pw2, w_out, ln1_g,
           ln1_b, w_rg, b_rg, w_re, b_re, w1, w3, w2, ln2_g, ln2_b):
    params = dict(zip(_PARAM_NAMES, (w_in, b_in, w_o_attn, w_dw, b_dw, conv_ln_g, conv_ln_b, w_pw2,
                                     b_pw2, w_out, ln1_g, ln1_b, w_rg, b_rg, w_re, b_re, w1, w3,
                                     w2, ln2_g, ln2_b)))
    return _forward(x, params)
```

```python
import functools

import jax
import jax.numpy as jnp
from jax import lax
from jax.experimental import pallas as pl
from jax.experimental.pallas import tpu as pltpu

F32 = jnp.float32
BF16 = jnp.bfloat16
U32 = jnp.uint32
I32 = jnp.int32

N_HEADS = 16
HEAD_DIM = 128
ROPE_THETA = 10000.0
MOBA_BLOCK = 256
MOBA_TOPK = 3
CONV_WIDTH = 31
N_GROUPS = 4
EXPERTS_PER_GROUP = 8
LN_EPS = 1e-5
LOG2_E = 1.4426950408889634

LANES = 128
SUBLANES = 8
VMEM_LIMIT = 56 * 1024 * 1024

PROJ_TM = 1024
PROJ_TN = 1024
ATTN_HPS = 2
ATTN_ONES_ROWS = 16
CONV_TS = 256
CONV_RT = 16
CONV_HALO = 32
POST_TM = 256
COMB_TT = 512
MOE_BM = 256
MOE_NCH = 4


def _cparams(sem):
    return pltpu.CompilerParams(dimension_semantics=sem, vmem_limit_bytes=VMEM_LIMIT)


def _sigmoid(x):
    return 1.0 / (1.0 + jnp.exp(-x))


def _pack_bf16_pair(lo_f32, hi_f32):
    lo = lax.bitcast_convert_type(lo_f32.astype(BF16).astype(F32), U32) >> 16
    hi = lax.bitcast_convert_type(hi_f32.astype(BF16).astype(F32), U32)
    return hi | lo


def _unpack_lo(u):
    return lax.bitcast_convert_type(u << 16, F32)


def _unpack_hi(u):
    return lax.bitcast_convert_type(u & jnp.uint32(0xFFFF0000), F32)


def _store_tile_rows(ref, lead, x):
    n = x.shape[0]
    for s in range(SUBLANES):
        ref[(*lead, pl.ds(s, n, stride=SUBLANES), slice(None))] = x[:, s * LANES:(s + 1) * LANES]


def _load_tile_rows(ref, lead, n):
    return jnp.concatenate(
        [ref[(*lead, pl.ds(s, n, stride=SUBLANES), slice(None))] for s in range(SUBLANES)], axis=1)


def _inproj_body(x_ref, w_ref, wg_ref, b_ref, bg_ref, cos_ref, sin_ref, o_ref, xb_ref,
                 *, n_q, n_qk, n_qkv, n_glu, scale, head_dim):
    j = pl.program_id(1)
    tn = o_ref.shape[1]

    @pl.when(j == 0)
    def _cast():
        xb_ref[...] = x_ref[...].astype(BF16)

    def project(w, b):
        return jnp.dot(xb_ref[...], w[...], preferred_element_type=F32) + b[...]

    @pl.when(j < n_qk)
    def _rope():
        acc = project(w_ref, b_ref)
        s = jnp.where(j < n_q, scale, 1.0).astype(F32)
        cos = cos_ref[...] * s
        sin = sin_ref[...] * s
        for h in range(tn // head_dim):
            t = acc[:, h * head_dim:(h + 1) * head_dim]
            r = pltpu.roll(t, head_dim // 2, axis=1)
            o_ref[:, h * head_dim:(h + 1) * head_dim] = (t * cos + r * sin).astype(o_ref.dtype)

    @pl.when((j >= n_qk) & (j < n_qkv))
    def _plain():
        o_ref[...] = project(w_ref, b_ref).astype(o_ref.dtype)

    @pl.when((j >= n_qkv) & (j < n_qkv + n_glu))
    def _glu():
        a = project(w_ref, b_ref)
        g = project(wg_ref, bg_ref)
        o_ref[...] = (a * _sigmoid(g)).astype(o_ref.dtype)

    @pl.when(j >= n_qkv + n_glu)
    def _gate():
        o_ref[...] = _sigmoid(project(w_ref, b_ref)).astype(o_ref.dtype)


def _in_projection(x2, w_all, b_all, cos2, sin_s, *, seq, attn_w, conv_c, head_dim, tm, tn):
    T, D = x2.shape
    W = w_all.shape[1] - conv_c
    assert T % tm == 0 and seq % tm == 0 and W % tn == 0
    assert attn_w % tn == 0 and conv_c % tn == 0 and tn % head_dim == 0
    assert head_dim == LANES
    n_q = attn_w // tn
    n_glu = conv_c // tn
    n_qkv = 3 * n_q
    pos_tiles = seq // tm

    def main_idx(i, j):
        return (0, jnp.where(j < n_qkv + n_glu, j, j + n_glu))

    def g_idx(i, j):
        return (0, n_qkv + n_glu + jnp.clip(j - n_qkv, 0, n_glu - 1))

    body = functools.partial(_inproj_body, n_q=n_q, n_qk=2 * n_q, n_qkv=n_qkv, n_glu=n_glu,
                             scale=float(head_dim) ** -0.5 * LOG2_E, head_dim=head_dim)
    return pl.pallas_call(
        body,
        grid=(T // tm, W // tn),
        in_specs=[
            pl.BlockSpec((tm, D), lambda i, j: (i, 0)),
            pl.BlockSpec((D, tn), main_idx),
            pl.BlockSpec((D, tn), g_idx),
            pl.BlockSpec((1, tn), main_idx),
            pl.BlockSpec((1, tn), g_idx),
            pl.BlockSpec((tm, head_dim), lambda i, j: (i % pos_tiles, 0)),
            pl.BlockSpec((tm, head_dim), lambda i, j: (i % pos_tiles, 0)),
        ],
        out_specs=pl.BlockSpec((tm, tn), lambda i, j: (i, j)),
        out_shape=jax.ShapeDtypeStruct((T, W), BF16),
        scratch_shapes=[pltpu.VMEM((tm, D), BF16)],
        compiler_params=_cparams(("parallel", "arbitrary")),
        name="in_projection",
    )(x2, w_all, w_all, b_all, b_all, cos2, sin_s)


def _attn_prep(q, k, v, *, seq, blk):
    nb = seq // blk
    nbp = -(-nb // SUBLANES) * SUBLANES
    dh = q.shape[1]
    contract_last = (((1,), (1,)), ((), ()))
    kmean = jnp.mean(k.astype(F32).reshape(nb, blk, dh), axis=1)
    if nbp > nb:
        kmean = jnp.concatenate([kmean, jnp.zeros((nbp - nb, dh), F32)], axis=0)
    gate_t = lax.dot_general(kmean.astype(BF16), q, contract_last, preferred_element_type=F32)
    ones_rows = (lax.broadcasted_iota(I32, (ATTN_ONES_ROWS, seq), 0) == 0).astype(BF16)
    vt = jnp.concatenate([v.astype(F32).T.astype(BF16), ones_rows], axis=0)
    return gate_t, vt


def _attn_scores(q, k, gate_t, i, s_ref, *, blk, topk, nbp):
    contract_last = (((1,), (1,)), ((), ()))
    neg_inf = jnp.float32(-jnp.inf)
    sub = lax.broadcasted_iota(I32, (nbp, blk), 0)
    qi = q[i * blk:(i + 1) * blk]
    nk = (i + 1) * blk
    st = lax.dot_general(k[:nk], qi, contract_last, preferred_element_type=F32)
    if i > topk:
        gm = jnp.where(sub < i, gate_t[:, i * blk:(i + 1) * blk], neg_inf)
        rank = jnp.zeros((nbp, blk), I32)
        for other in range(i):
            g_other = gm[other:other + 1, :]
            beats = (g_other > gm) | ((g_other == gm) & (sub > other))
            rank = rank + beats.astype(I32)
        bias_t = jnp.where((sub < i) & (rank < topk), 0.0, neg_inf).astype(F32)
    m = None
    for n in range(i + 1):
        t = st[n * blk:(n + 1) * blk]
        if n == i:
            key_r = lax.broadcasted_iota(I32, (blk, blk), 0)
            qry_c = lax.broadcasted_iota(I32, (blk, blk), 1)
            t = jnp.where(key_r <= qry_c, t, neg_inf)
        elif i > topk:
            t = t + bias_t[n:n + 1, :]
        s_ref[n * blk:(n + 1) * blk, :] = t
        t_max = jnp.max(t, axis=0, keepdims=True)
        m = t_max if m is None else jnp.maximum(m, t_max)
    return m


def _attn_output(vt, m, i, s_ref, *, blk, dh):
    nk = (i + 1) * blk
    p = jnp.exp2(s_ref[:nk, :] - m).astype(BF16)
    ot = jnp.dot(vt[:, :nk], p, preferred_element_type=F32)
    return (ot[:dh] * (1.0 / ot[dh:dh + 1])).T


def _attn_body(q_ref, k_ref, v_ref, o_ref, s_scr, *, seq, blk, topk, dh):
    nb = seq // blk
    nbp = -(-nb // SUBLANES) * SUBLANES
    heads = []
    for hh in range(q_ref.shape[1] // dh):
        c0 = hh * dh
        q, k, v = q_ref[:, c0:c0 + dh], k_ref[:, c0:c0 + dh], v_ref[:, c0:c0 + dh]
        heads.append((c0, q, k) + _attn_prep(q, k, v, seq=seq, blk=blk))
    units = [(h, i) for h in range(len(heads)) for i in range(nb)]

    def scores(u):
        h, i = units[u]
        _, q, k, gate_t, _ = heads[h]
        return _attn_scores(q, k, gate_t, i, s_scr.at[u % 2], blk=blk, topk=topk, nbp=nbp)

    m_next = scores(0)
    for u, (h, i) in enumerate(units):
        m = m_next
        if u + 1 < len(units):
            m_next = scores(u + 1)
        c0, _, _, _, vt = heads[h]
        o = _attn_output(vt, m, i, s_scr.at[u % 2], blk=blk, dh=dh)
        o_ref[i * blk:(i + 1) * blk, c0:c0 + dh] = o.astype(o_ref.dtype)


def _moba_attention(u, *, batch, seq, n_heads, head_dim, blk, topk, hps):
    T = u.shape[0]
    assert seq % blk == 0 and blk % LANES == 0 and n_heads % hps == 0
    n_hb = n_heads // hps
    body = functools.partial(_attn_body, seq=seq, blk=blk, topk=topk, dh=head_dim)
    return pl.pallas_call(
        body,
        grid=(batch, n_hb),
        in_specs=[
            pl.BlockSpec((seq, hps * head_dim), lambda b, h: (b, h)),
            pl.BlockSpec((seq, hps * head_dim), lambda b, h: (b, n_hb + h)),
            pl.BlockSpec((seq, hps * head_dim), lambda b, h: (b, 2 * n_hb + h)),
        ],
        out_specs=pl.BlockSpec((seq, hps * head_dim), lambda b, h: (b, h)),
        out_shape=jax.ShapeDtypeStruct((T, n_heads * head_dim), BF16),
        scratch_shapes=[pltpu.VMEM((2, seq, blk), F32)],
        compiler_params=_cparams(("parallel", "parallel")),
        name="moba_attention",
    )(u, u, u)


def _conv_body(x_ref, w_ref, bdw_ref, g_ref, b_ref, o_ref, win_ref, conv_ref,
               *, width, ts, rt, halo, eps):
    ng = win_ref.shape[0]
    sub = SUBLANES

    @pl.when(pl.program_id(1) == 0)
    def _zero_halo():
        win_ref[:, 0:halo * sub, :] = jnp.zeros((ng, halo * sub, LANES), F32)

    for g in range(ng):
        for s in range(sub):
            c0 = (g * sub + s) * LANES
            win_ref[g, pl.ds(halo * sub + s, ts, stride=sub), :] = x_ref[:, c0:c0 + LANES].astype(F32)

    first = halo - (width - 1)
    for g in range(ng):
        def chunk(r, carry, g=g):
            t0 = r * rt
            acc = jnp.broadcast_to(bdw_ref[g][None], (rt, sub, LANES))
            for j in range(width):
                start = pl.multiple_of((t0 + first + j) * sub, sub)
                slab = win_ref[g, pl.ds(start, rt * sub), :].reshape(rt, sub, LANES)
                acc = acc + slab * w_ref[g, j][None]
            conv_ref[g, pl.ds(pl.multiple_of(t0 * sub, sub), rt * sub), :] = acc.reshape(rt * sub, LANES)
            return carry

        lax.fori_loop(0, ts // rt, chunk, 0)

    win_ref[:, 0:halo * sub, :] = win_ref[:, ts * sub:(ts + halo) * sub, :]

    def channel_chunks():
        for g in range(ng):
            for s in range(sub):
                yield (g * sub + s) * LANES, conv_ref[g, pl.ds(s, ts, stride=sub), :]

    n = ng * sub * LANES
    total = jnp.zeros((ts, 1), F32)
    for _, y in channel_chunks():
        total = total + jnp.sum(y, axis=1, keepdims=True)
    mu = total * (1.0 / n)
    sq = jnp.zeros((ts, 1), F32)
    for _, y in channel_chunks():
        d = y - mu
        sq = sq + jnp.sum(d * d, axis=1, keepdims=True)
    inv = lax.rsqrt(sq * (1.0 / n) + eps)
    for c0, y in channel_chunks():
        z = (y - mu) * inv * g_ref[:, c0:c0 + LANES] + b_ref[:, c0:c0 + LANES]
        o_ref[:, c0:c0 + LANES] = (z * _sigmoid(z)).astype(o_ref.dtype)


def _conv_branch(u, w_dw, b_dw, ln_g, ln_b, *, batch, seq, conv_c, col_block, ts, rt, halo):
    T = u.shape[0]
    width = w_dw.shape[0]
    gw = SUBLANES * LANES
    assert seq % ts == 0 and conv_c % gw == 0 and ts % rt == 0
    assert halo >= width - 1 and ts >= halo
    ng = conv_c // gw
    w_t = w_dw.reshape(width, ng, SUBLANES, LANES).transpose(1, 0, 2, 3)
    n_s = seq // ts
    body = functools.partial(_conv_body, width=width, ts=ts, rt=rt, halo=halo, eps=LN_EPS)
    return pl.pallas_call(
        body,
        grid=(batch, n_s),
        in_specs=[
            pl.BlockSpec((ts, conv_c), lambda b, s: (b * n_s + s, col_block)),
            pl.BlockSpec((ng, width, SUBLANES, LANES), lambda b, s: (0, 0, 0, 0)),
            pl.BlockSpec((ng, SUBLANES, LANES), lambda b, s: (0, 0, 0)),
            pl.BlockSpec((1, conv_c), lambda b, s: (0, 0)),
            pl.BlockSpec((1, conv_c), lambda b, s: (0, 0)),
        ],
        out_specs=pl.BlockSpec((ts, conv_c), lambda b, s: (b * n_s + s, 0)),
        out_shape=jax.ShapeDtypeStruct((T, conv_c), BF16),
        scratch_shapes=[pltpu.VMEM((ng, (halo + ts) * SUBLANES, LANES), F32),
                        pltpu.VMEM((ng, ts * SUBLANES, LANES), F32)],
        compiler_params=_cparams(("parallel", "arbitrary")),
        name="conv_branch",
    )(u, w_t, b_dw.reshape(ng, SUBLANES, LANES), ln_g[None, :], ln_b[None, :])


def _post_body(o_ref, hc_ref, ga_ref, gb_ref, x_ref, wo_ref, wp_ref, bp_ref, wout_ref,
               g1_ref, b1_ref, wrh_ref, wrl_ref, br_ref, h_ref, hp_ref, rt_ref, m_scr, z_scr,
               *, alpha, eps, n_groups, e_per_group):
    i = pl.program_id(0)

    @pl.when(i == 0)
    def _no_previous_tiles():
        m_scr[1] = jnp.zeros(m_scr.shape[1:], BF16)
        z_scr[1] = jnp.zeros(z_scr.shape[1:], F32)
        z_scr[2] = jnp.zeros(z_scr.shape[1:], F32)

    ya = jnp.dot(o_ref[...], wo_ref[...], preferred_element_type=F32)

    z = z_scr[(i + 1) % 3]
    mu = jnp.mean(z, axis=1, keepdims=True)
    zc = z - mu
    var = jnp.mean(zc * zc, axis=1, keepdims=True)
    h = zc * lax.rsqrt(var + eps) * g1_ref[...] + b1_ref[...]
    h_ref[...] = h
    half = h.shape[1] // 2
    _store_tile_rows(hp_ref, (), _pack_bf16_pair(h[:, :half], h[:, half:]))

    yc = jnp.dot(hc_ref[...], wp_ref[...], preferred_element_type=F32) + bp_ref[...]

    h_hi = h.astype(BF16)
    h_lo = (h - h_hi.astype(F32)).astype(BF16)
    hi_terms = jnp.dot(h_hi, wrl_ref[...], preferred_element_type=F32)
    logits = (hi_terms[:, :LANES] + hi_terms[:, LANES:]
              + jnp.dot(h_lo, wrh_ref[...], preferred_element_type=F32) + br_ref[...])
    tm = logits.shape[0]
    lane = lax.broadcasted_iota(I32, (tm, LANES), 1)
    neg_inf = jnp.float32(-jnp.inf)
    big = jnp.int32(LANES)

    def first_argmax(vals):
        top = jnp.max(vals, axis=1, keepdims=True)
        idx = jnp.min(jnp.where(vals == top, lane, big), axis=1, keepdims=True)
        return top, idx

    gl = jnp.where(lane < n_groups, logits, neg_inf)
    gmax, grp = first_argmax(gl)
    grp_w = 1.0 / jnp.sum(jnp.exp(gl - gmax), axis=1, keepdims=True)
    lo_lane = n_groups + grp * e_per_group
    el = jnp.where((lane >= lo_lane) & (lane < lo_lane + e_per_group), logits, neg_inf)
    v1, i1 = first_argmax(el)
    v2, i2 = first_argmax(jnp.where(lane == i1, neg_inf, el))
    t = jnp.exp(v2 - v1)
    p1 = 1.0 / (1.0 + t)
    c1 = p1 * grp_w
    c2 = (t * p1) * grp_w
    e1 = (i1 - n_groups).astype(F32)
    e2 = (i2 - n_groups).astype(F32)
    rt_ref[...] = jnp.where(lane == 0, e1, jnp.where(lane == 1, e2,
                            jnp.where(lane == 2, c1, jnp.where(lane == 3, c2, 0.0))))

    prev = (i + 1) % 2
    z_scr[(i + 2) % 3] += jnp.dot(m_scr[prev], wout_ref[...], preferred_element_type=F32)
    m_scr[i % 2] = (ga_ref[...].astype(F32) * ya + gb_ref[...].astype(F32) * yc).astype(BF16)
    z_scr[i % 3] = alpha * x_ref[...]


def _post_block(o, hc, u, x2, wo, wp, bp, wout, g1, b1, wr_hi, wr_lo, br, *, gate_block,
                alpha, n_groups, e_per_group, tm):
    T, D = x2.shape
    A = o.shape[1]
    C = hc.shape[1]
    assert T % tm == 0 and D == 2 * SUBLANES * LANES, "packed rows are stored as one (8, 128) tile"
    assert n_groups * (1 + e_per_group) <= LANES
    const = lambda i: (0, 0)
    resident = lambda shape: pl.BlockSpec(shape, const, pipeline_mode=pl.Buffered(1))
    body = functools.partial(_post_body, alpha=alpha, eps=LN_EPS, n_groups=n_groups,
                             e_per_group=e_per_group)
    n = T // tm
    cur = lambda i: jnp.minimum(i, n - 1)
    prev = lambda i: jnp.maximum(i - 2, 0)
    return pl.pallas_call(
        body,
        grid=(n + 2,),
        in_specs=[
            pl.BlockSpec((tm, A), lambda i: (cur(i), 0)),
            pl.BlockSpec((tm, C), lambda i: (cur(i), 0)),
            pl.BlockSpec((tm, D), lambda i: (cur(i), gate_block)),
            pl.BlockSpec((tm, D), lambda i: (cur(i), gate_block + 1)),
            pl.BlockSpec((tm, D), lambda i: (cur(i), 0)),
            resident((A, D)),
            resident((C, D)),
            resident((1, D)),
            resident((D, D)),
            resident((1, D)),
            resident((1, D)),
            resident((D, LANES)),
            resident((D, 2 * LANES)),
            resident((1, LANES)),
        ],
        out_specs=[
            pl.BlockSpec((tm, D), lambda i: (prev(i), 0)),
            pl.BlockSpec((tm * SUBLANES, LANES), lambda i: (prev(i), 0)),
            pl.BlockSpec((tm, LANES), lambda i: (prev(i), 0)),
        ],
        out_shape=[
            jax.ShapeDtypeStruct((T, D), F32),
            jax.ShapeDtypeStruct((T * SUBLANES, LANES), U32),
            jax.ShapeDtypeStruct((T, LANES), F32),
        ],
        scratch_shapes=[pltpu.VMEM((2, tm, D), BF16), pltpu.VMEM((3, tm, D), F32)],
        compiler_params=_cparams(("arbitrary",)),
        name="merge_project_route",
    )(o, hc, u, u, x2, wo, wp, bp, wout, g1, b1, wr_hi, wr_lo, br)


def _row_copy(src_ref, src_row8, dst_ref, dst_row8, sem):
    aligned = lambda v: v if isinstance(v, int) else pl.multiple_of(v, SUBLANES)
    return pltpu.make_async_copy(src_ref.at[pl.ds(aligned(src_row8), SUBLANES)],
                                 dst_ref.at[pl.ds(aligned(dst_row8), SUBLANES)], sem)


def _moe_body(es_ref, ec_ref, plan_ref, hp_ref, w1_ref, w3_ref, w2_ref, y2_ref,
              w1b, w3b, w2b, xbuf, ybuf, idx, gsem, ssem, isem, *, n_experts, nch, bm, spare_row0):
    g = pl.program_id(0)
    c = pl.program_id(1)
    fc = w1_ref.shape[1]
    half = SUBLANES * LANES
    n_total = es_ref[n_experts - 1] + ec_ref[n_experts - 1]

    def idx_copy(q):
        return pltpu.make_async_copy(plan_ref.at[q], idx.at[q % 3], isem.at[q % 3])

    def gather_start(q):
        for r in range(bm):
            _row_copy(hp_ref, idx[q % 3, 0, r], xbuf.at[q % 3], r * SUBLANES, gsem.at[q % 3]).start()

    def gather_wait(q):
        for r in range(bm):
            _row_copy(hp_ref, 0, xbuf.at[q % 3], r * SUBLANES, gsem.at[q % 3]).wait()

    def scatter_start(q):
        for r in range(bm):
            _row_copy(ybuf.at[q % 2], r * SUBLANES, y2_ref, idx[q % 3, 1, r], ssem.at[q % 2]).start()

    def scatter_wait(q):
        for r in range(bm):
            _row_copy(ybuf.at[q % 2], r * SUBLANES, y2_ref, 0, ssem.at[q % 2]).wait()

    @pl.when((g == 0) & (c == 0))
    def _prime():
        ybuf[...] = jnp.zeros(ybuf.shape, U32)
        spare = [pltpu.make_async_copy(
            ybuf.at[s], y2_ref.at[pl.ds((spare_row0 + s * bm) * SUBLANES, bm * SUBLANES)], ssem.at[s])
            for s in range(2)]
        for cp in spare:
            cp.start()
        for cp in spare:
            cp.wait()
        for q in range(3):
            idx_copy(q).start()
        for q in range(2):
            idx_copy(q).wait()
            gather_start(q)

    @pl.when(g < n_experts)
    def _cast_next_expert_chunk():
        slot = g % 2
        w1b[slot, c] = w1_ref[...].astype(BF16)
        w3b[slot, c] = w3_ref[...].astype(BF16)
        w2b[slot, pl.ds(pl.multiple_of(c * fc, fc), fc), :] = w2_ref[...].astype(BF16)

    @pl.when(g >= 1)
    def _compute_previous_expert_share():
        e = g - 1
        slot = e % 2
        nb = ec_ref[e]
        base = es_ref[e]

        def block(r, carry):
            q = base + r
            gather_wait(q)

            @pl.when(q >= 2)
            def _():
                scatter_wait(q - 2)

            u = _load_tile_rows(xbuf, (q % 3,), bm)
            x_lo = _unpack_lo(u).astype(BF16)
            x_hi = _unpack_hi(u).astype(BF16)
            hs = []
            for cc in range(nch):
                a = (jnp.dot(x_lo, w1b[slot, cc, :half, :], preferred_element_type=F32)
                     + jnp.dot(x_hi, w1b[slot, cc, half:, :], preferred_element_type=F32))
                b = (jnp.dot(x_lo, w3b[slot, cc, :half, :], preferred_element_type=F32)
                     + jnp.dot(x_hi, w3b[slot, cc, half:, :], preferred_element_type=F32))
                hs.append((a * _sigmoid(a) * b).astype(BF16))
            y = jnp.dot(jnp.concatenate(hs, axis=1), w2b[slot], preferred_element_type=F32)
            _store_tile_rows(ybuf, (q % 2,), _pack_bf16_pair(y[:, :half], y[:, half:]))
            scatter_start(q)
            idx_copy(q + 2).wait()
            gather_start(q + 2)
            idx_copy(q + 3).start()
            return carry

        lax.fori_loop((nb * c) // nch, (nb * (c + 1)) // nch, block, 0)

    @pl.when((g == n_experts) & (c == nch - 1))
    def _drain():
        idx_copy(n_total + 2).wait()
        gather_wait(n_total)
        gather_wait(n_total + 1)

        @pl.when(n_total >= 1)
        def _():
            scatter_wait(n_total - 1)

        @pl.when(n_total >= 2)
        def _():
            scatter_wait(n_total - 2)


def _moe_experts(eb_start, eb_count, plan, hp, w1, w3, w2, *, n_tokens, bm, nch):
    n_out_rows = 2 * n_tokens + 2 * bm
    E, D, F = w1.shape
    assert D == 2 * SUBLANES * LANES and hp.shape[1] == LANES
    assert F % nch == 0 and plan.shape[1:] == (2, bm)
    fc = F // nch

    def w_in_idx(g, c, es, ec):
        return (jnp.minimum(g, E - 1), 0, jnp.where(g < E, c, nch - 1))

    def w_out_idx(g, c, es, ec):
        return (jnp.minimum(g, E - 1), jnp.where(g < E, c, nch - 1), 0)

    grid_spec = pltpu.PrefetchScalarGridSpec(
        num_scalar_prefetch=2,
        grid=(E + 1, nch),
        in_specs=[
            pl.BlockSpec(memory_space=pl.ANY),
            pl.BlockSpec(memory_space=pl.ANY),
            pl.BlockSpec((None, D, fc), w_in_idx),
            pl.BlockSpec((None, D, fc), w_in_idx),
            pl.BlockSpec((None, fc, D), w_out_idx),
        ],
        out_specs=pl.BlockSpec(memory_space=pl.ANY),
        scratch_shapes=[
            pltpu.VMEM((2, nch, D, fc), BF16),
            pltpu.VMEM((2, nch, D, fc), BF16),
            pltpu.VMEM((2, F, D), BF16),
            pltpu.VMEM((3, bm * SUBLANES, LANES), U32),
            pltpu.VMEM((2, bm * SUBLANES, LANES), U32),
            pltpu.SMEM((3, 2, bm), I32),
            pltpu.SemaphoreType.DMA((3,)),
            pltpu.SemaphoreType.DMA((2,)),
            pltpu.SemaphoreType.DMA((3,)),
        ],
    )
    body = functools.partial(_moe_body, n_experts=E, nch=nch, bm=bm, spare_row0=2 * n_tokens)
    return pl.pallas_call(
        body,
        grid_spec=grid_spec,
        out_shape=jax.ShapeDtypeStruct((n_out_rows * SUBLANES, LANES), U32),
        compiler_params=_cparams(("arbitrary", "arbitrary")),
        name="moe_experts",
    )(eb_start, eb_count, plan, hp, w1, w3, w2)


def _combine_body(h_ref, rt_ref, g_ref, b_ref, y0_ref, y1_ref, o_ref, *, alpha, eps):
    tt = h_ref.shape[0]
    u0 = _load_tile_rows(y0_ref, (), tt)
    u1 = _load_tile_rows(y1_ref, (), tt)
    half = u0.shape[1]
    c0 = rt_ref[:, 2:3]
    c1 = rt_ref[:, 3:4]
    z_lo = alpha * h_ref[:, :half] + (_unpack_lo(u0) * c0 + _unpack_lo(u1) * c1)
    z_hi = alpha * h_ref[:, half:] + (_unpack_hi(u0) * c0 + _unpack_hi(u1) * c1)
    n = 2 * half
    mu = (jnp.sum(z_lo, axis=1, keepdims=True) + jnp.sum(z_hi, axis=1, keepdims=True)) * (1.0 / n)
    d_lo = z_lo - mu
    d_hi = z_hi - mu
    var = (jnp.sum(d_lo * d_lo, axis=1, keepdims=True)
           + jnp.sum(d_hi * d_hi, axis=1, keepdims=True)) * (1.0 / n)
    inv = lax.rsqrt(var + eps)
    o_ref[:, :half] = d_lo * inv * g_ref[:, :half] + b_ref[:, :half]
    o_ref[:, half:] = d_hi * inv * g_ref[:, half:] + b_ref[:, half:]


def _combine(h, rt, y2, g2, b2, *, alpha, tt):
    T, D = h.shape
    assert T % tt == 0 and D == 2 * SUBLANES * LANES and y2.shape[1] == LANES
    body = functools.partial(_combine_body, alpha=alpha, eps=LN_EPS)
    return pl.pallas_call(
        body,
        grid=(T // tt,),
        in_specs=[
            pl.BlockSpec((tt, D), lambda i: (i, 0)),
            pl.BlockSpec((tt, LANES), lambda i: (i, 0)),
            pl.BlockSpec((1, D), lambda i: (0, 0)),
            pl.BlockSpec((1, D), lambda i: (0, 0)),
            pl.BlockSpec((tt * SUBLANES, LANES), lambda i: (i, 0)),
            pl.BlockSpec((tt * SUBLANES, LANES), lambda i: (i + T // tt, 0)),
        ],
        out_specs=pl.BlockSpec((tt, D), lambda i: (i, 0)),
        out_shape=jax.ShapeDtypeStruct((T, D), F32),
        compiler_params=_cparams(("parallel",)),
        name="moe_combine",
    )(h, rt, g2, b2, y2, y2)


def _slot_plan(expert, n_experts, bm, n_plan_blocks):
    T = expert.shape[0]
    e_flat = expert.reshape(-1)
    onehot = (e_flat[:, None] == jnp.arange(n_experts, dtype=I32)[None, :]).astype(I32)
    csum = jnp.cumsum(onehot, axis=0)
    counts = csum[-1]
    eb_count = (counts + bm - 1) // bm
    eb_start = jnp.cumsum(eb_count) - eb_count
    slot = jnp.sum(onehot * (csum - 1 + (eb_start * bm)[None, :]), axis=1)
    n_slots = n_plan_blocks * bm
    tok = jnp.arange(2 * T, dtype=I32) // 2
    k = jnp.arange(2 * T, dtype=I32) % 2
    p = jnp.arange(n_slots, dtype=I32)
    spare = 2 * T + ((p // bm) % 2) * bm + p % bm
    dst = spare.at[slot].set(k * T + tok, unique_indices=True)
    src = jnp.where(dst < 2 * T, dst % T, 0)
    plan = jnp.stack([src.reshape(n_plan_blocks, bm), dst.reshape(n_plan_blocks, bm)], axis=1)
    return eb_start.astype(I32), eb_count.astype(I32), plan * SUBLANES


def _layer(x2, p, *, batch, seq, depth, n_heads, head_dim, moba_block, moba_topk,
           n_groups, e_per_group, tiles):
    T, D = x2.shape
    A = n_heads * head_dim
    C = p["w_dw"].shape[1]
    alpha = (2.0 * depth) ** 0.25
    tn = tiles["proj_tn"]
    assert A == C == D, "column-block addressing below assumes equal branch widths"

    w_in, b_in = p["w_in"], p["b_in"]

    half = head_dim // 2
    inv_freq = jnp.power(ROPE_THETA, -jnp.arange(half, dtype=F32) * (2.0 / head_dim))
    ang = jnp.arange(seq, dtype=F32)[:, None] * inv_freq[None, :]
    cos2 = jnp.concatenate([jnp.cos(ang), jnp.cos(ang)], axis=1)
    sin_s = jnp.concatenate([-jnp.sin(ang), jnp.sin(ang)], axis=1)

    u = _in_projection(x2, w_in.astype(BF16), b_in[None, :], cos2, sin_s, seq=seq, attn_w=A, conv_c=C,
                       head_dim=head_dim, tm=tiles["proj_tm"], tn=tn)
    o = _moba_attention(u, batch=batch, seq=seq, n_heads=n_heads, head_dim=head_dim,
                        blk=moba_block, topk=moba_topk, hps=ATTN_HPS)
    hc = _conv_branch(u, p["w_dw"], p["b_dw"], p["conv_ln_g"], p["conv_ln_b"], batch=batch,
                      seq=seq, conv_c=C, col_block=3, ts=tiles["conv_ts"], rt=tiles["conv_rt"],
                      halo=CONV_HALO)

    n_experts = n_groups * e_per_group
    n_route = n_groups + n_experts
    w_route = jnp.concatenate(
        [p["w_rg"], p["w_re"].transpose(1, 0, 2).reshape(D, n_experts),
         jnp.zeros((D, LANES - n_route), F32)], axis=1)
    b_route = jnp.concatenate(
        [p["b_rg"], p["b_re"].reshape(n_experts), jnp.zeros((LANES - n_route,), F32)])[None, :]
    wr_hi = w_route.astype(BF16)
    wr_lo = jnp.concatenate([wr_hi, (w_route - wr_hi.astype(F32)).astype(BF16)], axis=1)

    h, hp, rt = _post_block(
        o, hc, u, x2, p["w_o_attn"].astype(BF16), p["w_pw2"].astype(BF16), p["b_pw2"][None, :],
        p["w_out"].astype(BF16), p["ln1_g"][None, :], p["ln1_b"][None, :], wr_hi, wr_lo, b_route,
        gate_block=4, alpha=alpha, n_groups=n_groups, e_per_group=e_per_group,
        tm=tiles["post_tm"])

    bm = tiles["moe_bm"]
    n_plan_blocks = -(-2 * T // bm) + n_experts + 3
    expert = rt[:, :2].astype(I32)
    eb_start, eb_count, plan = _slot_plan(expert, n_experts, bm, n_plan_blocks)
    y2 = _moe_experts(eb_start, eb_count, plan, hp, p["w1"], p["w3"], p["w2"],
                      n_tokens=T, bm=bm, nch=tiles["moe_nch"])
    return _combine(h, rt, y2, p["ln2_g"][None, :], p["ln2_b"][None, :], alpha=alpha,
                    tt=tiles["comb_tt"])


_PARAM_NAMES = ("w_in", "b_in", "w_o_attn", "w_dw", "b_dw", "conv_ln_g", "conv_ln_b", "w_pw2",
                "b_pw2", "w_out", "ln1_g", "ln1_b", "w_rg", "b_rg", "w_re", "b_re", "w1", "w3",
                "w2", "ln2_g", "ln2_b")

_TILES = dict(proj_tm=PROJ_TM, proj_tn=PROJ_TN, conv_ts=CONV_TS, conv_rt=CONV_RT,
              post_tm=POST_TM, comb_tt=COMB_TT, moe_bm=MOE_BM, moe_nch=MOE_NCH)


def _forward(x, params, *, n_heads=N_HEADS, head_dim=HEAD_DIM, moba_block=MOBA_BLOCK,
             moba_topk=MOBA_TOPK, n_groups=N_GROUPS, e_per_group=EXPERTS_PER_GROUP, tiles=None):
    tiles = dict(_TILES, **(tiles or {}))
    B, S, D = x.shape
    depth = params["w_in"].shape[0]
    x2 = x.reshape(B * S, D)
    for l in range(depth):
        p = {k: v[l] for k, v in params.items()}
        x2 = _layer(x2, p, batch=B, seq=S, depth=depth, n_heads=n_heads, head_dim=head_dim,
                    moba_block=moba_block, moba_topk=moba_topk, n_groups=n_groups,
                    e_per_group=e_per_group, tiles=tiles)
    return x2.reshape(B, S, D)


def kernel(x, w_in, b_in, w_o_attn, w_dw, b_dw, conv_ln_g, conv_ln_b, w_pw2, b_pw2, w_out, ln1_g,
           ln1_b, w_rg, b_rg, w_re, b_re, w1, w3, w2, ln2_g, ln2_b):
    params = dict(zip(_PARAM_NAMES, (w_in, b_in, w_o_attn, w_dw, b_dw, conv_ln_g, conv_ln_b, w_pw2,
                                     b_pw2, w_out, ln1_g, ln1_b, w_rg, b_rg, w_re, b_re, w1, w3,
                                     w2, ln2_g, ln2_b)))
    return _forward(x, params)
```

```python
import functools

import jax
import jax.numpy as jnp
from jax import lax
from jax.experimental import pallas as pl
from jax.experimental.pallas import tpu as pltpu

F32 = jnp.float32
BF16 = jnp.bfloat16
U32 = jnp.uint32
I32 = jnp.int32

N_HEADS = 16
HEAD_DIM = 128
ROPE_THETA = 10000.0
MOBA_BLOCK = 256
MOBA_TOPK = 3
CONV_WIDTH = 31
N_GROUPS = 4
EXPERTS_PER_GROUP = 8
LN_EPS = 1e-5
LOG2_E = 1.4426950408889634

LANES = 128
SUBLANES = 8
VMEM_LIMIT = 56 * 1024 * 1024

PROJ_TM = 1024
PROJ_TN = 1024
ATTN_HPS = 2
ATTN_LOOKAHEAD = 2
ATTN_ONES_ROWS = 16
CONV_TS = 256
CONV_RT = 16
CONV_HALO = 32
POST_TM = 256
COMB_TT = 512
MOE_BM = 256
MOE_NCH = 4
MOE_OUT_CHUNK = 256


def _cparams(sem):
    return pltpu.CompilerParams(dimension_semantics=sem, vmem_limit_bytes=VMEM_LIMIT)


def _sigmoid(x):
    return 1.0 / (1.0 + jnp.exp(-x))


def _pack_bf16_pair(lo_f32, hi_f32):
    lo = lax.bitcast_convert_type(lo_f32.astype(BF16).astype(F32), U32) >> 16
    hi = lax.bitcast_convert_type(hi_f32.astype(BF16).astype(F32), U32)
    return hi | lo


def _unpack_lo(u):
    return lax.bitcast_convert_type(u << 16, F32)


def _unpack_hi(u):
    return lax.bitcast_convert_type(u & jnp.uint32(0xFFFF0000), F32)


def _store_tile_rows(ref, lead, x):
    n = x.shape[0]
    for s in range(SUBLANES):
        ref[(*lead, pl.ds(s, n, stride=SUBLANES), slice(None))] = x[:, s * LANES:(s + 1) * LANES]


def _load_tile_rows(ref, lead, n):
    return jnp.concatenate(
        [ref[(*lead, pl.ds(s, n, stride=SUBLANES), slice(None))] for s in range(SUBLANES)], axis=1)


def _inproj_body(x_ref, w_ref, wg_ref, b_ref, bg_ref, cos_ref, sin_ref, o_ref, xb_ref,
                 *, n_q, n_qk, n_qkv, n_glu, scale, head_dim):
    j = pl.program_id(1)
    tn = o_ref.shape[1]

    @pl.when(j == 0)
    def _cast():
        xb_ref[...] = x_ref[...].astype(BF16)

    def project(w, b):
        return jnp.dot(xb_ref[...], w[...], preferred_element_type=F32) + b[...]

    @pl.when(j < n_qk)
    def _rope():
        acc = project(w_ref, b_ref)
        s = jnp.where(j < n_q, scale, 1.0).astype(F32)
        cos = cos_ref[...] * s
        sin = sin_ref[...] * s
        for h in range(tn // head_dim):
            t = acc[:, h * head_dim:(h + 1) * head_dim]
            r = pltpu.roll(t, head_dim // 2, axis=1)
            o_ref[:, h * head_dim:(h + 1) * head_dim] = (t * cos + r * sin).astype(o_ref.dtype)

    @pl.when((j >= n_qk) & (j < n_qkv))
    def _plain():
        o_ref[...] = project(w_ref, b_ref).astype(o_ref.dtype)

    @pl.when((j >= n_qkv) & (j < n_qkv + n_glu))
    def _glu():
        a = project(w_ref, b_ref)
        g = project(wg_ref, bg_ref)
        o_ref[...] = (a * _sigmoid(g)).astype(o_ref.dtype)

    @pl.when(j >= n_qkv + n_glu)
    def _gate():
        o_ref[...] = _sigmoid(project(w_ref, b_ref)).astype(o_ref.dtype)


def _in_projection(x2, w_all, b_all, cos2, sin_s, *, seq, attn_w, conv_c, head_dim, tm, tn):
    T, D = x2.shape
    W = w_all.shape[1] - conv_c
    assert T % tm == 0 and seq % tm == 0 and W % tn == 0
    assert attn_w % tn == 0 and conv_c % tn == 0 and tn % head_dim == 0
    assert head_dim == LANES
    n_q = attn_w // tn
    n_glu = conv_c // tn
    n_qkv = 3 * n_q
    pos_tiles = seq // tm

    def main_idx(i, j):
        return (0, jnp.where(j < n_qkv + n_glu, j, j + n_glu))

    def g_idx(i, j):
        return (0, n_qkv + n_glu + jnp.clip(j - n_qkv, 0, n_glu - 1))

    body = functools.partial(_inproj_body, n_q=n_q, n_qk=2 * n_q, n_qkv=n_qkv, n_glu=n_glu,
                             scale=float(head_dim) ** -0.5 * LOG2_E, head_dim=head_dim)
    return pl.pallas_call(
        body,
        grid=(T // tm, W // tn),
        in_specs=[
            pl.BlockSpec((tm, D), lambda i, j: (i, 0)),
            pl.BlockSpec((D, tn), main_idx),
            pl.BlockSpec((D, tn), g_idx),
            pl.BlockSpec((1, tn), main_idx),
            pl.BlockSpec((1, tn), g_idx),
            pl.BlockSpec((tm, head_dim), lambda i, j: (i % pos_tiles, 0)),
            pl.BlockSpec((tm, head_dim), lambda i, j: (i % pos_tiles, 0)),
        ],
        out_specs=pl.BlockSpec((tm, tn), lambda i, j: (i, j)),
        out_shape=jax.ShapeDtypeStruct((T, W), BF16),
        scratch_shapes=[pltpu.VMEM((tm, D), BF16)],
        compiler_params=_cparams(("parallel", "arbitrary")),
        name="in_projection",
    )(x2, w_all, w_all, b_all, b_all, cos2, sin_s)


def _attn_prep(q, k, v, *, seq, blk):
    nb = seq // blk
    nbp = -(-nb // SUBLANES) * SUBLANES
    dh = q.shape[1]
    contract_last = (((1,), (1,)), ((), ()))
    kmean = jnp.mean(k.astype(F32).reshape(nb, blk, dh), axis=1)
    if nbp > nb:
        kmean = jnp.concatenate([kmean, jnp.zeros((nbp - nb, dh), F32)], axis=0)
    gate_t = lax.dot_general(kmean.astype(BF16), q, contract_last, preferred_element_type=F32)
    ones_rows = (lax.broadcasted_iota(I32, (ATTN_ONES_ROWS, seq), 0) == 0).astype(BF16)
    vt = jnp.concatenate([v.astype(F32).T.astype(BF16), ones_rows], axis=0)
    return gate_t, vt


def _attn_scores(q, k, gate_t, i, s_ref, *, blk, topk, nbp):
    contract_last = (((1,), (1,)), ((), ()))
    neg_inf = jnp.float32(-jnp.inf)
    sub = lax.broadcasted_iota(I32, (nbp, blk), 0)
    qi = q[i * blk:(i + 1) * blk]
    nk = (i + 1) * blk
    st = lax.dot_general(k[:nk], qi, contract_last, preferred_element_type=F32)
    if i > topk:
        gm = jnp.where(sub < i, gate_t[:, i * blk:(i + 1) * blk], neg_inf)
        rank = jnp.zeros((nbp, blk), I32)
        for other in range(i):
            g_other = gm[other:other + 1, :]
            beats = (g_other > gm) | ((g_other == gm) & (sub > other))
            rank = rank + beats.astype(I32)
        bias_t = jnp.where((sub < i) & (rank < topk), 0.0, neg_inf).astype(F32)
    m = None
    for n in range(i + 1):
        t = st[n * blk:(n + 1) * blk]
        if n == i:
            key_r = lax.broadcasted_iota(I32, (blk, blk), 0)
            qry_c = lax.broadcasted_iota(I32, (blk, blk), 1)
            t = jnp.where(key_r <= qry_c, t, neg_inf)
        elif i > topk:
            t = t + bias_t[n:n + 1, :]
        s_ref[n * blk:(n + 1) * blk, :] = t
        t_max = jnp.max(t, axis=0, keepdims=True)
        m = t_max if m is None else jnp.maximum(m, t_max)
    return m


def _attn_output(vt, m, i, s_ref, *, blk, dh):
    nk = (i + 1) * blk
    p = jnp.exp2(s_ref[:nk, :] - m).astype(BF16)
    ot = jnp.dot(vt[:, :nk], p, preferred_element_type=F32)
    return (ot[:dh] * (1.0 / ot[dh:dh + 1])).T


def _attn_body(q_ref, k_ref, v_ref, o_ref, s_scr, *, seq, blk, topk, dh):
    nb = seq // blk
    nbp = -(-nb // SUBLANES) * SUBLANES
    heads = []
    for hh in range(q_ref.shape[1] // dh):
        c0 = hh * dh
        q, k, v = q_ref[:, c0:c0 + dh], k_ref[:, c0:c0 + dh], v_ref[:, c0:c0 + dh]
        heads.append((c0, q, k) + _attn_prep(q, k, v, seq=seq, blk=blk))
    units = [(h, i) for h in range(len(heads)) for i in range(nb)]

    n_buf = s_scr.shape[0]
    ahead = n_buf - 1

    def scores(u):
        h, i = units[u]
        _, q, k, gate_t, _ = heads[h]
        return _attn_scores(q, k, gate_t, i, s_scr.at[u % n_buf], blk=blk, topk=topk, nbp=nbp)

    col_max = {u: scores(u) for u in range(min(ahead, len(units)))}
    for u, (h, i) in enumerate(units):
        if u + ahead < len(units):
            col_max[u + ahead] = scores(u + ahead)
        c0, _, _, _, vt = heads[h]
        o = _attn_output(vt, col_max.pop(u), i, s_scr.at[u % n_buf], blk=blk, dh=dh)
        o_ref[i * blk:(i + 1) * blk, c0:c0 + dh] = o.astype(o_ref.dtype)


def _moba_attention(u, *, batch, seq, n_heads, head_dim, blk, topk, hps):
    T = u.shape[0]
    assert seq % blk == 0 and blk % LANES == 0 and n_heads % hps == 0
    n_hb = n_heads // hps
    body = functools.partial(_attn_body, seq=seq, blk=blk, topk=topk, dh=head_dim)
    return pl.pallas_call(
        body,
        grid=(batch, n_hb),
        in_specs=[
            pl.BlockSpec((seq, hps * head_dim), lambda b, h: (b, h)),
            pl.BlockSpec((seq, hps * head_dim), lambda b, h: (b, n_hb + h)),
            pl.BlockSpec((seq, hps * head_dim), lambda b, h: (b, 2 * n_hb + h)),
        ],
        out_specs=pl.BlockSpec((seq, hps * head_dim), lambda b, h: (b, h)),
        out_shape=jax.ShapeDtypeStruct((T, n_heads * head_dim), BF16),
        scratch_shapes=[pltpu.VMEM((ATTN_LOOKAHEAD + 1, seq, blk), F32)],
        compiler_params=_cparams(("parallel", "parallel")),
        name="moba_attention",
    )(u, u, u)


def _conv_body(x_ref, w_ref, bdw_ref, g_ref, b_ref, o_ref, win_ref, conv_ref,
               *, width, ts, rt, halo, eps):
    ng = win_ref.shape[0]
    sub = SUBLANES

    @pl.when(pl.program_id(1) == 0)
    def _zero_halo():
        win_ref[:, 0:halo * sub, :] = jnp.zeros((ng, halo * sub, LANES), F32)

    for g in range(ng):
        for s in range(sub):
            c0 = (g * sub + s) * LANES
            win_ref[g, pl.ds(halo * sub + s, ts, stride=sub), :] = x_ref[:, c0:c0 + LANES].astype(F32)

    first = halo - (width - 1)
    for g in range(ng):
        def chunk(r, carry, g=g):
            t0 = r * rt
            acc = jnp.broadcast_to(bdw_ref[g][None], (rt, sub, LANES))
            for j in range(width):
                start = pl.multiple_of((t0 + first + j) * sub, sub)
                slab = win_ref[g, pl.ds(start, rt * sub), :].reshape(rt, sub, LANES)
                acc = acc + slab * w_ref[g, j][None]
            conv_ref[g, pl.ds(pl.multiple_of(t0 * sub, sub), rt * sub), :] = acc.reshape(rt * sub, LANES)
            return carry

        lax.fori_loop(0, ts // rt, chunk, 0)

    win_ref[:, 0:halo * sub, :] = win_ref[:, ts * sub:(ts + halo) * sub, :]

    def channel_chunks():
        for g in range(ng):
            for s in range(sub):
                yield (g * sub + s) * LANES, conv_ref[g, pl.ds(s, ts, stride=sub), :]

    n = ng * sub * LANES
    total = jnp.zeros((ts, 1), F32)
    for _, y in channel_chunks():
        total = total + jnp.sum(y, axis=1, keepdims=True)
    mu = total * (1.0 / n)
    sq = jnp.zeros((ts, 1), F32)
    for _, y in channel_chunks():
        d = y - mu
        sq = sq + jnp.sum(d * d, axis=1, keepdims=True)
    inv = lax.rsqrt(sq * (1.0 / n) + eps)
    for c0, y in channel_chunks():
        z = (y - mu) * inv * g_ref[:, c0:c0 + LANES] + b_ref[:, c0:c0 + LANES]
        o_ref[:, c0:c0 + LANES] = (z * _sigmoid(z)).astype(o_ref.dtype)


def _conv_branch(u, w_dw, b_dw, ln_g, ln_b, *, batch, seq, conv_c, col_block, ts, rt, halo):
    T = u.shape[0]
    width = w_dw.shape[0]
    gw = SUBLANES * LANES
    assert seq % ts == 0 and conv_c % gw == 0 and ts % rt == 0
    assert halo >= width - 1 and ts >= halo
    ng = conv_c // gw
    w_t = w_dw.reshape(width, ng, SUBLANES, LANES).transpose(1, 0, 2, 3)
    n_s = seq // ts
    body = functools.partial(_conv_body, width=width, ts=ts, rt=rt, halo=halo, eps=LN_EPS)
    return pl.pallas_call(
        body,
        grid=(batch, n_s),
        in_specs=[
            pl.BlockSpec((ts, conv_c), lambda b, s: (b * n_s + s, col_block)),
            pl.BlockSpec((ng, width, SUBLANES, LANES), lambda b, s: (0, 0, 0, 0)),
            pl.BlockSpec((ng, SUBLANES, LANES), lambda b, s: (0, 0, 0)),
            pl.BlockSpec((1, conv_c), lambda b, s: (0, 0)),
            pl.BlockSpec((1, conv_c), lambda b, s: (0, 0)),
        ],
        out_specs=pl.BlockSpec((ts, conv_c), lambda b, s: (b * n_s + s, 0)),
        out_shape=jax.ShapeDtypeStruct((T, conv_c), BF16),
        scratch_shapes=[pltpu.VMEM((ng, (halo + ts) * SUBLANES, LANES), F32),
                        pltpu.VMEM((ng, ts * SUBLANES, LANES), F32)],
        compiler_params=_cparams(("parallel", "arbitrary")),
        name="conv_branch",
    )(u, w_t, b_dw.reshape(ng, SUBLANES, LANES), ln_g[None, :], ln_b[None, :])


def _post_body(o_ref, hc_ref, ga_ref, gb_ref, x_ref, wo_ref, wp_ref, bp_ref, wout_ref,
               g1_ref, b1_ref, wrh_ref, wrl_ref, br_ref, h_ref, hp_ref, rt_ref, m_scr, z_scr,
               *, alpha, eps, n_groups, e_per_group):
    i = pl.program_id(0)

    @pl.when(i == 0)
    def _no_previous_tiles():
        m_scr[1] = jnp.zeros(m_scr.shape[1:], BF16)
        z_scr[1] = jnp.zeros(z_scr.shape[1:], F32)
        z_scr[2] = jnp.zeros(z_scr.shape[1:], F32)

    ya = jnp.dot(o_ref[...], wo_ref[...], preferred_element_type=F32)

    z = z_scr[(i + 1) % 3]
    mu = jnp.mean(z, axis=1, keepdims=True)
    zc = z - mu
    var = jnp.mean(zc * zc, axis=1, keepdims=True)
    h = zc * lax.rsqrt(var + eps) * g1_ref[...] + b1_ref[...]
    h_ref[...] = h
    half = h.shape[1] // 2
    _store_tile_rows(hp_ref, (), _pack_bf16_pair(h[:, :half], h[:, half:]))

    yc = jnp.dot(hc_ref[...], wp_ref[...], preferred_element_type=F32) + bp_ref[...]

    h_hi = h.astype(BF16)
    h_lo = (h - h_hi.astype(F32)).astype(BF16)
    hi_terms = jnp.dot(h_hi, wrl_ref[...], preferred_element_type=F32)
    logits = (hi_terms[:, :LANES] + hi_terms[:, LANES:]
              + jnp.dot(h_lo, wrh_ref[...], preferred_element_type=F32) + br_ref[...])
    tm = logits.shape[0]
    lane = lax.broadcasted_iota(I32, (tm, LANES), 1)
    neg_inf = jnp.float32(-jnp.inf)
    big = jnp.int32(LANES)

    def first_argmax(vals):
        top = jnp.max(vals, axis=1, keepdims=True)
        idx = jnp.min(jnp.where(vals == top, lane, big), axis=1, keepdims=True)
        return top, idx

    gl = jnp.where(lane < n_groups, logits, neg_inf)
    gmax, grp = first_argmax(gl)
    grp_w = 1.0 / jnp.sum(jnp.exp(gl - gmax), axis=1, keepdims=True)
    lo_lane = n_groups + grp * e_per_group
    el = jnp.where((lane >= lo_lane) & (lane < lo_lane + e_per_group), logits, neg_inf)
    v1, i1 = first_argmax(el)
    v2, i2 = first_argmax(jnp.where(lane == i1, neg_inf, el))
    t = jnp.exp(v2 - v1)
    p1 = 1.0 / (1.0 + t)
    c1 = p1 * grp_w
    c2 = (t * p1) * grp_w
    e1 = (i1 - n_groups).astype(F32)
    e2 = (i2 - n_groups).astype(F32)
    rt_ref[...] = jnp.where(lane == 0, e1, jnp.where(lane == 1, e2,
                            jnp.where(lane == 2, c1, jnp.where(lane == 3, c2, 0.0))))

    prev = (i + 1) % 2
    z_scr[(i + 2) % 3] += jnp.dot(m_scr[prev], wout_ref[...], preferred_element_type=F32)
    m_scr[i % 2] = (ga_ref[...].astype(F32) * ya + gb_ref[...].astype(F32) * yc).astype(BF16)
    z_scr[i % 3] = alpha * x_ref[...]


def _post_block(o, hc, u, x2, wo, wp, bp, wout, g1, b1, wr_hi, wr_lo, br, *, gate_block,
                alpha, n_groups, e_per_group, tm):
    T, D = x2.shape
    A = o.shape[1]
    C = hc.shape[1]
    assert T % tm == 0 and D == 2 * SUBLANES * LANES, "packed rows are stored as one (8, 128) tile"
    assert n_groups * (1 + e_per_group) <= LANES
    const = lambda i: (0, 0)
    resident = lambda shape: pl.BlockSpec(shape, const, pipeline_mode=pl.Buffered(1))
    body = functools.partial(_post_body, alpha=alpha, eps=LN_EPS, n_groups=n_groups,
                             e_per_group=e_per_group)
    n = T // tm
    cur = lambda i: jnp.minimum(i, n - 1)
    prev = lambda i: jnp.maximum(i - 2, 0)
    return pl.pallas_call(
        body,
        grid=(n + 2,),
        in_specs=[
            pl.BlockSpec((tm, A), lambda i: (cur(i), 0)),
            pl.BlockSpec((tm, C), lambda i: (cur(i), 0)),
            pl.BlockSpec((tm, D), lambda i: (cur(i), gate_block)),
            pl.BlockSpec((tm, D), lambda i: (cur(i), gate_block + 1)),
            pl.BlockSpec((tm, D), lambda i: (cur(i), 0)),
            resident((A, D)),
            resident((C, D)),
            resident((1, D)),
            resident((D, D)),
            resident((1, D)),
            resident((1, D)),
            resident((D, LANES)),
            resident((D, 2 * LANES)),
            resident((1, LANES)),
        ],
        out_specs=[
            pl.BlockSpec((tm, D), lambda i: (prev(i), 0)),
            pl.BlockSpec((tm * SUBLANES, LANES), lambda i: (prev(i), 0)),
            pl.BlockSpec((tm, LANES), lambda i: (prev(i), 0)),
        ],
        out_shape=[
            jax.ShapeDtypeStruct((T, D), F32),
            jax.ShapeDtypeStruct((T * SUBLANES, LANES), U32),
            jax.ShapeDtypeStruct((T, LANES), F32),
        ],
        scratch_shapes=[pltpu.VMEM((2, tm, D), BF16), pltpu.VMEM((3, tm, D), F32)],
        compiler_params=_cparams(("arbitrary",)),
        name="merge_project_route",
    )(o, hc, u, u, x2, wo, wp, bp, wout, g1, b1, wr_hi, wr_lo, br)


def _row_copy(src_ref, src_row8, dst_ref, dst_row8, sem):
    aligned = lambda v: v if isinstance(v, int) else pl.multiple_of(v, SUBLANES)
    return pltpu.make_async_copy(src_ref.at[pl.ds(aligned(src_row8), SUBLANES)],
                                 dst_ref.at[pl.ds(aligned(dst_row8), SUBLANES)], sem)


def _moe_body(es_ref, ec_ref, src_ref, dst_ref, hp_ref, w1_ref, w3_ref, w2_ref, y2_ref,
              w1b, w3b, w2b, xbuf, ybuf, src, dst, gsem, ssem, src_sem, dst_sem,
              *, n_experts, nch, bm, spare_row0):
    g = pl.program_id(0)
    c = pl.program_id(1)
    fc = w1_ref.shape[1]
    half = SUBLANES * LANES
    n_total = es_ref[n_experts - 1] + ec_ref[n_experts - 1]

    def src_copy(q):
        return pltpu.make_async_copy(src_ref.at[q], src.at[q % 2], src_sem.at[q % 2])

    def dst_copy(q):
        return pltpu.make_async_copy(dst_ref.at[q + 1], dst.at[(q + 1) % 2], dst_sem.at[(q + 1) % 2])

    def gather_row(q, r):
        return _row_copy(hp_ref, src[q % 2, 0, r], xbuf.at[q % 3], r * SUBLANES, gsem.at[q % 3])

    def scatter_row(q, r):
        return _row_copy(ybuf.at[(q + 3) % 3], r * SUBLANES, y2_ref, dst[(q + 1) % 2, 0, r],
                         ssem.at[(q + 3) % 3])

    def gather_wait(q):
        for r in range(bm):
            _row_copy(hp_ref, 0, xbuf.at[q % 3], r * SUBLANES, gsem.at[q % 3]).wait()

    def scatter_wait(q):
        for r in range(bm):
            _row_copy(ybuf.at[(q + 3) % 3], r * SUBLANES, y2_ref, 0, ssem.at[(q + 3) % 3]).wait()

    @pl.when((g == 0) & (c == 0))
    def _prime():
        ybuf[...] = jnp.zeros(ybuf.shape, U32)
        spare = [pltpu.make_async_copy(
            ybuf.at[s], y2_ref.at[pl.ds((spare_row0 + s * bm) * SUBLANES, bm * SUBLANES)], ssem.at[s])
            for s in range(2)]
        for cp in spare:
            cp.start()
        for cp in spare:
            cp.wait()
        dst_copy(-1).start()
        for q in range(2):
            src_copy(q).start()
        for q in range(2):
            src_copy(q).wait()
            for r in range(bm):
                gather_row(q, r).start()
        src_copy(2).start()

    @pl.when(g < n_experts)
    def _cast_next_expert_chunk():
        slot = g % 2
        w1b[slot, c] = w1_ref[...].astype(BF16)
        w3b[slot, c] = w3_ref[...].astype(BF16)
        w2b[slot, pl.ds(pl.multiple_of(c * fc, fc), fc), :] = w2_ref[...].astype(BF16)

    @pl.when(g >= 1)
    def _compute_previous_expert_share():
        e = g - 1
        slot = e % 2
        nb = ec_ref[e]
        base = es_ref[e]
        n_out = w2b.shape[2] // MOE_OUT_CHUNK

        def block(r, carry):
            q = base + r
            src_copy(q + 2).wait()
            src_copy(q + 3).start()
            dst_copy(q - 1).wait()
            dst_copy(q).start()
            gather_wait(q)

            @pl.when(q >= 2)
            def _():
                scatter_wait(q - 3)

            pending = []
            for rr in range(bm):
                pending.append(scatter_row(q - 1, rr))
                pending.append(gather_row(q + 2, rr))
            per_gap = -(-len(pending) // (4 * nch + n_out))

            def issue_some():
                for _ in range(min(per_gap, len(pending))):
                    pending.pop(0).start()

            u = _load_tile_rows(xbuf, (q % 3,), bm)
            x_lo = _unpack_lo(u).astype(BF16)
            x_hi = _unpack_hi(u).astype(BF16)
            hs = []
            for cc in range(nch):
                a = jnp.dot(x_lo, w1b[slot, cc, :half, :], preferred_element_type=F32)
                issue_some()
                a = a + jnp.dot(x_hi, w1b[slot, cc, half:, :], preferred_element_type=F32)
                issue_some()
                b = jnp.dot(x_lo, w3b[slot, cc, :half, :], preferred_element_type=F32)
                issue_some()
                b = b + jnp.dot(x_hi, w3b[slot, cc, half:, :], preferred_element_type=F32)
                issue_some()
                hs.append((a * _sigmoid(a) * b).astype(BF16))
            hmid = jnp.concatenate(hs, axis=1)
            ys = []
            for oc in range(n_out):
                cols = slice(oc * MOE_OUT_CHUNK, (oc + 1) * MOE_OUT_CHUNK)
                ys.append(jnp.dot(hmid, w2b[slot, :, cols], preferred_element_type=F32))
                issue_some()
            assert not pending
            y = jnp.concatenate(ys, axis=1)
            _store_tile_rows(ybuf, (q % 3,), _pack_bf16_pair(y[:, :half], y[:, half:]))
            return carry

        lax.fori_loop((nb * c) // nch, (nb * (c + 1)) // nch, block, 0)

    @pl.when((g == n_experts) & (c == nch - 1))
    def _drain():
        dst_copy(n_total - 1).wait()
        for r in range(bm):
            scatter_row(n_total - 1, r).start()

        @pl.when(n_total >= 2)
        def _():
            scatter_wait(n_total - 3)

        @pl.when(n_total >= 1)
        def _():
            scatter_wait(n_total - 2)

        scatter_wait(n_total - 1)
        gather_wait(n_total)
        gather_wait(n_total + 1)
        src_copy(n_total + 2).wait()


def _moe_experts(eb_start, eb_count, src_plan, dst_plan, hp, w1, w3, w2, *, n_tokens, bm, nch):
    n_out_rows = 2 * n_tokens + 2 * bm
    E, D, F = w1.shape
    assert D == 2 * SUBLANES * LANES and hp.shape[1] == LANES and D % MOE_OUT_CHUNK == 0
    assert F % nch == 0 and src_plan.shape[1:] == (1, bm) and dst_plan.shape[1:] == (1, bm)
    assert dst_plan.shape[0] == src_plan.shape[0] + 1
    fc = F // nch

    def w_in_idx(g, c, es, ec):
        return (jnp.minimum(g, E - 1), 0, jnp.where(g < E, c, nch - 1))

    def w_out_idx(g, c, es, ec):
        return (jnp.minimum(g, E - 1), jnp.where(g < E, c, nch - 1), 0)

    grid_spec = pltpu.PrefetchScalarGridSpec(
        num_scalar_prefetch=2,
        grid=(E + 1, nch),
        in_specs=[
            pl.BlockSpec(memory_space=pl.ANY),
            pl.BlockSpec(memory_space=pl.ANY),
            pl.BlockSpec(memory_space=pl.ANY),
            pl.BlockSpec((None, D, fc), w_in_idx),
            pl.BlockSpec((None, D, fc), w_in_idx),
            pl.BlockSpec((None, fc, D), w_out_idx),
        ],
        out_specs=pl.BlockSpec(memory_space=pl.ANY),
        scratch_shapes=[
            pltpu.VMEM((2, nch, D, fc), BF16),
            pltpu.VMEM((2, nch, D, fc), BF16),
            pltpu.VMEM((2, F, D), BF16),
            pltpu.VMEM((3, bm * SUBLANES, LANES), U32),
            pltpu.VMEM((3, bm * SUBLANES, LANES), U32),
            pltpu.SMEM((2, 1, bm), I32),
            pltpu.SMEM((2, 1, bm), I32),
            pltpu.SemaphoreType.DMA((3,)),
            pltpu.SemaphoreType.DMA((3,)),
            pltpu.SemaphoreType.DMA((2,)),
            pltpu.SemaphoreType.DMA((2,)),
        ],
    )
    body = functools.partial(_moe_body, n_experts=E, nch=nch, bm=bm, spare_row0=2 * n_tokens)
    return pl.pallas_call(
        body,
        grid_spec=grid_spec,
        out_shape=jax.ShapeDtypeStruct((n_out_rows * SUBLANES, LANES), U32),
        compiler_params=_cparams(("arbitrary", "arbitrary")),
        name="moe_experts",
    )(eb_start, eb_count, src_plan, dst_plan, hp, w1, w3, w2)


def _combine_body(h_ref, rt_ref, g_ref, b_ref, y0_ref, y1_ref, o_ref, *, alpha, eps):
    tt = h_ref.shape[0]
    u0 = _load_tile_rows(y0_ref, (), tt)
    u1 = _load_tile_rows(y1_ref, (), tt)
    half = u0.shape[1]
    c0 = rt_ref[:, 2:3]
    c1 = rt_ref[:, 3:4]
    z_lo = alpha * h_ref[:, :half] + (_unpack_lo(u0) * c0 + _unpack_lo(u1) * c1)
    z_hi = alpha * h_ref[:, half:] + (_unpack_hi(u0) * c0 + _unpack_hi(u1) * c1)
    n = 2 * half
    mu = (jnp.sum(z_lo, axis=1, keepdims=True) + jnp.sum(z_hi, axis=1, keepdims=True)) * (1.0 / n)
    d_lo = z_lo - mu
    d_hi = z_hi - mu
    var = (jnp.sum(d_lo * d_lo, axis=1, keepdims=True)
           + jnp.sum(d_hi * d_hi, axis=1, keepdims=True)) * (1.0 / n)
    inv = lax.rsqrt(var + eps)
    o_ref[:, :half] = d_lo * inv * g_ref[:, :half] + b_ref[:, :half]
    o_ref[:, half:] = d_hi * inv * g_ref[:, half:] + b_ref[:, half:]


def _combine(h, rt, y2, g2, b2, *, alpha, tt):
    T, D = h.shape
    assert T % tt == 0 and D == 2 * SUBLANES * LANES and y2.shape[1] == LANES
    body = functools.partial(_combine_body, alpha=alpha, eps=LN_EPS)
    return pl.pallas_call(
        body,
        grid=(T // tt,),
        in_specs=[
            pl.BlockSpec((tt, D), lambda i: (i, 0)),
            pl.BlockSpec((tt, LANES), lambda i: (i, 0)),
            pl.BlockSpec((1, D), lambda i: (0, 0)),
            pl.BlockSpec((1, D), lambda i: (0, 0)),
            pl.BlockSpec((tt * SUBLANES, LANES), lambda i: (i, 0)),
            pl.BlockSpec((tt * SUBLANES, LANES), lambda i: (i + T // tt, 0)),
        ],
        out_specs=pl.BlockSpec((tt, D), lambda i: (i, 0)),
        out_shape=jax.ShapeDtypeStruct((T, D), F32),
        compiler_params=_cparams(("parallel",)),
        name="moe_combine",
    )(h, rt, g2, b2, y2, y2)


def _slot_plan(expert, n_experts, bm, n_plan_blocks):
    T = expert.shape[0]
    e_flat = expert.reshape(-1)
    onehot = (e_flat[:, None] == jnp.arange(n_experts, dtype=I32)[None, :]).astype(I32)
    csum = jnp.cumsum(onehot, axis=0)
    counts = csum[-1]
    eb_count = (counts + bm - 1) // bm
    eb_start = jnp.cumsum(eb_count) - eb_count
    slot = jnp.sum(onehot * (csum - 1 + (eb_start * bm)[None, :]), axis=1)
    n_slots = n_plan_blocks * bm
    tok = jnp.arange(2 * T, dtype=I32) // 2
    k = jnp.arange(2 * T, dtype=I32) % 2
    p = jnp.arange(n_slots, dtype=I32)
    spare = 2 * T + ((p // bm) % 2) * bm + p % bm
    dst = spare.at[slot].set(k * T + tok, unique_indices=True)
    src = jnp.where(dst < 2 * T, dst % T, 0)
    dst = jnp.concatenate([2 * T + bm + jnp.arange(bm, dtype=I32), dst])
    to_blocks = lambda a: (a * SUBLANES).reshape(-1, 1, bm)
    return eb_start.astype(I32), eb_count.astype(I32), to_blocks(src), to_blocks(dst)


def _layer(x2, p, *, batch, seq, depth, n_heads, head_dim, moba_block, moba_topk,
           n_groups, e_per_group, tiles):
    T, D = x2.shape
    A = n_heads * head_dim
    C = p["w_dw"].shape[1]
    alpha = (2.0 * depth) ** 0.25
    tn = tiles["proj_tn"]
    assert A == C == D, "column-block addressing below assumes equal branch widths"

    w_in, b_in = p["w_in"], p["b_in"]

    half = head_dim // 2
    inv_freq = jnp.power(ROPE_THETA, -jnp.arange(half, dtype=F32) * (2.0 / head_dim))
    ang = jnp.arange(seq, dtype=F32)[:, None] * inv_freq[None, :]
    cos2 = jnp.concatenate([jnp.cos(ang), jnp.cos(ang)], axis=1)
    sin_s = jnp.concatenate([-jnp.sin(ang), jnp.sin(ang)], axis=1)

    u = _in_projection(x2, w_in.astype(BF16), b_in[None, :], cos2, sin_s, seq=seq, attn_w=A, conv_c=C,
                       head_dim=head_dim, tm=tiles["proj_tm"], tn=tn)
    o = _moba_attention(u, batch=batch, seq=seq, n_heads=n_heads, head_dim=head_dim,
                        blk=moba_block, topk=moba_topk, hps=ATTN_HPS)
    hc = _conv_branch(u, p["w_dw"], p["b_dw"], p["conv_ln_g"], p["conv_ln_b"], batch=batch,
                      seq=seq, conv_c=C, col_block=3, ts=tiles["conv_ts"], rt=tiles["conv_rt"],
                      halo=CONV_HALO)

    n_experts = n_groups * e_per_group
    n_route = n_groups + n_experts
    w_route = jnp.concatenate(
        [p["w_rg"], p["w_re"].transpose(1, 0, 2).reshape(D, n_experts),
         jnp.zeros((D, LANES - n_route), F32)], axis=1)
    b_route = jnp.concatenate(
        [p["b_rg"], p["b_re"].reshape(n_experts), jnp.zeros((LANES - n_route,), F32)])[None, :]
    wr_hi = w_route.astype(BF16)
    wr_lo = jnp.concatenate([wr_hi, (w_route - wr_hi.astype(F32)).astype(BF16)], axis=1)

    h, hp, rt = _post_block(
        o, hc, u, x2, p["w_o_attn"].astype(BF16), p["w_pw2"].astype(BF16), p["b_pw2"][None, :],
        p["w_out"].astype(BF16), p["ln1_g"][None, :], p["ln1_b"][None, :], wr_hi, wr_lo, b_route,
        gate_block=4, alpha=alpha, n_groups=n_groups, e_per_group=e_per_group,
        tm=tiles["post_tm"])

    bm = tiles["moe_bm"]
    n_plan_blocks = -(-2 * T // bm) + n_experts + 3
    expert = rt[:, :2].astype(I32)
    eb_start, eb_count, src_plan, dst_plan = _slot_plan(expert, n_experts, bm, n_plan_blocks)
    y2 = _moe_experts(eb_start, eb_count, src_plan, dst_plan, hp, p["w1"], p["w3"], p["w2"],
                      n_tokens=T, bm=bm, nch=tiles["moe_nch"])
    return _combine(h, rt, y2, p["ln2_g"][None, :], p["ln2_b"][None, :], alpha=alpha,
                    tt=tiles["comb_tt"])


_PARAM_NAMES = ("w_in", "b_in", "w_o_attn", "w_dw", "b_dw", "conv_ln_g", "conv_ln_b", "w_pw2",
                "b_pw2", "w_out", "ln1_g", "ln1_b", "w_rg", "b_rg", "w_re", "b_re", "w1", "w3",
                "w2", "ln2_g", "ln2_b")

_TILES = dict(proj_tm=PROJ_TM, proj_tn=PROJ_TN, conv_ts=CONV_TS, conv_rt=CONV_RT,
              post_tm=POST_TM, comb_tt=COMB_TT, moe_bm=MOE_BM, moe_nch=MOE_NCH)


def _forward(x, params, *, n_heads=N_HEADS, head_dim=HEAD_DIM, moba_block=MOBA_BLOCK,
             moba_topk=MOBA_TOPK, n_groups=N_GROUPS, e_per_group=EXPERTS_PER_GROUP, tiles=None):
    tiles = dict(_TILES, **(tiles or {}))
    B, S, D = x.shape
    depth = params["w_in"].shape[0]
    x2 = x.reshape(B * S, D)
    for l in range(depth):
        p = {k: v[l] for k, v in params.items()}
        x2 = _layer(x2, p, batch=B, seq=S, depth=depth, n_heads=n_heads, head_dim=head_dim,
                    moba_block=moba_block, moba_topk=moba_topk, n_groups=n_groups,
                    e_per_group=e_per_group, tiles=tiles)
    return x2.reshape(B, S, D)


def kernel(x, w_in, b_in, w_o_attn, w_dw, b_dw, conv_ln_g, conv_ln_b, w_pw2, b_pw2, w_out, ln1_g,
           ln1_b, w_rg, b_rg, w_re, b_re, w1, w3, w2, ln2_g, ln2_b):
    params = dict(zip(_PARAM_NAMES, (w_in, b_in, w_o_attn, w_dw, b_dw, conv_ln_g, conv_ln_b, w_pw2,
                                     b_pw2, w_out, ln1_g, ln1_b, w_rg, b_rg, w_re, b_re, w1, w3,
                                     w2, ln2_g, ln2_b)))
    return _forward(x, params)
```

```python
import functools

import jax
import jax.numpy as jnp
from jax import lax
from jax.experimental import pallas as pl
from jax.experimental.pallas import tpu as pltpu

F32 = jnp.float32
BF16 = jnp.bfloat16
U32 = jnp.uint32
I32 = jnp.int32

N_HEADS = 16
HEAD_DIM = 128
ROPE_THETA = 10000.0
MOBA_BLOCK = 256
MOBA_TOPK = 3
CONV_WIDTH = 31
N_GROUPS = 4
EXPERTS_PER_GROUP = 8
LN_EPS = 1e-5
LOG2_E = 1.4426950408889634

LANES = 128
SUBLANES = 8
VMEM_LIMIT = 56 * 1024 * 1024

PROJ_TM = 1024
PROJ_TN = 1024
ATTN_HPS = 2
ATTN_LOOKAHEAD = 2
ATTN_ONES_ROWS = 16
CONV_TS = 256
CONV_RT = 16
CONV_HALO = 32
POST_TM = 256
COMB_TT = 512
MOE_BM = 256
MOE_NCH = 4
MOE_OUT_CHUNK = 256


def _cparams(sem):
    return pltpu.CompilerParams(dimension_semantics=sem, vmem_limit_bytes=VMEM_LIMIT)


def _sigmoid(x):
    return 1.0 / (1.0 + jnp.exp(-x))


def _pack_bf16_pair(lo_f32, hi_f32):
    lo = lax.bitcast_convert_type(lo_f32.astype(BF16).astype(F32), U32) >> 16
    hi = lax.bitcast_convert_type(hi_f32.astype(BF16).astype(F32), U32)
    return hi | lo


def _unpack_lo(u):
    return lax.bitcast_convert_type(u << 16, F32)


def _unpack_hi(u):
    return lax.bitcast_convert_type(u & jnp.uint32(0xFFFF0000), F32)


def _store_tile_rows(ref, lead, x):
    n = x.shape[0]
    for s in range(SUBLANES):
        ref[(*lead, pl.ds(s, n, stride=SUBLANES), slice(None))] = x[:, s * LANES:(s + 1) * LANES]


def _load_tile_rows(ref, lead, n):
    return jnp.concatenate(
        [ref[(*lead, pl.ds(s, n, stride=SUBLANES), slice(None))] for s in range(SUBLANES)], axis=1)


def _inproj_body(x_ref, w_ref, wg_ref, b_ref, bg_ref, cos_ref, sin_ref, o_ref, xb_ref,
                 *, n_q, n_qk, n_qkv, n_glu, scale, head_dim):
    j = pl.program_id(1)
    tn = o_ref.shape[1]

    @pl.when(j == 0)
    def _cast():
        xb_ref[...] = x_ref[...].astype(BF16)

    def project(w, b):
        return jnp.dot(xb_ref[...], w[...], preferred_element_type=F32) + b[...]

    @pl.when(j < n_qk)
    def _rope():
        acc = project(w_ref, b_ref)
        s = jnp.where(j < n_q, scale, 1.0).astype(F32)
        cos = cos_ref[...] * s
        sin = sin_ref[...] * s
        for h in range(tn // head_dim):
            t = acc[:, h * head_dim:(h + 1) * head_dim]
            r = pltpu.roll(t, head_dim // 2, axis=1)
            o_ref[:, h * head_dim:(h + 1) * head_dim] = (t * cos + r * sin).astype(o_ref.dtype)

    @pl.when((j >= n_qk) & (j < n_qkv))
    def _plain():
        o_ref[...] = project(w_ref, b_ref).astype(o_ref.dtype)

    @pl.when((j >= n_qkv) & (j < n_qkv + n_glu))
    def _glu():
        a = project(w_ref, b_ref)
        g = project(wg_ref, bg_ref)
        o_ref[...] = (a * _sigmoid(g)).astype(o_ref.dtype)

    @pl.when(j >= n_qkv + n_glu)
    def _gate():
        o_ref[...] = _sigmoid(project(w_ref, b_ref)).astype(o_ref.dtype)


def _in_projection(x2, w_all, b_all, cos2, sin_s, *, seq, attn_w, conv_c, head_dim, tm, tn):
    T, D = x2.shape
    W = w_all.shape[1] - conv_c
    assert T % tm == 0 and seq % tm == 0 and W % tn == 0
    assert attn_w % tn == 0 and conv_c % tn == 0 and tn % head_dim == 0
    assert head_dim == LANES
    n_q = attn_w // tn
    n_glu = conv_c // tn
    n_qkv = 3 * n_q
    pos_tiles = seq // tm

    def main_idx(i, j):
        return (0, jnp.where(j < n_qkv + n_glu, j, j + n_glu))

    def g_idx(i, j):
        return (0, n_qkv + n_glu + jnp.clip(j - n_qkv, 0, n_glu - 1))

    body = functools.partial(_inproj_body, n_q=n_q, n_qk=2 * n_q, n_qkv=n_qkv, n_glu=n_glu,
                             scale=float(head_dim) ** -0.5 * LOG2_E, head_dim=head_dim)
    return pl.pallas_call(
        body,
        grid=(T // tm, W // tn),
        in_specs=[
            pl.BlockSpec((tm, D), lambda i, j: (i, 0)),
            pl.BlockSpec((D, tn), main_idx),
            pl.BlockSpec((D, tn), g_idx),
            pl.BlockSpec((1, tn), main_idx),
            pl.BlockSpec((1, tn), g_idx),
            pl.BlockSpec((tm, head_dim), lambda i, j: (i % pos_tiles, 0)),
            pl.BlockSpec((tm, head_dim), lambda i, j: (i % pos_tiles, 0)),
        ],
        out_specs=pl.BlockSpec((tm, tn), lambda i, j: (i, j)),
        out_shape=jax.ShapeDtypeStruct((T, W), BF16),
        scratch_shapes=[pltpu.VMEM((tm, D), BF16)],
        compiler_params=_cparams(("parallel", "arbitrary")),
        name="in_projection",
    )(x2, w_all, w_all, b_all, b_all, cos2, sin_s)


def _attn_prep(q, k, v, *, seq, blk):
    nb = seq // blk
    nbp = -(-nb // SUBLANES) * SUBLANES
    dh = q.shape[1]
    contract_last = (((1,), (1,)), ((), ()))
    kmean = jnp.mean(k.astype(F32).reshape(nb, blk, dh), axis=1)
    if nbp > nb:
        kmean = jnp.concatenate([kmean, jnp.zeros((nbp - nb, dh), F32)], axis=0)
    gate_t = lax.dot_general(kmean.astype(BF16), q, contract_last, preferred_element_type=F32)
    ones_rows = (lax.broadcasted_iota(I32, (ATTN_ONES_ROWS, seq), 0) == 0).astype(BF16)
    vt = jnp.concatenate([v.astype(F32).T.astype(BF16), ones_rows], axis=0)
    return gate_t, vt


def _attn_scores(q, k, gate_t, i, s_ref, *, blk, topk, nbp):
    contract_last = (((1,), (1,)), ((), ()))
    neg_inf = jnp.float32(-jnp.inf)
    sub = lax.broadcasted_iota(I32, (nbp, blk), 0)
    qi = q[i * blk:(i + 1) * blk]
    nk = (i + 1) * blk
    st = lax.dot_general(k[:nk], qi, contract_last, preferred_element_type=F32)
    if i > topk:
        gm = jnp.where(sub < i, gate_t[:, i * blk:(i + 1) * blk], neg_inf)
        rank = jnp.zeros((nbp, blk), I32)
        for other in range(i):
            g_other = gm[other:other + 1, :]
            beats = (g_other > gm) | ((g_other == gm) & (sub > other))
            rank = rank + beats.astype(I32)
        bias_t = jnp.where((sub < i) & (rank < topk), 0.0, neg_inf).astype(F32)
    m = None
    for n in range(i + 1):
        t = st[n * blk:(n + 1) * blk]
        if n == i:
            key_r = lax.broadcasted_iota(I32, (blk, blk), 0)
            qry_c = lax.broadcasted_iota(I32, (blk, blk), 1)
            t = jnp.where(key_r <= qry_c, t, neg_inf)
        elif i > topk:
            t = t + bias_t[n:n + 1, :]
        s_ref[n * blk:(n + 1) * blk, :] = t
        t_max = jnp.max(t, axis=0, keepdims=True)
        m = t_max if m is None else jnp.maximum(m, t_max)
    return m


def _attn_output(vt, m, i, s_ref, *, blk, dh):
    nk = (i + 1) * blk
    p = jnp.exp2(s_ref[:nk, :] - m).astype(BF16)
    ot = jnp.dot(vt[:, :nk], p, preferred_element_type=F32)
    return (ot[:dh] * (1.0 / ot[dh:dh + 1])).T


def _attn_body(q_ref, k_ref, v_ref, o_ref, s_scr, *, seq, blk, topk, dh):
    nb = seq // blk
    nbp = -(-nb // SUBLANES) * SUBLANES
    heads = []
    for hh in range(q_ref.shape[1] // dh):
        c0 = hh * dh
        q, k, v = q_ref[:, c0:c0 + dh], k_ref[:, c0:c0 + dh], v_ref[:, c0:c0 + dh]
        heads.append((c0, q, k) + _attn_prep(q, k, v, seq=seq, blk=blk))
    units = [(h, i) for h in range(len(heads)) for i in range(nb)]

    n_buf = s_scr.shape[0]
    ahead = n_buf - 1

    def scores(u):
        h, i = units[u]
        _, q, k, gate_t, _ = heads[h]
        return _attn_scores(q, k, gate_t, i, s_scr.at[u % n_buf], blk=blk, topk=topk, nbp=nbp)

    col_max = {u: scores(u) for u in range(min(ahead, len(units)))}
    for u, (h, i) in enumerate(units):
        if u + ahead < len(units):
            col_max[u + ahead] = scores(u + ahead)
        c0, _, _, _, vt = heads[h]
        o = _attn_output(vt, col_max.pop(u), i, s_scr.at[u % n_buf], blk=blk, dh=dh)
        o_ref[i * blk:(i + 1) * blk, c0:c0 + dh] = o.astype(o_ref.dtype)


def _moba_attention(u, *, batch, seq, n_heads, head_dim, blk, topk, hps):
    T = u.shape[0]
    assert seq % blk == 0 and blk % LANES == 0 and n_heads % hps == 0
    n_hb = n_heads // hps
    body = functools.partial(_attn_body, seq=seq, blk=blk, topk=topk, dh=head_dim)
    return pl.pallas_call(
        body,
        grid=(batch, n_hb),
        in_specs=[
            pl.BlockSpec((seq, hps * head_dim), lambda b, h: (b, h)),
            pl.BlockSpec((seq, hps * head_dim), lambda b, h: (b, n_hb + h)),
            pl.BlockSpec((seq, hps * head_dim), lambda b, h: (b, 2 * n_hb + h)),
        ],
        out_specs=pl.BlockSpec((seq, hps * head_dim), lambda b, h: (b, h)),
        out_shape=jax.ShapeDtypeStruct((T, n_heads * head_dim), BF16),
        scratch_shapes=[pltpu.VMEM((ATTN_LOOKAHEAD + 1, seq, blk), F32)],
        compiler_params=_cparams(("parallel", "parallel")),
        name="moba_attention",
    )(u, u, u)


def _conv_body(x_ref, w_ref, bdw_ref, g_ref, b_ref, o_ref, win_ref, conv_ref,
               *, width, ts, rt, halo, eps):
    ng = win_ref.shape[0]
    sub = SUBLANES

    @pl.when(pl.program_id(1) == 0)
    def _zero_halo():
        win_ref[:, 0:halo * sub, :] = jnp.zeros((ng, halo * sub, LANES), F32)

    for g in range(ng):
        for s in range(sub):
            c0 = (g * sub + s) * LANES
            win_ref[g, pl.ds(halo * sub + s, ts, stride=sub), :] = x_ref[:, c0:c0 + LANES].astype(F32)

    first = halo - (width - 1)
    for g in range(ng):
        def chunk(r, carry, g=g):
            t0 = r * rt
            acc = jnp.broadcast_to(bdw_ref[g][None], (rt, sub, LANES))
            for j in range(width):
                start = pl.multiple_of((t0 + first + j) * sub, sub)
                slab = win_ref[g, pl.ds(start, rt * sub), :].reshape(rt, sub, LANES)
                acc = acc + slab * w_ref[g, j][None]
            conv_ref[g, pl.ds(pl.multiple_of(t0 * sub, sub), rt * sub), :] = acc.reshape(rt * sub, LANES)
            return carry

        lax.fori_loop(0, ts // rt, chunk, 0)

    win_ref[:, 0:halo * sub, :] = win_ref[:, ts * sub:(ts + halo) * sub, :]

    def channel_chunks():
        for g in range(ng):
            for s in range(sub):
                yield (g * sub + s) * LANES, conv_ref[g, pl.ds(s, ts, stride=sub), :]

    n = ng * sub * LANES
    total = jnp.zeros((ts, 1), F32)
    for _, y in channel_chunks():
        total = total + jnp.sum(y, axis=1, keepdims=True)
    mu = total * (1.0 / n)
    sq = jnp.zeros((ts, 1), F32)
    for _, y in channel_chunks():
        d = y - mu
        sq = sq + jnp.sum(d * d, axis=1, keepdims=True)
    inv = lax.rsqrt(sq * (1.0 / n) + eps)
    for c0, y in channel_chunks():
        z = (y - mu) * inv * g_ref[:, c0:c0 + LANES] + b_ref[:, c0:c0 + LANES]
        o_ref[:, c0:c0 + LANES] = (z * _sigmoid(z)).astype(o_ref.dtype)


def _conv_branch(u, w_dw, b_dw, ln_g, ln_b, *, batch, seq, conv_c, col_block, ts, rt, halo):
    T = u.shape[0]
    width = w_dw.shape[0]
    gw = SUBLANES * LANES
    assert seq % ts == 0 and conv_c % gw == 0 and ts % rt == 0
    assert halo >= width - 1 and ts >= halo
    ng = conv_c // gw
    w_t = w_dw.reshape(width, ng, SUBLANES, LANES).transpose(1, 0, 2, 3)
    n_s = seq // ts
    body = functools.partial(_conv_body, width=width, ts=ts, rt=rt, halo=halo, eps=LN_EPS)
    return pl.pallas_call(
        body,
        grid=(batch, n_s),
        in_specs=[
            pl.BlockSpec((ts, conv_c), lambda b, s: (b * n_s + s, col_block)),
            pl.BlockSpec((ng, width, SUBLANES, LANES), lambda b, s: (0, 0, 0, 0)),
            pl.BlockSpec((ng, SUBLANES, LANES), lambda b, s: (0, 0, 0)),
            pl.BlockSpec((1, conv_c), lambda b, s: (0, 0)),
            pl.BlockSpec((1, conv_c), lambda b, s: (0, 0)),
        ],
        out_specs=pl.BlockSpec((ts, conv_c), lambda b, s: (b * n_s + s, 0)),
        out_shape=jax.ShapeDtypeStruct((T, conv_c), BF16),
        scratch_shapes=[pltpu.VMEM((ng, (halo + ts) * SUBLANES, LANES), F32),
                        pltpu.VMEM((ng, ts * SUBLANES, LANES), F32)],
        compiler_params=_cparams(("parallel", "arbitrary")),
        name="conv_branch",
    )(u, w_t, b_dw.reshape(ng, SUBLANES, LANES), ln_g[None, :], ln_b[None, :])


def _post_body(o_ref, hc_ref, ga_ref, gb_ref, x_ref, wo_ref, wp_ref, bp_ref, wout_ref,
               g1_ref, b1_ref, wrh_ref, wrl_ref, br_ref, h_ref, hp_ref, rt_ref, m_scr, z_scr,
               *, alpha, eps, n_groups, e_per_group):
    i = pl.program_id(0)

    @pl.when(i == 0)
    def _no_previous_tiles():
        m_scr[1] = jnp.zeros(m_scr.shape[1:], BF16)
        z_scr[1] = jnp.zeros(z_scr.shape[1:], F32)
        z_scr[2] = jnp.zeros(z_scr.shape[1:], F32)

    ya = jnp.dot(o_ref[...], wo_ref[...], preferred_element_type=F32)

    z = z_scr[(i + 1) % 3]
    mu = jnp.mean(z, axis=1, keepdims=True)
    zc = z - mu
    var = jnp.mean(zc * zc, axis=1, keepdims=True)
    h = zc * lax.rsqrt(var + eps) * g1_ref[...] + b1_ref[...]
    h_ref[...] = h
    half = h.shape[1] // 2
    _store_tile_rows(hp_ref, (), _pack_bf16_pair(h[:, :half], h[:, half:]))

    yc = jnp.dot(hc_ref[...], wp_ref[...], preferred_element_type=F32) + bp_ref[...]

    h_hi = h.astype(BF16)
    h_lo = (h - h_hi.astype(F32)).astype(BF16)
    hi_terms = jnp.dot(h_hi, wrl_ref[...], preferred_element_type=F32)
    logits = (hi_terms[:, :LANES] + hi_terms[:, LANES:]
              + jnp.dot(h_lo, wrh_ref[...], preferred_element_type=F32) + br_ref[...])
    tm = logits.shape[0]
    lane = lax.broadcasted_iota(I32, (tm, LANES), 1)
    neg_inf = jnp.float32(-jnp.inf)
    big = jnp.int32(LANES)

    def first_argmax(vals):
        top = jnp.max(vals, axis=1, keepdims=True)
        idx = jnp.min(jnp.where(vals == top, lane, big), axis=1, keepdims=True)
        return top, idx

    gl = jnp.where(lane < n_groups, logits, neg_inf)
    gmax, grp = first_argmax(gl)
    grp_w = 1.0 / jnp.sum(jnp.exp(gl - gmax), axis=1, keepdims=True)
    lo_lane = n_groups + grp * e_per_group
    el = jnp.where((lane >= lo_lane) & (lane < lo_lane + e_per_group), logits, neg_inf)
    v1, i1 = first_argmax(el)
    v2, i2 = first_argmax(jnp.where(lane == i1, neg_inf, el))
    t = jnp.exp(v2 - v1)
    p1 = 1.0 / (1.0 + t)
    c1 = p1 * grp_w
    c2 = (t * p1) * grp_w
    e1 = (i1 - n_groups).astype(F32)
    e2 = (i2 - n_groups).astype(F32)
    rt_ref[...] = jnp.where(lane == 0, e1, jnp.where(lane == 1, e2,
                            jnp.where(lane == 2, c1, jnp.where(lane == 3, c2, 0.0))))

    prev = (i + 1) % 2
    z_scr[(i + 2) % 3] += jnp.dot(m_scr[prev], wout_ref[...], preferred_element_type=F32)
    m_scr[i % 2] = (ga_ref[...].astype(F32) * ya + gb_ref[...].astype(F32) * yc).astype(BF16)
    z_scr[i % 3] = alpha * x_ref[...]


def _post_block(o, hc, u, x2, wo, wp, bp, wout, g1, b1, wr_hi, wr_lo, br, *, gate_block,
                alpha, n_groups, e_per_group, tm):
    T, D = x2.shape
    A = o.shape[1]
    C = hc.shape[1]
    assert T % tm == 0 and D == 2 * SUBLANES * LANES, "packed rows are stored as one (8, 128) tile"
    assert n_groups * (1 + e_per_group) <= LANES
    const = lambda i: (0, 0)
    resident = lambda shape: pl.BlockSpec(shape, const, pipeline_mode=pl.Buffered(1))
    body = functools.partial(_post_body, alpha=alpha, eps=LN_EPS, n_groups=n_groups,
                             e_per_group=e_per_group)
    n = T // tm
    cur = lambda i: jnp.minimum(i, n - 1)
    prev = lambda i: jnp.maximum(i - 2, 0)
    return pl.pallas_call(
        body,
        grid=(n + 2,),
        in_specs=[
            pl.BlockSpec((tm, A), lambda i: (cur(i), 0)),
            pl.BlockSpec((tm, C), lambda i: (cur(i), 0)),
            pl.BlockSpec((tm, D), lambda i: (cur(i), gate_block)),
            pl.BlockSpec((tm, D), lambda i: (cur(i), gate_block + 1)),
            pl.BlockSpec((tm, D), lambda i: (cur(i), 0)),
            resident((A, D)),
            resident((C, D)),
            resident((1, D)),
            resident((D, D)),
            resident((1, D)),
            resident((1, D)),
            resident((D, LANES)),
            resident((D, 2 * LANES)),
            resident((1, LANES)),
        ],
        out_specs=[
            pl.BlockSpec((tm, D), lambda i: (prev(i), 0)),
            pl.BlockSpec((tm * SUBLANES, LANES), lambda i: (prev(i), 0)),
            pl.BlockSpec((tm, LANES), lambda i: (prev(i), 0)),
        ],
        out_shape=[
            jax.ShapeDtypeStruct((T, D), F32),
            jax.ShapeDtypeStruct((T * SUBLANES, LANES), U32),
            jax.ShapeDtypeStruct((T, LANES), F32),
        ],
        scratch_shapes=[pltpu.VMEM((2, tm, D), BF16), pltpu.VMEM((3, tm, D), F32)],
        compiler_params=_cparams(("arbitrary",)),
        name="merge_project_route",
    )(o, hc, u, u, x2, wo, wp, bp, wout, g1, b1, wr_hi, wr_lo, br)


def _row_copy(src_ref, src_row8, dst_ref, dst_row8, sem):
    aligned = lambda v: v if isinstance(v, int) else pl.multiple_of(v, SUBLANES)
    return pltpu.make_async_copy(src_ref.at[pl.ds(aligned(src_row8), SUBLANES)],
                                 dst_ref.at[pl.ds(aligned(dst_row8), SUBLANES)], sem)


def _moe_body(es_ref, ec_ref, src_ref, dst_ref, hp_ref, w1_ref, w3_ref, w2_ref, y2_ref,
              w1b, w3b, w2b, xbuf, ybuf, src, dst, gsem, ssem, src_sem, dst_sem,
              *, n_experts, nch, bm, spare_row0):
    g = pl.program_id(0)
    c = pl.program_id(1)
    fc = w1_ref.shape[1]
    half = SUBLANES * LANES
    n_total = es_ref[n_experts - 1] + ec_ref[n_experts - 1]

    def src_copy(q):
        return pltpu.make_async_copy(src_ref.at[q], src.at[q % 2], src_sem.at[q % 2])

    def dst_copy(q):
        return pltpu.make_async_copy(dst_ref.at[q + 1], dst.at[(q + 1) % 2], dst_sem.at[(q + 1) % 2])

    def gather_row(q, r):
        return _row_copy(hp_ref, src[q % 2, 0, r], xbuf.at[q % 3], r * SUBLANES, gsem.at[q % 3])

    def scatter_row(q, r):
        return _row_copy(ybuf.at[(q + 3) % 3], r * SUBLANES, y2_ref, dst[(q + 1) % 2, 0, r],
                         ssem.at[(q + 3) % 3])

    def gather_wait(q):
        for r in range(bm):
            _row_copy(hp_ref, 0, xbuf.at[q % 3], r * SUBLANES, gsem.at[q % 3]).wait()

    def scatter_wait(q):
        for r in range(bm):
            _row_copy(ybuf.at[(q + 3) % 3], r * SUBLANES, y2_ref, 0, ssem.at[(q + 3) % 3]).wait()

    @pl.when((g == 0) & (c == 0))
    def _prime():
        ybuf[...] = jnp.zeros(ybuf.shape, U32)
        spare = [pltpu.make_async_copy(
            ybuf.at[s], y2_ref.at[pl.ds((spare_row0 + s * bm) * SUBLANES, bm * SUBLANES)], ssem.at[s])
            for s in range(2)]
        for cp in spare:
            cp.start()
        for cp in spare:
            cp.wait()
        dst_copy(-1).start()
        for q in range(2):
            src_copy(q).start()
        for q in range(2):
            src_copy(q).wait()
            for r in range(bm):
                gather_row(q, r).start()
        src_copy(2).start()

    @pl.when(g < n_experts)
    def _cast_next_expert_chunk():
        slot = g % 2
        w1b[slot, c] = w1_ref[...].astype(BF16)
        w3b[slot, c] = w3_ref[...].astype(BF16)
        w2b[slot, pl.ds(pl.multiple_of(c * fc, fc), fc), :] = w2_ref[...].astype(BF16)

    @pl.when(g >= 1)
    def _compute_previous_expert_share():
        e = g - 1
        slot = e % 2
        nb = ec_ref[e]
        base = es_ref[e]
        n_out = w2b.shape[2] // MOE_OUT_CHUNK

        def block(r, carry):
            q = base + r
            src_copy(q + 2).wait()
            src_copy(q + 3).start()
            dst_copy(q - 1).wait()
            dst_copy(q).start()
            gather_wait(q)

            @pl.when(q >= 2)
            def _():
                scatter_wait(q - 3)

            pending = []
            for rr in range(bm):
                pending.append(scatter_row(q - 1, rr))
                pending.append(gather_row(q + 2, rr))
            per_gap = -(-len(pending) // (4 * nch + n_out))

            def issue_some():
                for _ in range(min(per_gap, len(pending))):
                    pending.pop(0).start(priority=1)

            u = _load_tile_rows(xbuf, (q % 3,), bm)
            x_lo = _unpack_lo(u).astype(BF16)
            x_hi = _unpack_hi(u).astype(BF16)
            hs = []
            for cc in range(nch):
                a = jnp.dot(x_lo, w1b[slot, cc, :half, :], preferred_element_type=F32)
                issue_some()
                a = a + jnp.dot(x_hi, w1b[slot, cc, half:, :], preferred_element_type=F32)
                issue_some()
                b = jnp.dot(x_lo, w3b[slot, cc, :half, :], preferred_element_type=F32)
                issue_some()
                b = b + jnp.dot(x_hi, w3b[slot, cc, half:, :], preferred_element_type=F32)
                issue_some()
                hs.append((a * _sigmoid(a) * b).astype(BF16))
            hmid = jnp.concatenate(hs, axis=1)
            ys = []
            for oc in range(n_out):
                cols = slice(oc * MOE_OUT_CHUNK, (oc + 1) * MOE_OUT_CHUNK)
                ys.append(jnp.dot(hmid, w2b[slot, :, cols], preferred_element_type=F32))
                issue_some()
            assert not pending
            y = jnp.concatenate(ys, axis=1)
            _store_tile_rows(ybuf, (q % 3,), _pack_bf16_pair(y[:, :half], y[:, half:]))
            return carry

        lax.fori_loop((nb * c) // nch, (nb * (c + 1)) // nch, block, 0)

    @pl.when((g == n_experts) & (c == nch - 1))
    def _drain():
        dst_copy(n_total - 1).wait()
        for r in range(bm):
            scatter_row(n_total - 1, r).start()

        @pl.when(n_total >= 2)
        def _():
            scatter_wait(n_total - 3)

        @pl.when(n_total >= 1)
        def _():
            scatter_wait(n_total - 2)

        scatter_wait(n_total - 1)
        gather_wait(n_total)
        gather_wait(n_total + 1)
        src_copy(n_total + 2).wait()


def _moe_experts(eb_start, eb_count, src_plan, dst_plan, hp, w1, w3, w2, *, n_tokens, bm, nch):
    n_out_rows = 2 * n_tokens + 2 * bm
    E, D, F = w1.shape
    assert D == 2 * SUBLANES * LANES and hp.shape[1] == LANES and D % MOE_OUT_CHUNK == 0
    assert F % nch == 0 and src_plan.shape[1:] == (1, bm) and dst_plan.shape[1:] == (1, bm)
    assert dst_plan.shape[0] == src_plan.shape[0] + 1
    fc = F // nch

    def w_in_idx(g, c, es, ec):
        return (jnp.minimum(g, E - 1), 0, jnp.where(g < E, c, nch - 1))

    def w_out_idx(g, c, es, ec):
        return (jnp.minimum(g, E - 1), jnp.where(g < E, c, nch - 1), 0)

    grid_spec = pltpu.PrefetchScalarGridSpec(
        num_scalar_prefetch=2,
        grid=(E + 1, nch),
        in_specs=[
            pl.BlockSpec(memory_space=pl.ANY),
            pl.BlockSpec(memory_space=pl.ANY),
            pl.BlockSpec(memory_space=pl.ANY),
            pl.BlockSpec((None, D, fc), w_in_idx),
            pl.BlockSpec((None, D, fc), w_in_idx),
            pl.BlockSpec((None, fc, D), w_out_idx),
        ],
        out_specs=pl.BlockSpec(memory_space=pl.ANY),
        scratch_shapes=[
            pltpu.VMEM((2, nch, D, fc), BF16),
            pltpu.VMEM((2, nch, D, fc), BF16),
            pltpu.VMEM((2, F, D), BF16),
            pltpu.VMEM((3, bm * SUBLANES, LANES), U32),
            pltpu.VMEM((3, bm * SUBLANES, LANES), U32),
            pltpu.SMEM((2, 1, bm), I32),
            pltpu.SMEM((2, 1, bm), I32),
            pltpu.SemaphoreType.DMA((3,)),
            pltpu.SemaphoreType.DMA((3,)),
            pltpu.SemaphoreType.DMA((2,)),
            pltpu.SemaphoreType.DMA((2,)),
        ],
    )
    body = functools.partial(_moe_body, n_experts=E, nch=nch, bm=bm, spare_row0=2 * n_tokens)
    return pl.pallas_call(
        body,
        grid_spec=grid_spec,
        out_shape=jax.ShapeDtypeStruct((n_out_rows * SUBLANES, LANES), U32),
        compiler_params=_cparams(("arbitrary", "arbitrary")),
        name="moe_experts",
    )(eb_start, eb_count, src_plan, dst_plan, hp, w1, w3, w2)


def _combine_body(h_ref, rt_ref, g_ref, b_ref, y0_ref, y1_ref, o_ref, *, alpha, eps):
    tt = h_ref.shape[0]
    u0 = _load_tile_rows(y0_ref, (), tt)
    u1 = _load_tile_rows(y1_ref, (), tt)
    half = u0.shape[1]
    c0 = rt_ref[:, 2:3]
    c1 = rt_ref[:, 3:4]
    z_lo = alpha * h_ref[:, :half] + (_unpack_lo(u0) * c0 + _unpack_lo(u1) * c1)
    z_hi = alpha * h_ref[:, half:] + (_unpack_hi(u0) * c0 + _unpack_hi(u1) * c1)
    n = 2 * half
    mu = (jnp.sum(z_lo, axis=1, keepdims=True) + jnp.sum(z_hi, axis=1, keepdims=True)) * (1.0 / n)
    d_lo = z_lo - mu
    d_hi = z_hi - mu
    var = (jnp.sum(d_lo * d_lo, axis=1, keepdims=True)
           + jnp.sum(d_hi * d_hi, axis=1, keepdims=True)) * (1.0 / n)
    inv = lax.rsqrt(var + eps)
    o_ref[:, :half] = d_lo * inv * g_ref[:, :half] + b_ref[:, :half]
    o_ref[:, half:] = d_hi * inv * g_ref[:, half:] + b_ref[:, half:]


def _combine(h, rt, y2, g2, b2, *, alpha, tt):
    T, D = h.shape
    assert T % tt == 0 and D == 2 * SUBLANES * LANES and y2.shape[1] == LANES
    body = functools.partial(_combine_body, alpha=alpha, eps=LN_EPS)
    return pl.pallas_call(
        body,
        grid=(T // tt,),
        in_specs=[
            pl.BlockSpec((tt, D), lambda i: (i, 0)),
            pl.BlockSpec((tt, LANES), lambda i: (i, 0)),
            pl.BlockSpec((1, D), lambda i: (0, 0)),
            pl.BlockSpec((1, D), lambda i: (0, 0)),
            pl.BlockSpec((tt * SUBLANES, LANES), lambda i: (i, 0)),
            pl.BlockSpec((tt * SUBLANES, LANES), lambda i: (i + T // tt, 0)),
        ],
        out_specs=pl.BlockSpec((tt, D), lambda i: (i, 0)),
        out_shape=jax.ShapeDtypeStruct((T, D), F32),
        compiler_params=_cparams(("parallel",)),
        name="moe_combine",
    )(h, rt, g2, b2, y2, y2)


def _slot_plan(expert, n_experts, bm, n_plan_blocks):
    T = expert.shape[0]
    e_flat = expert.reshape(-1)
    onehot = (e_flat[:, None] == jnp.arange(n_experts, dtype=I32)[None, :]).astype(I32)
    csum = jnp.cumsum(onehot, axis=0)
    counts = csum[-1]
    eb_count = (counts + bm - 1) // bm
    eb_start = jnp.cumsum(eb_count) - eb_count
    slot = jnp.sum(onehot * (csum - 1 + (eb_start * bm)[None, :]), axis=1)
    n_slots = n_plan_blocks * bm
    tok = jnp.arange(2 * T, dtype=I32) // 2
    k = jnp.arange(2 * T, dtype=I32) % 2
    p = jnp.arange(n_slots, dtype=I32)
    spare = 2 * T + ((p // bm) % 2) * bm + p % bm
    dst = spare.at[slot].set(k * T + tok, unique_indices=True)
    src = jnp.where(dst < 2 * T, dst % T, 0)
    dst = jnp.concatenate([2 * T + bm + jnp.arange(bm, dtype=I32), dst])
    to_blocks = lambda a: (a * SUBLANES).reshape(-1, 1, bm)
    return eb_start.astype(I32), eb_count.astype(I32), to_blocks(src), to_blocks(dst)


def _layer(x2, p, *, batch, seq, depth, n_heads, head_dim, moba_block, moba_topk,
           n_groups, e_per_group, tiles):
    T, D = x2.shape
    A = n_heads * head_dim
    C = p["w_dw"].shape[1]
    alpha = (2.0 * depth) ** 0.25
    tn = tiles["proj_tn"]
    assert A == C == D, "column-block addressing below assumes equal branch widths"

    w_in, b_in = p["w_in"], p["b_in"]

    half = head_dim // 2
    inv_freq = jnp.power(ROPE_THETA, -jnp.arange(half, dtype=F32) * (2.0 / head_dim))
    ang = jnp.arange(seq, dtype=F32)[:, None] * inv_freq[None, :]
    cos2 = jnp.concatenate([jnp.cos(ang), jnp.cos(ang)], axis=1)
    sin_s = jnp.concatenate([-jnp.sin(ang), jnp.sin(ang)], axis=1)

    u = _in_projection(x2, w_in.astype(BF16), b_in[None, :], cos2, sin_s, seq=seq, attn_w=A, conv_c=C,
                       head_dim=head_dim, tm=tiles["proj_tm"], tn=tn)
    o = _moba_attention(u, batch=batch, seq=seq, n_heads=n_heads, head_dim=head_dim,
                        blk=moba_block, topk=moba_topk, hps=ATTN_HPS)
    hc = _conv_branch(u, p["w_dw"], p["b_dw"], p["conv_ln_g"], p["conv_ln_b"], batch=batch,
                      seq=seq, conv_c=C, col_block=3, ts=tiles["conv_ts"], rt=tiles["conv_rt"],
                      halo=CONV_HALO)

    n_experts = n_groups * e_per_group
    n_route = n_groups + n_experts
    w_route = jnp.concatenate(
        [p["w_rg"], p["w_re"].transpose(1, 0, 2).reshape(D, n_experts),
         jnp.zeros((D, LANES - n_route), F32)], axis=1)
    b_route = jnp.concatenate(
        [p["b_rg"], p["b_re"].reshape(n_experts), jnp.zeros((LANES - n_route,), F32)])[None, :]
    wr_hi = w_route.astype(BF16)
    wr_lo = jnp.concatenate([wr_hi, (w_route - wr_hi.astype(F32)).astype(BF16)], axis=1)

    h, hp, rt = _post_block(
        o, hc, u, x2, p["w_o_attn"].astype(BF16), p["w_pw2"].astype(BF16), p["b_pw2"][None, :],
        p["w_out"].astype(BF16), p["ln1_g"][None, :], p["ln1_b"][None, :], wr_hi, wr_lo, b_route,
        gate_block=4, alpha=alpha, n_groups=n_groups, e_per_group=e_per_group,
        tm=tiles["post_tm"])

    bm = tiles["moe_bm"]
    n_plan_blocks = -(-2 * T // bm) + n_experts + 3
    expert = rt[:, :2].astype(I32)
    eb_start, eb_count, src_plan, dst_plan = _slot_plan(expert, n_experts, bm, n_plan_blocks)
    y2 = _moe_experts(eb_start, eb_count, src_plan, dst_plan, hp, p["w1"], p["w3"], p["w2"],
                      n_tokens=T, bm=bm, nch=tiles["moe_nch"])
    return _combine(h, rt, y2, p["ln2_g"][None, :], p["ln2_b"][None, :], alpha=alpha,
                    tt=tiles["comb_tt"])


_PARAM_NAMES = ("w_in", "b_in", "w_o_attn", "w_dw", "b_dw", "conv_ln_g", "conv_ln_b", "w_pw2",
                "b_pw2", "w_out", "ln1_g", "ln1_b", "w_rg", "b_rg", "w_re", "b_re", "w1", "w3",
                "w2", "ln2_g", "ln2_b")

_TILES = dict(proj_tm=PROJ_TM, proj_tn=PROJ_TN, conv_ts=CONV_TS, conv_rt=CONV_RT,
              post_tm=POST_TM, comb_tt=COMB_TT, moe_bm=MOE_BM, moe_nch=MOE_NCH)


def _forward(x, params, *, n_heads=N_HEADS, head_dim=HEAD_DIM, moba_block=MOBA_BLOCK,
             moba_topk=MOBA_TOPK, n_groups=N_GROUPS, e_per_group=EXPERTS_PER_GROUP, tiles=None):
    tiles = dict(_TILES, **(tiles or {}))
    B, S, D = x.shape
    depth = params["w_in"].shape[0]
    x2 = x.reshape(B * S, D)
    for l in range(depth):
        p = {k: v[l] for k, v in params.items()}
        x2 = _layer(x2, p, batch=B, seq=S, depth=depth, n_heads=n_heads, head_dim=head_dim,
                    moba_block=moba_block, moba_topk=moba_topk, n_groups=n_groups,
                    e_per_group=e_per_group, tiles=tiles)
    return x2.reshape(B, S, D)


def kernel(x, w_in, b_in, w_o_attn, w_dw, b_dw, conv_ln_g, conv_ln_b, w_pw2, b_pw2, w_out, ln1_g,
           ln1_b, w_rg, b_rg, w_re, b_re, w1, w3, w2, ln2_g, ln2_b):
    params = dict(zip(_PARAM_NAMES, (w_in, b_in, w_o_attn, w_dw, b_dw, conv_ln_g, conv_ln_b, w_pw2,
                                     b_pw2, w_out, ln1_g, ln1_b, w_rg, b_rg, w_re, b_re, w1, w3,
                                     w2, ln2_g, ln2_b)))
    return _forward(x, params)
```

```python
import functools

import jax
import jax.numpy as jnp
from jax import lax
from jax.experimental import pallas as pl
from jax.experimental.pallas import tpu as pltpu

F32 = jnp.float32
BF16 = jnp.bfloat16
U32 = jnp.uint32
I32 = jnp.int32

N_HEADS = 16
HEAD_DIM = 128
ROPE_THETA = 10000.0
MOBA_BLOCK = 256
MOBA_TOPK = 3
CONV_WIDTH = 31
N_GROUPS = 4
EXPERTS_PER_GROUP = 8
LN_EPS = 1e-5
LOG2_E = 1.4426950408889634

LANES = 128
SUBLANES = 8
VMEM_LIMIT = 56 * 1024 * 1024

PROJ_TM = 1024
PROJ_TN = 1024
ATTN_HPS = 2
ATTN_LOOKAHEAD = 2
ATTN_ONES_ROWS = 16
CONV_TS = 256
CONV_RT = 16
CONV_HALO = 32
POST_TM = 256
COMB_TT = 512
MOE_BM = 256
MOE_NCH = 4
MOE_OUT_CHUNK = 256


def _cparams(sem):
    return pltpu.CompilerParams(dimension_semantics=sem, vmem_limit_bytes=VMEM_LIMIT)


def _sigmoid(x):
    return 1.0 / (1.0 + jnp.exp(-x))


def _pack_bf16_pair(lo_f32, hi_f32):
    lo = lax.bitcast_convert_type(lo_f32.astype(BF16).astype(F32), U32) >> 16
    hi = lax.bitcast_convert_type(hi_f32.astype(BF16).astype(F32), U32)
    return hi | lo


def _unpack_lo(u):
    return lax.bitcast_convert_type(u << 16, F32)


def _unpack_hi(u):
    return lax.bitcast_convert_type(u & jnp.uint32(0xFFFF0000), F32)


def _store_tile_rows(ref, lead, x):
    n = x.shape[0]
    for s in range(SUBLANES):
        ref[(*lead, pl.ds(s, n, stride=SUBLANES), slice(None))] = x[:, s * LANES:(s + 1) * LANES]


def _load_tile_rows(ref, lead, n):
    return jnp.concatenate(
        [ref[(*lead, pl.ds(s, n, stride=SUBLANES), slice(None))] for s in range(SUBLANES)], axis=1)


def _inproj_body(x_ref, w_ref, wg_ref, b_ref, bg_ref, cos_ref, sin_ref, o_ref, xb_ref,
                 *, n_q, n_qk, n_qkv, n_glu, scale, head_dim):
    j = pl.program_id(1)
    tn = o_ref.shape[1]

    @pl.when(j == 0)
    def _cast():
        xb_ref[...] = x_ref[...].astype(BF16)

    def project(w, b):
        return jnp.dot(xb_ref[...], w[...], preferred_element_type=F32) + b[...]

    @pl.when(j < n_qk)
    def _rope():
        acc = project(w_ref, b_ref)
        s = jnp.where(j < n_q, scale, 1.0).astype(F32)
        cos = cos_ref[...] * s
        sin = sin_ref[...] * s
        for h in range(tn // head_dim):
            t = acc[:, h * head_dim:(h + 1) * head_dim]
            r = pltpu.roll(t, head_dim // 2, axis=1)
            o_ref[:, h * head_dim:(h + 1) * head_dim] = (t * cos + r * sin).astype(o_ref.dtype)

    @pl.when((j >= n_qk) & (j < n_qkv))
    def _plain():
        o_ref[...] = project(w_ref, b_ref).astype(o_ref.dtype)

    @pl.when((j >= n_qkv) & (j < n_qkv + n_glu))
    def _glu():
        a = project(w_ref, b_ref)
        g = project(wg_ref, bg_ref)
        o_ref[...] = (a * _sigmoid(g)).astype(o_ref.dtype)

    @pl.when(j >= n_qkv + n_glu)
    def _gate():
        o_ref[...] = _sigmoid(project(w_ref, b_ref)).astype(o_ref.dtype)


def _in_projection(x2, w_all, b_all, cos2, sin_s, *, seq, attn_w, conv_c, head_dim, tm, tn):
    T, D = x2.shape
    W = w_all.shape[1] - conv_c
    assert T % tm == 0 and seq % tm == 0 and W % tn == 0
    assert attn_w % tn == 0 and conv_c % tn == 0 and tn % head_dim == 0
    assert head_dim == LANES
    n_q = attn_w // tn
    n_glu = conv_c // tn
    n_qkv = 3 * n_q
    pos_tiles = seq // tm

    def main_idx(i, j):
        return (0, jnp.where(j < n_qkv + n_glu, j, j + n_glu))

    def g_idx(i, j):
        return (0, n_qkv + n_glu + jnp.clip(j - n_qkv, 0, n_glu - 1))

    body = functools.partial(_inproj_body, n_q=n_q, n_qk=2 * n_q, n_qkv=n_qkv, n_glu=n_glu,
                             scale=float(head_dim) ** -0.5 * LOG2_E, head_dim=head_dim)
    return pl.pallas_call(
        body,
        grid=(T // tm, W // tn),
        in_specs=[
            pl.BlockSpec((tm, D), lambda i, j: (i, 0)),
            pl.BlockSpec((D, tn), main_idx),
            pl.BlockSpec((D, tn), g_idx),
            pl.BlockSpec((1, tn), main_idx),
            pl.BlockSpec((1, tn), g_idx),
            pl.BlockSpec((tm, head_dim), lambda i, j: (i % pos_tiles, 0)),
            pl.BlockSpec((tm, head_dim), lambda i, j: (i % pos_tiles, 0)),
        ],
        out_specs=pl.BlockSpec((tm, tn), lambda i, j: (i, j)),
        out_shape=jax.ShapeDtypeStruct((T, W), BF16),
        scratch_shapes=[pltpu.VMEM((tm, D), BF16)],
        compiler_params=_cparams(("parallel", "arbitrary")),
        name="in_projection",
    )(x2, w_all, w_all, b_all, b_all, cos2, sin_s)


def _attn_prep(q, k, v, *, seq, blk):
    nb = seq // blk
    nbp = -(-nb // SUBLANES) * SUBLANES
    dh = q.shape[1]
    contract_last = (((1,), (1,)), ((), ()))
    kmean = jnp.mean(k.astype(F32).reshape(nb, blk, dh), axis=1)
    if nbp > nb:
        kmean = jnp.concatenate([kmean, jnp.zeros((nbp - nb, dh), F32)], axis=0)
    gate_t = lax.dot_general(kmean.astype(BF16), q, contract_last, preferred_element_type=F32)
    ones_rows = (lax.broadcasted_iota(I32, (ATTN_ONES_ROWS, seq), 0) == 0).astype(BF16)
    vt = jnp.concatenate([v.astype(F32).T.astype(BF16), ones_rows], axis=0)
    return gate_t, vt


def _attn_scores(q, k, gate_t, i, s_ref, *, blk, topk, nbp):
    contract_last = (((1,), (1,)), ((), ()))
    neg_inf = jnp.float32(-jnp.inf)
    sub = lax.broadcasted_iota(I32, (nbp, blk), 0)
    qi = q[i * blk:(i + 1) * blk]
    nk = (i + 1) * blk
    st = lax.dot_general(k[:nk], qi, contract_last, preferred_element_type=F32)
    if i > topk:
        gm = jnp.where(sub < i, gate_t[:, i * blk:(i + 1) * blk], neg_inf)
        rank = jnp.zeros((nbp, blk), I32)
        for other in range(i):
            g_other = gm[other:other + 1, :]
            beats = (g_other > gm) | ((g_other == gm) & (sub > other))
            rank = rank + beats.astype(I32)
        bias_t = jnp.where((sub < i) & (rank < topk), 0.0, neg_inf).astype(F32)
    m = None
    for n in range(i + 1):
        t = st[n * blk:(n + 1) * blk]
        if n == i:
            key_r = lax.broadcasted_iota(I32, (blk, blk), 0)
            qry_c = lax.broadcasted_iota(I32, (blk, blk), 1)
            t = jnp.where(key_r <= qry_c, t, neg_inf)
        elif i > topk:
            t = t + bias_t[n:n + 1, :]
        s_ref[n * blk:(n + 1) * blk, :] = t
        t_max = jnp.max(t, axis=0, keepdims=True)
        m = t_max if m is None else jnp.maximum(m, t_max)
    return m


def _attn_output(vt, m, i, s_ref, *, blk, dh):
    nk = (i + 1) * blk
    p = jnp.exp2(s_ref[:nk, :] - m).astype(BF16)
    ot = jnp.dot(vt[:, :nk], p, preferred_element_type=F32)
    return (ot[:dh] * (1.0 / ot[dh:dh + 1])).T


def _attn_body(q_ref, k_ref, v_ref, o_ref, s_scr, *, seq, blk, topk, dh):
    nb = seq // blk
    nbp = -(-nb // SUBLANES) * SUBLANES
    heads = []
    for hh in range(q_ref.shape[1] // dh):
        c0 = hh * dh
        q, k, v = q_ref[:, c0:c0 + dh], k_ref[:, c0:c0 + dh], v_ref[:, c0:c0 + dh]
        heads.append((c0, q, k) + _attn_prep(q, k, v, seq=seq, blk=blk))
    units = [(h, i) for h in range(len(heads)) for i in range(nb)]

    n_buf = s_scr.shape[0]
    ahead = n_buf - 1

    def scores(u):
        h, i = units[u]
        _, q, k, gate_t, _ = heads[h]
        return _attn_scores(q, k, gate_t, i, s_scr.at[u % n_buf], blk=blk, topk=topk, nbp=nbp)

    col_max = {u: scores(u) for u in range(min(ahead, len(units)))}
    for u, (h, i) in enumerate(units):
        if u + ahead < len(units):
            col_max[u + ahead] = scores(u + ahead)
        c0, _, _, _, vt = heads[h]
        o = _attn_output(vt, col_max.pop(u), i, s_scr.at[u % n_buf], blk=blk, dh=dh)
        o_ref[i * blk:(i + 1) * blk, c0:c0 + dh] = o.astype(o_ref.dtype)


def _moba_attention(u, *, batch, seq, n_heads, head_dim, blk, topk, hps):
    T = u.shape[0]
    assert seq % blk == 0 and blk % LANES == 0 and n_heads % hps == 0
    n_hb = n_heads // hps
    body = functools.partial(_attn_body, seq=seq, blk=blk, topk=topk, dh=head_dim)
    return pl.pallas_call(
        body,
        grid=(batch, n_hb),
        in_specs=[
            pl.BlockSpec((seq, hps * head_dim), lambda b, h: (b, h)),
            pl.BlockSpec((seq, hps * head_dim), lambda b, h: (b, n_hb + h)),
            pl.BlockSpec((seq, hps * head_dim), lambda b, h: (b, 2 * n_hb + h)),
        ],
        out_specs=pl.BlockSpec((seq, hps * head_dim), lambda b, h: (b, h)),
        out_shape=jax.ShapeDtypeStruct((T, n_heads * head_dim), BF16),
        scratch_shapes=[pltpu.VMEM((ATTN_LOOKAHEAD + 1, seq, blk), F32)],
        compiler_params=_cparams(("parallel", "parallel")),
        name="moba_attention",
    )(u, u, u)


def _conv_body(x_ref, w_ref, bdw_ref, g_ref, b_ref, o_ref, win_ref, conv_ref,
               *, width, ts, rt, halo, eps):
    ng = win_ref.shape[0]
    sub = SUBLANES

    @pl.when(pl.program_id(1) == 0)
    def _zero_halo():
        win_ref[:, 0:halo * sub, :] = jnp.zeros((ng, halo * sub, LANES), F32)

    for g in range(ng):
        for s in range(sub):
            c0 = (g * sub + s) * LANES
            win_ref[g, pl.ds(halo * sub + s, ts, stride=sub), :] = x_ref[:, c0:c0 + LANES].astype(F32)

    first = halo - (width - 1)
    for g in range(ng):
        def chunk(r, carry, g=g):
            t0 = r * rt
            acc = jnp.broadcast_to(bdw_ref[g][None], (rt, sub, LANES))
            for j in range(width):
                start = pl.multiple_of((t0 + first + j) * sub, sub)
                slab = win_ref[g, pl.ds(start, rt * sub), :].reshape(rt, sub, LANES)
                acc = acc + slab * w_ref[g, j][None]
            conv_ref[g, pl.ds(pl.multiple_of(t0 * sub, sub), rt * sub), :] = acc.reshape(rt * sub, LANES)
            return carry

        lax.fori_loop(0, ts // rt, chunk, 0)

    win_ref[:, 0:halo * sub, :] = win_ref[:, ts * sub:(ts + halo) * sub, :]

    def channel_chunks():
        for g in range(ng):
            for s in range(sub):
                yield (g * sub + s) * LANES, conv_ref[g, pl.ds(s, ts, stride=sub), :]

    n = ng * sub * LANES
    total = jnp.zeros((ts, 1), F32)
    for _, y in channel_chunks():
        total = total + jnp.sum(y, axis=1, keepdims=True)
    mu = total * (1.0 / n)
    sq = jnp.zeros((ts, 1), F32)
    for _, y in channel_chunks():
        d = y - mu
        sq = sq + jnp.sum(d * d, axis=1, keepdims=True)
    inv = lax.rsqrt(sq * (1.0 / n) + eps)
    for c0, y in channel_chunks():
        z = (y - mu) * inv * g_ref[:, c0:c0 + LANES] + b_ref[:, c0:c0 + LANES]
        o_ref[:, c0:c0 + LANES] = (z * _sigmoid(z)).astype(o_ref.dtype)


def _conv_branch(u, w_dw, b_dw, ln_g, ln_b, *, batch, seq, conv_c, col_block, ts, rt, halo):
    T = u.shape[0]
    width = w_dw.shape[0]
    gw = SUBLANES * LANES
    assert seq % ts == 0 and conv_c % gw == 0 and ts % rt == 0
    assert halo >= width - 1 and ts >= halo
    ng = conv_c // gw
    w_t = w_dw.reshape(width, ng, SUBLANES, LANES).transpose(1, 0, 2, 3)
    n_s = seq // ts
    body = functools.partial(_conv_body, width=width, ts=ts, rt=rt, halo=halo, eps=LN_EPS)
    return pl.pallas_call(
        body,
        grid=(batch, n_s),
        in_specs=[
            pl.BlockSpec((ts, conv_c), lambda b, s: (b * n_s + s, col_block)),
            pl.BlockSpec((ng, width, SUBLANES, LANES), lambda b, s: (0, 0, 0, 0)),
            pl.BlockSpec((ng, SUBLANES, LANES), lambda b, s: (0, 0, 0)),
            pl.BlockSpec((1, conv_c), lambda b, s: (0, 0)),
            pl.BlockSpec((1, conv_c), lambda b, s: (0, 0)),
        ],
        out_specs=pl.BlockSpec((ts, conv_c), lambda b, s: (b * n_s + s, 0)),
        out_shape=jax.ShapeDtypeStruct((T, conv_c), BF16),
        scratch_shapes=[pltpu.VMEM((ng, (halo + ts) * SUBLANES, LANES), F32),
                        pltpu.VMEM((ng, ts * SUBLANES, LANES), F32)],
        compiler_params=_cparams(("parallel", "arbitrary")),
        name="conv_branch",
    )(u, w_t, b_dw.reshape(ng, SUBLANES, LANES), ln_g[None, :], ln_b[None, :])


def _post_body(o_ref, hc_ref, ga_ref, gb_ref, x_ref, wo_ref, wp_ref, bp_ref, wout_ref,
               g1_ref, b1_ref, wrh_ref, wrl_ref, br_ref, h_ref, hp_ref, rt_ref, m_scr, z_scr,
               *, alpha, eps, n_groups, e_per_group):
    i = pl.program_id(0)

    @pl.when(i == 0)
    def _no_previous_tiles():
        m_scr[1] = jnp.zeros(m_scr.shape[1:], BF16)
        z_scr[1] = jnp.zeros(z_scr.shape[1:], F32)
        z_scr[2] = jnp.zeros(z_scr.shape[1:], F32)

    ya = jnp.dot(o_ref[...], wo_ref[...], preferred_element_type=F32)

    z = z_scr[(i + 1) % 3]
    mu = jnp.mean(z, axis=1, keepdims=True)
    zc = z - mu
    var = jnp.mean(zc * zc, axis=1, keepdims=True)
    h = zc * lax.rsqrt(var + eps) * g1_ref[...] + b1_ref[...]
    h_ref[...] = h
    half = h.shape[1] // 2
    _store_tile_rows(hp_ref, (), _pack_bf16_pair(h[:, :half], h[:, half:]))

    yc = jnp.dot(hc_ref[...], wp_ref[...], preferred_element_type=F32) + bp_ref[...]

    h_hi = h.astype(BF16)
    h_lo = (h - h_hi.astype(F32)).astype(BF16)
    hi_terms = jnp.dot(h_hi, wrl_ref[...], preferred_element_type=F32)
    logits = (hi_terms[:, :LANES] + hi_terms[:, LANES:]
              + jnp.dot(h_lo, wrh_ref[...], preferred_element_type=F32) + br_ref[...])
    tm = logits.shape[0]
    lane = lax.broadcasted_iota(I32, (tm, LANES), 1)
    neg_inf = jnp.float32(-jnp.inf)
    big = jnp.int32(LANES)

    def first_argmax(vals):
        top = jnp.max(vals, axis=1, keepdims=True)
        idx = jnp.min(jnp.where(vals == top, lane, big), axis=1, keepdims=True)
        return top, idx

    gl = jnp.where(lane < n_groups, logits, neg_inf)
    gmax, grp = first_argmax(gl)
    grp_w = 1.0 / jnp.sum(jnp.exp(gl - gmax), axis=1, keepdims=True)
    lo_lane = n_groups + grp * e_per_group
    el = jnp.where((lane >= lo_lane) & (lane < lo_lane + e_per_group), logits, neg_inf)
    v1, i1 = first_argmax(el)
    v2, i2 = first_argmax(jnp.where(lane == i1, neg_inf, el))
    t = jnp.exp(v2 - v1)
    p1 = 1.0 / (1.0 + t)
    c1 = p1 * grp_w
    c2 = (t * p1) * grp_w
    e1 = (i1 - n_groups).astype(F32)
    e2 = (i2 - n_groups).astype(F32)
    rt_ref[...] = jnp.where(lane == 0, e1, jnp.where(lane == 1, e2,
                            jnp.where(lane == 2, c1, jnp.where(lane == 3, c2, 0.0))))

    prev = (i + 1) % 2
    z_scr[(i + 2) % 3] += jnp.dot(m_scr[prev], wout_ref[...], preferred_element_type=F32)
    m_scr[i % 2] = (ga_ref[...].astype(F32) * ya + gb_ref[...].astype(F32) * yc).astype(BF16)
    z_scr[i % 3] = alpha * x_ref[...]


def _post_block(o, hc, u, x2, wo, wp, bp, wout, g1, b1, wr_hi, wr_lo, br, *, gate_block,
                alpha, n_groups, e_per_group, tm):
    T, D = x2.shape
    A = o.shape[1]
    C = hc.shape[1]
    assert T % tm == 0 and D == 2 * SUBLANES * LANES, "packed rows are stored as one (8, 128) tile"
    assert n_groups * (1 + e_per_group) <= LANES
    const = lambda i: (0, 0)
    resident = lambda shape: pl.BlockSpec(shape, const, pipeline_mode=pl.Buffered(1))
    body = functools.partial(_post_body, alpha=alpha, eps=LN_EPS, n_groups=n_groups,
                             e_per_group=e_per_group)
    n = T // tm
    cur = lambda i: jnp.minimum(i, n - 1)
    prev = lambda i: jnp.maximum(i - 2, 0)
    return pl.pallas_call(
        body,
        grid=(n + 2,),
        in_specs=[
            pl.BlockSpec((tm, A), lambda i: (cur(i), 0)),
            pl.BlockSpec((tm, C), lambda i: (cur(i), 0)),
            pl.BlockSpec((tm, D), lambda i: (cur(i), gate_block)),
            pl.BlockSpec((tm, D), lambda i: (cur(i), gate_block + 1)),
            pl.BlockSpec((tm, D), lambda i: (cur(i), 0)),
            resident((A, D)),
            resident((C, D)),
            resident((1, D)),
            resident((D, D)),
            resident((1, D)),
            resident((1, D)),
            resident((D, LANES)),
            resident((D, 2 * LANES)),
            resident((1, LANES)),
        ],
        out_specs=[
            pl.BlockSpec((tm, D), lambda i: (prev(i), 0)),
            pl.BlockSpec((tm * SUBLANES, LANES), lambda i: (prev(i), 0)),
            pl.BlockSpec((tm, LANES), lambda i: (prev(i), 0)),
        ],
        out_shape=[
            jax.ShapeDtypeStruct((T, D), F32),
            jax.ShapeDtypeStruct((T * SUBLANES, LANES), U32),
            jax.ShapeDtypeStruct((T, LANES), F32),
        ],
        scratch_shapes=[pltpu.VMEM((2, tm, D), BF16), pltpu.VMEM((3, tm, D), F32)],
        compiler_params=_cparams(("arbitrary",)),
        name="merge_project_route",
    )(o, hc, u, u, x2, wo, wp, bp, wout, g1, b1, wr_hi, wr_lo, br)


def _row_copy(src_ref, src_row8, dst_ref, dst_row8, sem):
    aligned = lambda v: v if isinstance(v, int) else pl.multiple_of(v, SUBLANES)
    return pltpu.make_async_copy(src_ref.at[pl.ds(aligned(src_row8), SUBLANES)],
                                 dst_ref.at[pl.ds(aligned(dst_row8), SUBLANES)], sem)


def _moe_body(es_ref, ec_ref, src_ref, dst_ref, hp_ref, w1_ref, w3_ref, w2_ref, y2_ref,
              w1b, w3b, w2b, xbuf, ybuf, src, dst, gsem, ssem, src_sem, dst_sem,
              *, n_experts, nch, bm, spare_row0):
    g = pl.program_id(0)
    c = pl.program_id(1)
    fc = w1_ref.shape[1]
    half = SUBLANES * LANES
    n_total = es_ref[n_experts - 1] + ec_ref[n_experts - 1]

    def src_copy(q):
        return pltpu.make_async_copy(src_ref.at[q], src.at[q % 2], src_sem.at[q % 2])

    def dst_copy(q):
        return pltpu.make_async_copy(dst_ref.at[q + 1], dst.at[(q + 1) % 2], dst_sem.at[(q + 1) % 2])

    def gather_row(q, r):
        return _row_copy(hp_ref, src[q % 2, 0, r], xbuf.at[q % 3], r * SUBLANES, gsem.at[q % 3])

    def scatter_row(q, r):
        return _row_copy(ybuf.at[(q + 3) % 3], r * SUBLANES, y2_ref, dst[(q + 1) % 2, 0, r],
                         ssem.at[(q + 3) % 3])

    def gather_wait(q):
        for r in range(bm):
            _row_copy(hp_ref, 0, xbuf.at[q % 3], r * SUBLANES, gsem.at[q % 3]).wait()

    def scatter_wait(q):
        for r in range(bm):
            _row_copy(ybuf.at[(q + 3) % 3], r * SUBLANES, y2_ref, 0, ssem.at[(q + 3) % 3]).wait()

    @pl.when((g == 0) & (c == 0))
    def _prime():
        ybuf[...] = jnp.zeros(ybuf.shape, U32)
        spare = [pltpu.make_async_copy(
            ybuf.at[s], y2_ref.at[pl.ds((spare_row0 + s * bm) * SUBLANES, bm * SUBLANES)], ssem.at[s])
            for s in range(2)]
        for cp in spare:
            cp.start()
        for cp in spare:
            cp.wait()
        dst_copy(-1).start()
        for q in range(2):
            src_copy(q).start()
        for q in range(2):
            src_copy(q).wait()
            for r in range(bm):
                gather_row(q, r).start()
        src_copy(2).start()

    @pl.when(g < n_experts)
    def _cast_next_expert_chunk():
        slot = g % 2
        w1b[slot, c] = w1_ref[...].astype(BF16)
        w3b[slot, c] = w3_ref[...].astype(BF16)
        w2b[slot, pl.ds(pl.multiple_of(c * fc, fc), fc), :] = w2_ref[...].astype(BF16)

    @pl.when(g >= 1)
    def _compute_previous_expert_share():
        e = g - 1
        slot = e % 2
        nb = ec_ref[e]
        base = es_ref[e]
        n_out = w2b.shape[2] // MOE_OUT_CHUNK

        def block(r, carry):
            q = base + r
            src_copy(q + 2).wait()
            src_copy(q + 3).start()
            dst_copy(q - 1).wait()
            dst_copy(q).start()
            gather_wait(q)

            @pl.when(q >= 2)
            def _():
                scatter_wait(q - 3)

            pending = []
            for rr in range(bm):
                pending.append((scatter_row(q - 1, rr), rr % 2))
                pending.append((gather_row(q + 2, rr), rr % 2))
            per_gap = -(-len(pending) // (4 * nch + n_out))

            def issue_some():
                for _ in range(min(per_gap, len(pending))):
                    copy, queue = pending.pop(0)
                    copy.start(priority=queue)

            u = _load_tile_rows(xbuf, (q % 3,), bm)
            x_lo = _unpack_lo(u).astype(BF16)
            x_hi = _unpack_hi(u).astype(BF16)
            hs = []
            for cc in range(nch):
                a = jnp.dot(x_lo, w1b[slot, cc, :half, :], preferred_element_type=F32)
                issue_some()
                a = a + jnp.dot(x_hi, w1b[slot, cc, half:, :], preferred_element_type=F32)
                issue_some()
                b = jnp.dot(x_lo, w3b[slot, cc, :half, :], preferred_element_type=F32)
                issue_some()
                b = b + jnp.dot(x_hi, w3b[slot, cc, half:, :], preferred_element_type=F32)
                issue_some()
                hs.append((a * _sigmoid(a) * b).astype(BF16))
            hmid = jnp.concatenate(hs, axis=1)
            ys = []
            for oc in range(n_out):
                cols = slice(oc * MOE_OUT_CHUNK, (oc + 1) * MOE_OUT_CHUNK)
                ys.append(jnp.dot(hmid, w2b[slot, :, cols], preferred_element_type=F32))
                issue_some()
            assert not pending
            y = jnp.concatenate(ys, axis=1)
            _store_tile_rows(ybuf, (q % 3,), _pack_bf16_pair(y[:, :half], y[:, half:]))
            return carry

        lax.fori_loop((nb * c) // nch, (nb * (c + 1)) // nch, block, 0)

    @pl.when((g == n_experts) & (c == nch - 1))
    def _drain():
        dst_copy(n_total - 1).wait()
        for r in range(bm):
            scatter_row(n_total - 1, r).start()

        @pl.when(n_total >= 2)
        def _():
            scatter_wait(n_total - 3)

        @pl.when(n_total >= 1)
        def _():
            scatter_wait(n_total - 2)

        scatter_wait(n_total - 1)
        gather_wait(n_total)
        gather_wait(n_total + 1)
        src_copy(n_total + 2).wait()


def _moe_experts(eb_start, eb_count, src_plan, dst_plan, hp, w1, w3, w2, *, n_tokens, bm, nch):
    n_out_rows = 2 * n_tokens + 2 * bm
    E, D, F = w1.shape
    assert D == 2 * SUBLANES * LANES and hp.shape[1] == LANES and D % MOE_OUT_CHUNK == 0
    assert F % nch == 0 and src_plan.shape[1:] == (1, bm) and dst_plan.shape[1:] == (1, bm)
    assert dst_plan.shape[0] == src_plan.shape[0] + 1
    fc = F // nch

    def w_in_idx(g, c, es, ec):
        return (jnp.minimum(g, E - 1), 0, jnp.where(g < E, c, nch - 1))

    def w_out_idx(g, c, es, ec):
        return (jnp.minimum(g, E - 1), jnp.where(g < E, c, nch - 1), 0)

    grid_spec = pltpu.PrefetchScalarGridSpec(
        num_scalar_prefetch=2,
        grid=(E + 1, nch),
        in_specs=[
            pl.BlockSpec(memory_space=pl.ANY),
            pl.BlockSpec(memory_space=pl.ANY),
            pl.BlockSpec(memory_space=pl.ANY),
            pl.BlockSpec((None, D, fc), w_in_idx),
            pl.BlockSpec((None, D, fc), w_in_idx),
            pl.BlockSpec((None, fc, D), w_out_idx),
        ],
        out_specs=pl.BlockSpec(memory_space=pl.ANY),
        scratch_shapes=[
            pltpu.VMEM((2, nch, D, fc), BF16),
            pltpu.VMEM((2, nch, D, fc), BF16),
            pltpu.VMEM((2, F, D), BF16),
            pltpu.VMEM((3, bm * SUBLANES, LANES), U32),
            pltpu.VMEM((3, bm * SUBLANES, LANES), U32),
            pltpu.SMEM((2, 1, bm), I32),
            pltpu.SMEM((2, 1, bm), I32),
            pltpu.SemaphoreType.DMA((3,)),
            pltpu.SemaphoreType.DMA((3,)),
            pltpu.SemaphoreType.DMA((2,)),
            pltpu.SemaphoreType.DMA((2,)),
        ],
    )
    body = functools.partial(_moe_body, n_experts=E, nch=nch, bm=bm, spare_row0=2 * n_tokens)
    return pl.pallas_call(
        body,
        grid_spec=grid_spec,
        out_shape=jax.ShapeDtypeStruct((n_out_rows * SUBLANES, LANES), U32),
        compiler_params=_cparams(("arbitrary", "arbitrary")),
        name="moe_experts",
    )(eb_start, eb_count, src_plan, dst_plan, hp, w1, w3, w2)


def _combine_body(h_ref, rt_ref, g_ref, b_ref, y0_ref, y1_ref, o_ref, *, alpha, eps):
    tt = h_ref.shape[0]
    u0 = _load_tile_rows(y0_ref, (), tt)
    u1 = _load_tile_rows(y1_ref, (), tt)
    half = u0.shape[1]
    c0 = rt_ref[:, 2:3]
    c1 = rt_ref[:, 3:4]
    z_lo = alpha * h_ref[:, :half] + (_unpack_lo(u0) * c0 + _unpack_lo(u1) * c1)
    z_hi = alpha * h_ref[:, half:] + (_unpack_hi(u0) * c0 + _unpack_hi(u1) * c1)
    n = 2 * half
    mu = (jnp.sum(z_lo, axis=1, keepdims=True) + jnp.sum(z_hi, axis=1, keepdims=True)) * (1.0 / n)
    d_lo = z_lo - mu
    d_hi = z_hi - mu
    var = (jnp.sum(d_lo * d_lo, axis=1, keepdims=True)
           + jnp.sum(d_hi * d_hi, axis=1, keepdims=True)) * (1.0 / n)
    inv = lax.rsqrt(var + eps)
    o_ref[:, :half] = d_lo * inv * g_ref[:, :half] + b_ref[:, :half]
    o_ref[:, half:] = d_hi * inv * g_ref[:, half:] + b_ref[:, half:]


def _combine(h, rt, y2, g2, b2, *, alpha, tt):
    T, D = h.shape
    assert T % tt == 0 and D == 2 * SUBLANES * LANES and y2.shape[1] == LANES
    body = functools.partial(_combine_body, alpha=alpha, eps=LN_EPS)
    return pl.pallas_call(
        body,
        grid=(T // tt,),
        in_specs=[
            pl.BlockSpec((tt, D), lambda i: (i, 0)),
            pl.BlockSpec((tt, LANES), lambda i: (i, 0)),
            pl.BlockSpec((1, D), lambda i: (0, 0)),
            pl.BlockSpec((1, D), lambda i: (0, 0)),
            pl.BlockSpec((tt * SUBLANES, LANES), lambda i: (i, 0)),
            pl.BlockSpec((tt * SUBLANES, LANES), lambda i: (i + T // tt, 0)),
        ],
        out_specs=pl.BlockSpec((tt, D), lambda i: (i, 0)),
        out_shape=jax.ShapeDtypeStruct((T, D), F32),
        compiler_params=_cparams(("parallel",)),
        name="moe_combine",
    )(h, rt, g2, b2, y2, y2)


def _slot_plan(expert, n_experts, bm, n_plan_blocks):
    T = expert.shape[0]
    e_flat = expert.reshape(-1)
    onehot = (e_flat[:, None] == jnp.arange(n_experts, dtype=I32)[None, :]).astype(I32)
    csum = jnp.cumsum(onehot, axis=0)
    counts = csum[-1]
    eb_count = (counts + bm - 1) // bm
    eb_start = jnp.cumsum(eb_count) - eb_count
    slot = jnp.sum(onehot * (csum - 1 + (eb_start * bm)[None, :]), axis=1)
    n_slots = n_plan_blocks * bm
    tok = jnp.arange(2 * T, dtype=I32) // 2
    k = jnp.arange(2 * T, dtype=I32) % 2
    p = jnp.arange(n_slots, dtype=I32)
    spare = 2 * T + ((p // bm) % 2) * bm + p % bm
    dst = spare.at[slot].set(k * T + tok, unique_indices=True)
    src = jnp.where(dst < 2 * T, dst % T, 0)
    dst = jnp.concatenate([2 * T + bm + jnp.arange(bm, dtype=I32), dst])
    to_blocks = lambda a: (a * SUBLANES).reshape(-1, 1, bm)
    return eb_start.astype(I32), eb_count.astype(I32), to_blocks(src), to_blocks(dst)


def _layer(x2, p, *, batch, seq, depth, n_heads, head_dim, moba_block, moba_topk,
           n_groups, e_per_group, tiles):
    T, D = x2.shape
    A = n_heads * head_dim
    C = p["w_dw"].shape[1]
    alpha = (2.0 * depth) ** 0.25
    tn = tiles["proj_tn"]
    assert A == C == D, "column-block addressing below assumes equal branch widths"

    w_in, b_in = p["w_in"], p["b_in"]

    half = head_dim // 2
    inv_freq = jnp.power(ROPE_THETA, -jnp.arange(half, dtype=F32) * (2.0 / head_dim))
    ang = jnp.arange(seq, dtype=F32)[:, None] * inv_freq[None, :]
    cos2 = jnp.concatenate([jnp.cos(ang), jnp.cos(ang)], axis=1)
    sin_s = jnp.concatenate([-jnp.sin(ang), jnp.sin(ang)], axis=1)

    u = _in_projection(x2, w_in.astype(BF16), b_in[None, :], cos2, sin_s, seq=seq, attn_w=A, conv_c=C,
                       head_dim=head_dim, tm=tiles["proj_tm"], tn=tn)
    o = _moba_attention(u, batch=batch, seq=seq, n_heads=n_heads, head_dim=head_dim,
                        blk=moba_block, topk=moba_topk, hps=ATTN_HPS)
    hc = _conv_branch(u, p["w_dw"], p["b_dw"], p["conv_ln_g"], p["conv_ln_b"], batch=batch,
                      seq=seq, conv_c=C, col_block=3, ts=tiles["conv_ts"], rt=tiles["conv_rt"],
                      halo=CONV_HALO)

    n_experts = n_groups * e_per_group
    n_route = n_groups + n_experts
    w_route = jnp.concatenate(
        [p["w_rg"], p["w_re"].transpose(1, 0, 2).reshape(D, n_experts),
         jnp.zeros((D, LANES - n_route), F32)], axis=1)
    b_route = jnp.concatenate(
        [p["b_rg"], p["b_re"].reshape(n_experts), jnp.zeros((LANES - n_route,), F32)])[None, :]
    wr_hi = w_route.astype(BF16)
    wr_lo = jnp.concatenate([wr_hi, (w_route - wr_hi.astype(F32)).astype(BF16)], axis=1)

    h, hp, rt = _post_block(
        o, hc, u, x2, p["w_o_attn"].astype(BF16), p["w_pw2"].astype(BF16), p["b_pw2"][None, :],
        p["w_out"].astype(BF16), p["ln1_g"][None, :], p["ln1_b"][None, :], wr_hi, wr_lo, b_route,
        gate_block=4, alpha=alpha, n_groups=n_groups, e_per_group=e_per_group,
        tm=tiles["post_tm"])

    bm = tiles["moe_bm"]
    n_plan_blocks = -(-2 * T // bm) + n_experts + 3
    expert = rt[:, :2].astype(I32)
    eb_start, eb_count, src_plan, dst_plan = _slot_plan(expert, n_experts, bm, n_plan_blocks)
    y2 = _moe_experts(eb_start, eb_count, src_plan, dst_plan, hp, p["w1"], p["w3"], p["w2"],
                      n_tokens=T, bm=bm, nch=tiles["moe_nch"])
    return _combine(h, rt, y2, p["ln2_g"][None, :], p["ln2_b"][None, :], alpha=alpha,
                    tt=tiles["comb_tt"])


_PARAM_NAMES = ("w_in", "b_in", "w_o_attn", "w_dw", "b_dw", "conv_ln_g", "conv_ln_b", "w_pw2",
                "b_pw2", "w_out", "ln1_g", "ln1_b", "w_rg", "b_rg", "w_re", "b_re", "w1", "w3",
                "w2", "ln2_g", "ln2_b")

_TILES = dict(proj_tm=PROJ_TM, proj_tn=PROJ_TN, conv_ts=CONV_TS, conv_rt=CONV_RT,
              post_tm=POST_TM, comb_tt=COMB_TT, moe_bm=MOE_BM, moe_nch=MOE_NCH)


def _forward(x, params, *, n_heads=N_HEADS, head_dim=HEAD_DIM, moba_block=MOBA_BLOCK,
             moba_topk=MOBA_TOPK, n_groups=N_GROUPS, e_per_group=EXPERTS_PER_GROUP, tiles=None):
    tiles = dict(_TILES, **(tiles or {}))
    B, S, D = x.shape
    depth = params["w_in"].shape[0]
    x2 = x.reshape(B * S, D)
    for l in range(depth):
        p = {k: v[l] for k, v in params.items()}
        x2 = _layer(x2, p, batch=B, seq=S, depth=depth, n_heads=n_heads, head_dim=head_dim,
                    moba_block=moba_block, moba_topk=moba_topk, n_groups=n_groups,
                    e_per_group=e_per_group, tiles=tiles)
    return x2.reshape(B, S, D)


def kernel(x, w_in, b_in, w_o_attn, w_dw, b_dw, conv_ln_g, conv_ln_b, w_pw2, b_pw2, w_out, ln1_g,
           ln1_b, w_rg, b_rg, w_re, b_re, w1, w3, w2, ln2_g, ln2_b):
    params = dict(zip(_PARAM_NAMES, (w_in, b_in, w_o_attn, w_dw, b_dw, conv_ln_g, conv_ln_b, w_pw2,
                                     b_pw2, w_out, ln1_g, ln1_b, w_rg, b_rg, w_re, b_re, w1, w3,
                                     w2, ln2_g, ln2_b)))
    return _forward(x, params)
```

```python
import functools

import jax
import jax.numpy as jnp
from jax import lax
from jax.experimental import pallas as pl
from jax.experimental.pallas import tpu as pltpu

F32 = jnp.float32
BF16 = jnp.bfloat16
U32 = jnp.uint32
I32 = jnp.int32

N_HEADS = 16
HEAD_DIM = 128
ROPE_THETA = 10000.0
MOBA_BLOCK = 256
MOBA_TOPK = 3
CONV_WIDTH = 31
N_GROUPS = 4
EXPERTS_PER_GROUP = 8
LN_EPS = 1e-5
LOG2_E = 1.4426950408889634

LANES = 128
SUBLANES = 8
VMEM_LIMIT = 56 * 1024 * 1024

PROJ_TM = 1024
PROJ_TN = 1024
ATTN_HPS = 2
ATTN_LOOKAHEAD = 2
ATTN_ONES_ROWS = 16
CONV_TS = 256
CONV_RT = 16
CONV_HALO = 32
POST_TM = 256
COMB_TT = 512
MOE_BM = 256
MOE_NCH = 4
MOE_OUT_CHUNK = 256
MOE_GATHER_AHEAD = 3


def _cparams(sem):
    return pltpu.CompilerParams(dimension_semantics=sem, vmem_limit_bytes=VMEM_LIMIT)


def _sigmoid(x):
    return 1.0 / (1.0 + jnp.exp(-x))


def _pack_bf16_pair(lo_f32, hi_f32):
    lo = lax.bitcast_convert_type(lo_f32.astype(BF16).astype(F32), U32) >> 16
    hi = lax.bitcast_convert_type(hi_f32.astype(BF16).astype(F32), U32)
    return hi | lo


def _unpack_lo(u):
    return lax.bitcast_convert_type(u << 16, F32)


def _unpack_hi(u):
    return lax.bitcast_convert_type(u & jnp.uint32(0xFFFF0000), F32)


def _store_tile_rows(ref, lead, x):
    n = x.shape[0]
    for s in range(SUBLANES):
        ref[(*lead, pl.ds(s, n, stride=SUBLANES), slice(None))] = x[:, s * LANES:(s + 1) * LANES]


def _load_tile_rows(ref, lead, n):
    return jnp.concatenate(
        [ref[(*lead, pl.ds(s, n, stride=SUBLANES), slice(None))] for s in range(SUBLANES)], axis=1)


def _inproj_body(x_ref, w_ref, wg_ref, b_ref, bg_ref, cos_ref, sin_ref, o_ref, xb_ref,
                 *, n_q, n_qk, n_qkv, n_glu, scale, head_dim):
    j = pl.program_id(1)
    tn = o_ref.shape[1]

    @pl.when(j == 0)
    def _cast():
        xb_ref[...] = x_ref[...].astype(BF16)

    def project(w, b):
        return jnp.dot(xb_ref[...], w[...], preferred_element_type=F32) + b[...]

    @pl.when(j < n_qk)
    def _rope():
        acc = project(w_ref, b_ref)
        s = jnp.where(j < n_q, scale, 1.0).astype(F32)
        cos = cos_ref[...] * s
        sin = sin_ref[...] * s
        for h in range(tn // head_dim):
            t = acc[:, h * head_dim:(h + 1) * head_dim]
            r = pltpu.roll(t, head_dim // 2, axis=1)
            o_ref[:, h * head_dim:(h + 1) * head_dim] = (t * cos + r * sin).astype(o_ref.dtype)

    @pl.when((j >= n_qk) & (j < n_qkv))
    def _plain():
        o_ref[...] = project(w_ref, b_ref).astype(o_ref.dtype)

    @pl.when((j >= n_qkv) & (j < n_qkv + n_glu))
    def _glu():
        a = project(w_ref, b_ref)
        g = project(wg_ref, bg_ref)
        o_ref[...] = (a * _sigmoid(g)).astype(o_ref.dtype)

    @pl.when(j >= n_qkv + n_glu)
    def _gate():
        o_ref[...] = _sigmoid(project(w_ref, b_ref)).astype(o_ref.dtype)


def _in_projection(x2, w_all, b_all, cos2, sin_s, *, seq, attn_w, conv_c, head_dim, tm, tn):
    T, D = x2.shape
    W = w_all.shape[1] - conv_c
    assert T % tm == 0 and seq % tm == 0 and W % tn == 0
    assert attn_w % tn == 0 and conv_c % tn == 0 and tn % head_dim == 0
    assert head_dim == LANES
    n_q = attn_w // tn
    n_glu = conv_c // tn
    n_qkv = 3 * n_q
    pos_tiles = seq // tm

    def main_idx(i, j):
        return (0, jnp.where(j < n_qkv + n_glu, j, j + n_glu))

    def g_idx(i, j):
        return (0, n_qkv + n_glu + jnp.clip(j - n_qkv, 0, n_glu - 1))

    body = functools.partial(_inproj_body, n_q=n_q, n_qk=2 * n_q, n_qkv=n_qkv, n_glu=n_glu,
                             scale=float(head_dim) ** -0.5 * LOG2_E, head_dim=head_dim)
    return pl.pallas_call(
        body,
        grid=(T // tm, W // tn),
        in_specs=[
            pl.BlockSpec((tm, D), lambda i, j: (i, 0)),
            pl.BlockSpec((D, tn), main_idx),
            pl.BlockSpec((D, tn), g_idx),
            pl.BlockSpec((1, tn), main_idx),
            pl.BlockSpec((1, tn), g_idx),
            pl.BlockSpec((tm, head_dim), lambda i, j: (i % pos_tiles, 0)),
            pl.BlockSpec((tm, head_dim), lambda i, j: (i % pos_tiles, 0)),
        ],
        out_specs=pl.BlockSpec((tm, tn), lambda i, j: (i, j)),
        out_shape=jax.ShapeDtypeStruct((T, W), BF16),
        scratch_shapes=[pltpu.VMEM((tm, D), BF16)],
        compiler_params=_cparams(("parallel", "arbitrary")),
        name="in_projection",
    )(x2, w_all, w_all, b_all, b_all, cos2, sin_s)


def _attn_prep(q, k, v, *, seq, blk):
    nb = seq // blk
    nbp = -(-nb // SUBLANES) * SUBLANES
    dh = q.shape[1]
    contract_last = (((1,), (1,)), ((), ()))
    kmean = jnp.mean(k.astype(F32).reshape(nb, blk, dh), axis=1)
    if nbp > nb:
        kmean = jnp.concatenate([kmean, jnp.zeros((nbp - nb, dh), F32)], axis=0)
    gate_t = lax.dot_general(kmean.astype(BF16), q, contract_last, preferred_element_type=F32)
    ones_rows = (lax.broadcasted_iota(I32, (ATTN_ONES_ROWS, seq), 0) == 0).astype(BF16)
    vt = jnp.concatenate([v.astype(F32).T.astype(BF16), ones_rows], axis=0)
    return gate_t, vt


def _attn_scores(q, k, gate_t, i, s_ref, *, blk, topk, nbp):
    contract_last = (((1,), (1,)), ((), ()))
    neg_inf = jnp.float32(-jnp.inf)
    sub = lax.broadcasted_iota(I32, (nbp, blk), 0)
    qi = q[i * blk:(i + 1) * blk]
    nk = (i + 1) * blk
    st = lax.dot_general(k[:nk], qi, contract_last, preferred_element_type=F32)
    if i > topk:
        gm = jnp.where(sub < i, gate_t[:, i * blk:(i + 1) * blk], neg_inf)
        rank = jnp.zeros((nbp, blk), I32)
        for other in range(i):
            g_other = gm[other:other + 1, :]
            beats = (g_other > gm) | ((g_other == gm) & (sub > other))
            rank = rank + beats.astype(I32)
        bias_t = jnp.where((sub < i) & (rank < topk), 0.0, neg_inf).astype(F32)
    m = None
    for n in range(i + 1):
        t = st[n * blk:(n + 1) * blk]
        if n == i:
            key_r = lax.broadcasted_iota(I32, (blk, blk), 0)
            qry_c = lax.broadcasted_iota(I32, (blk, blk), 1)
            t = jnp.where(key_r <= qry_c, t, neg_inf)
        elif i > topk:
            t = t + bias_t[n:n + 1, :]
        s_ref[n * blk:(n + 1) * blk, :] = t
        t_max = jnp.max(t, axis=0, keepdims=True)
        m = t_max if m is None else jnp.maximum(m, t_max)
    return m


def _attn_output(vt, m, i, s_ref, *, blk, dh):
    nk = (i + 1) * blk
    p = jnp.exp2(s_ref[:nk, :] - m).astype(BF16)
    ot = jnp.dot(vt[:, :nk], p, preferred_element_type=F32)
    return (ot[:dh] * (1.0 / ot[dh:dh + 1])).T


def _attn_body(q_ref, k_ref, v_ref, o_ref, s_scr, *, seq, blk, topk, dh):
    nb = seq // blk
    nbp = -(-nb // SUBLANES) * SUBLANES
    heads = []
    for hh in range(q_ref.shape[1] // dh):
        c0 = hh * dh
        q, k, v = q_ref[:, c0:c0 + dh], k_ref[:, c0:c0 + dh], v_ref[:, c0:c0 + dh]
        heads.append((c0, q, k) + _attn_prep(q, k, v, seq=seq, blk=blk))
    units = [(h, i) for h in range(len(heads)) for i in range(nb)]

    n_buf = s_scr.shape[0]
    ahead = n_buf - 1

    def scores(u):
        h, i = units[u]
        _, q, k, gate_t, _ = heads[h]
        return _attn_scores(q, k, gate_t, i, s_scr.at[u % n_buf], blk=blk, topk=topk, nbp=nbp)

    col_max = {u: scores(u) for u in range(min(ahead, len(units)))}
    for u, (h, i) in enumerate(units):
        if u + ahead < len(units):
            col_max[u + ahead] = scores(u + ahead)
        c0, _, _, _, vt = heads[h]
        o = _attn_output(vt, col_max.pop(u), i, s_scr.at[u % n_buf], blk=blk, dh=dh)
        o_ref[i * blk:(i + 1) * blk, c0:c0 + dh] = o.astype(o_ref.dtype)


def _moba_attention(u, *, batch, seq, n_heads, head_dim, blk, topk, hps):
    T = u.shape[0]
    assert seq % blk == 0 and blk % LANES == 0 and n_heads % hps == 0
    n_hb = n_heads // hps
    body = functools.partial(_attn_body, seq=seq, blk=blk, topk=topk, dh=head_dim)
    return pl.pallas_call(
        body,
        grid=(batch, n_hb),
        in_specs=[
            pl.BlockSpec((seq, hps * head_dim), lambda b, h: (b, h)),
            pl.BlockSpec((seq, hps * head_dim), lambda b, h: (b, n_hb + h)),
            pl.BlockSpec((seq, hps * head_dim), lambda b, h: (b, 2 * n_hb + h)),
        ],
        out_specs=pl.BlockSpec((seq, hps * head_dim), lambda b, h: (b, h)),
        out_shape=jax.ShapeDtypeStruct((T, n_heads * head_dim), BF16),
        scratch_shapes=[pltpu.VMEM((ATTN_LOOKAHEAD + 1, seq, blk), F32)],
        compiler_params=_cparams(("parallel", "parallel")),
        name="moba_attention",
    )(u, u, u)


def _conv_body(x_ref, w_ref, bdw_ref, g_ref, b_ref, o_ref, win_ref, conv_ref,
               *, width, ts, rt, halo, eps):
    ng = win_ref.shape[0]
    sub = SUBLANES

    @pl.when(pl.program_id(1) == 0)
    def _zero_halo():
        win_ref[:, 0:halo * sub, :] = jnp.zeros((ng, halo * sub, LANES), F32)

    for g in range(ng):
        for s in range(sub):
            c0 = (g * sub + s) * LANES
            win_ref[g, pl.ds(halo * sub + s, ts, stride=sub), :] = x_ref[:, c0:c0 + LANES].astype(F32)

    first = halo - (width - 1)
    for g in range(ng):
        def chunk(r, carry, g=g):
            t0 = r * rt
            acc = jnp.broadcast_to(bdw_ref[g][None], (rt, sub, LANES))
            for j in range(width):
                start = pl.multiple_of((t0 + first + j) * sub, sub)
                slab = win_ref[g, pl.ds(start, rt * sub), :].reshape(rt, sub, LANES)
                acc = acc + slab * w_ref[g, j][None]
            conv_ref[g, pl.ds(pl.multiple_of(t0 * sub, sub), rt * sub), :] = acc.reshape(rt * sub, LANES)
            return carry

        lax.fori_loop(0, ts // rt, chunk, 0)

    win_ref[:, 0:halo * sub, :] = win_ref[:, ts * sub:(ts + halo) * sub, :]

    def channel_chunks():
        for g in range(ng):
            for s in range(sub):
                yield (g * sub + s) * LANES, conv_ref[g, pl.ds(s, ts, stride=sub), :]

    n = ng * sub * LANES
    total = jnp.zeros((ts, 1), F32)
    for _, y in channel_chunks():
        total = total + jnp.sum(y, axis=1, keepdims=True)
    mu = total * (1.0 / n)
    sq = jnp.zeros((ts, 1), F32)
    for _, y in channel_chunks():
        d = y - mu
        sq = sq + jnp.sum(d * d, axis=1, keepdims=True)
    inv = lax.rsqrt(sq * (1.0 / n) + eps)
    for c0, y in channel_chunks():
        z = (y - mu) * inv * g_ref[:, c0:c0 + LANES] + b_ref[:, c0:c0 + LANES]
        o_ref[:, c0:c0 + LANES] = (z * _sigmoid(z)).astype(o_ref.dtype)


def _conv_branch(u, w_dw, b_dw, ln_g, ln_b, *, batch, seq, conv_c, col_block, ts, rt, halo):
    T = u.shape[0]
    width = w_dw.shape[0]
    gw = SUBLANES * LANES
    assert seq % ts == 0 and conv_c % gw == 0 and ts % rt == 0
    assert halo >= width - 1 and ts >= halo
    ng = conv_c // gw
    w_t = w_dw.reshape(width, ng, SUBLANES, LANES).transpose(1, 0, 2, 3)
    n_s = seq // ts
    body = functools.partial(_conv_body, width=width, ts=ts, rt=rt, halo=halo, eps=LN_EPS)
    return pl.pallas_call(
        body,
        grid=(batch, n_s),
        in_specs=[
            pl.BlockSpec((ts, conv_c), lambda b, s: (b * n_s + s, col_block)),
            pl.BlockSpec((ng, width, SUBLANES, LANES), lambda b, s: (0, 0, 0, 0)),
            pl.BlockSpec((ng, SUBLANES, LANES), lambda b, s: (0, 0, 0)),
            pl.BlockSpec((1, conv_c), lambda b, s: (0, 0)),
            pl.BlockSpec((1, conv_c), lambda b, s: (0, 0)),
        ],
        out_specs=pl.BlockSpec((ts, conv_c), lambda b, s: (b * n_s + s, 0)),
        out_shape=jax.ShapeDtypeStruct((T, conv_c), BF16),
        scratch_shapes=[pltpu.VMEM((ng, (halo + ts) * SUBLANES, LANES), F32),
                        pltpu.VMEM((ng, ts * SUBLANES, LANES), F32)],
        compiler_params=_cparams(("parallel", "arbitrary")),
        name="conv_branch",
    )(u, w_t, b_dw.reshape(ng, SUBLANES, LANES), ln_g[None, :], ln_b[None, :])


def _post_body(o_ref, hc_ref, ga_ref, gb_ref, x_ref, wo_ref, wp_ref, bp_ref, wout_ref,
               g1_ref, b1_ref, wrh_ref, wrl_ref, br_ref, h_ref, hp_ref, rt_ref, m_scr, z_scr,
               *, alpha, eps, n_groups, e_per_group):
    i = pl.program_id(0)

    @pl.when(i == 0)
    def _no_previous_tiles():
        m_scr[1] = jnp.zeros(m_scr.shape[1:], BF16)
        z_scr[1] = jnp.zeros(z_scr.shape[1:], F32)
        z_scr[2] = jnp.zeros(z_scr.shape[1:], F32)

    ya = jnp.dot(o_ref[...], wo_ref[...], preferred_element_type=F32)

    z = z_scr[(i + 1) % 3]
    mu = jnp.mean(z, axis=1, keepdims=True)
    zc = z - mu
    var = jnp.mean(zc * zc, axis=1, keepdims=True)
    h = zc * lax.rsqrt(var + eps) * g1_ref[...] + b1_ref[...]
    h_ref[...] = h
    half = h.shape[1] // 2
    _store_tile_rows(hp_ref, (), _pack_bf16_pair(h[:, :half], h[:, half:]))

    yc = jnp.dot(hc_ref[...], wp_ref[...], preferred_element_type=F32) + bp_ref[...]

    h_hi = h.astype(BF16)
    h_lo = (h - h_hi.astype(F32)).astype(BF16)
    hi_terms = jnp.dot(h_hi, wrl_ref[...], preferred_element_type=F32)
    logits = (hi_terms[:, :LANES] + hi_terms[:, LANES:]
              + jnp.dot(h_lo, wrh_ref[...], preferred_element_type=F32) + br_ref[...])
    tm = logits.shape[0]
    lane = lax.broadcasted_iota(I32, (tm, LANES), 1)
    neg_inf = jnp.float32(-jnp.inf)
    big = jnp.int32(LANES)

    def first_argmax(vals):
        top = jnp.max(vals, axis=1, keepdims=True)
        idx = jnp.min(jnp.where(vals == top, lane, big), axis=1, keepdims=True)
        return top, idx

    gl = jnp.where(lane < n_groups, logits, neg_inf)
    gmax, grp = first_argmax(gl)
    grp_w = 1.0 / jnp.sum(jnp.exp(gl - gmax), axis=1, keepdims=True)
    lo_lane = n_groups + grp * e_per_group
    el = jnp.where((lane >= lo_lane) & (lane < lo_lane + e_per_group), logits, neg_inf)
    v1, i1 = first_argmax(el)
    v2, i2 = first_argmax(jnp.where(lane == i1, neg_inf, el))
    t = jnp.exp(v2 - v1)
    p1 = 1.0 / (1.0 + t)
    c1 = p1 * grp_w
    c2 = (t * p1) * grp_w
    e1 = (i1 - n_groups).astype(F32)
    e2 = (i2 - n_groups).astype(F32)
    rt_ref[...] = jnp.where(lane == 0, e1, jnp.where(lane == 1, e2,
                            jnp.where(lane == 2, c1, jnp.where(lane == 3, c2, 0.0))))

    prev = (i + 1) % 2
    z_scr[(i + 2) % 3] += jnp.dot(m_scr[prev], wout_ref[...], preferred_element_type=F32)
    m_scr[i % 2] = (ga_ref[...].astype(F32) * ya + gb_ref[...].astype(F32) * yc).astype(BF16)
    z_scr[i % 3] = alpha * x_ref[...]


def _post_block(o, hc, u, x2, wo, wp, bp, wout, g1, b1, wr_hi, wr_lo, br, *, gate_block,
                alpha, n_groups, e_per_group, tm):
    T, D = x2.shape
    A = o.shape[1]
    C = hc.shape[1]
    assert T % tm == 0 and D == 2 * SUBLANES * LANES, "packed rows are stored as one (8, 128) tile"
    assert n_groups * (1 + e_per_group) <= LANES
    const = lambda i: (0, 0)
    resident = lambda shape: pl.BlockSpec(shape, const, pipeline_mode=pl.Buffered(1))
    body = functools.partial(_post_body, alpha=alpha, eps=LN_EPS, n_groups=n_groups,
                             e_per_group=e_per_group)
    n = T // tm
    cur = lambda i: jnp.minimum(i, n - 1)
    prev = lambda i: jnp.maximum(i - 2, 0)
    return pl.pallas_call(
        body,
        grid=(n + 2,),
        in_specs=[
            pl.BlockSpec((tm, A), lambda i: (cur(i), 0)),
            pl.BlockSpec((tm, C), lambda i: (cur(i), 0)),
            pl.BlockSpec((tm, D), lambda i: (cur(i), gate_block)),
            pl.BlockSpec((tm, D), lambda i: (cur(i), gate_block + 1)),
            pl.BlockSpec((tm, D), lambda i: (cur(i), 0)),
            resident((A, D)),
            resident((C, D)),
            resident((1, D)),
            resident((D, D)),
            resident((1, D)),
            resident((1, D)),
            resident((D, LANES)),
            resident((D, 2 * LANES)),
            resident((1, LANES)),
        ],
        out_specs=[
            pl.BlockSpec((tm, D), lambda i: (prev(i), 0)),
            pl.BlockSpec((tm * SUBLANES, LANES), lambda i: (prev(i), 0)),
            pl.BlockSpec((tm, LANES), lambda i: (prev(i), 0)),
        ],
        out_shape=[
            jax.ShapeDtypeStruct((T, D), F32),
            jax.ShapeDtypeStruct((T * SUBLANES, LANES), U32),
            jax.ShapeDtypeStruct((T, LANES), F32),
        ],
        scratch_shapes=[pltpu.VMEM((2, tm, D), BF16), pltpu.VMEM((3, tm, D), F32)],
        compiler_params=_cparams(("arbitrary",)),
        name="merge_project_route",
    )(o, hc, u, u, x2, wo, wp, bp, wout, g1, b1, wr_hi, wr_lo, br)


def _row_copy(src_ref, src_row8, dst_ref, dst_row8, sem):
    aligned = lambda v: v if isinstance(v, int) else pl.multiple_of(v, SUBLANES)
    return pltpu.make_async_copy(src_ref.at[pl.ds(aligned(src_row8), SUBLANES)],
                                 dst_ref.at[pl.ds(aligned(dst_row8), SUBLANES)], sem)


def _moe_body(es_ref, ec_ref, src_ref, dst_ref, hp_ref, w1_ref, w3_ref, w2_ref, y2_ref,
              w1b, w3b, w2b, xbuf, ybuf, src, dst, gsem, ssem, src_sem, dst_sem,
              *, n_experts, nch, bm, spare_row0):
    g = pl.program_id(0)
    c = pl.program_id(1)
    fc = w1_ref.shape[1]
    half = SUBLANES * LANES
    n_total = es_ref[n_experts - 1] + ec_ref[n_experts - 1]
    n_xbuf = xbuf.shape[0]
    ahead = n_xbuf - 1

    def src_copy(q):
        return pltpu.make_async_copy(src_ref.at[q], src.at[q % 2], src_sem.at[q % 2])

    def dst_copy(q):
        return pltpu.make_async_copy(dst_ref.at[q + 1], dst.at[(q + 1) % 2], dst_sem.at[(q + 1) % 2])

    def gather_row(q, r):
        return _row_copy(hp_ref, src[q % 2, 0, r], xbuf.at[q % n_xbuf], r * SUBLANES,
                         gsem.at[q % n_xbuf])

    def scatter_row(q, r):
        return _row_copy(ybuf.at[(q + 3) % 3], r * SUBLANES, y2_ref, dst[(q + 1) % 2, 0, r],
                         ssem.at[(q + 3) % 3])

    def gather_wait(q):
        for r in range(bm):
            _row_copy(hp_ref, 0, xbuf.at[q % n_xbuf], r * SUBLANES, gsem.at[q % n_xbuf]).wait()

    def scatter_wait(q):
        for r in range(bm):
            _row_copy(ybuf.at[(q + 3) % 3], r * SUBLANES, y2_ref, 0, ssem.at[(q + 3) % 3]).wait()

    @pl.when((g == 0) & (c == 0))
    def _prime():
        ybuf[...] = jnp.zeros(ybuf.shape, U32)
        spare = [pltpu.make_async_copy(
            ybuf.at[s], y2_ref.at[pl.ds((spare_row0 + s * bm) * SUBLANES, bm * SUBLANES)], ssem.at[s])
            for s in range(2)]
        for cp in spare:
            cp.start()
        for cp in spare:
            cp.wait()
        dst_copy(-1).start()
        for q in range(ahead):
            src_copy(q).start()
            src_copy(q).wait()
            for r in range(bm):
                gather_row(q, r).start()
        src_copy(ahead).start()

    @pl.when(g < n_experts)
    def _cast_next_expert_chunk():
        slot = g % 2
        w1b[slot, c] = w1_ref[...].astype(BF16)
        w3b[slot, c] = w3_ref[...].astype(BF16)
        w2b[slot, pl.ds(pl.multiple_of(c * fc, fc), fc), :] = w2_ref[...].astype(BF16)

    @pl.when(g >= 1)
    def _compute_previous_expert_share():
        e = g - 1
        slot = e % 2
        nb = ec_ref[e]
        base = es_ref[e]
        n_out = w2b.shape[2] // MOE_OUT_CHUNK

        def block(r, carry):
            q = base + r
            src_copy(q + ahead).wait()
            src_copy(q + ahead + 1).start()
            dst_copy(q - 1).wait()
            dst_copy(q).start()
            gather_wait(q)

            @pl.when(q >= 2)
            def _():
                scatter_wait(q - 3)

            pending = []
            for rr in range(bm):
                pending.append(scatter_row(q - 1, rr))
                pending.append(gather_row(q + ahead, rr))
            per_gap = -(-len(pending) // (4 * nch + n_out))

            def issue_some():
                for _ in range(min(per_gap, len(pending))):
                    pending.pop(0).start()

            u = _load_tile_rows(xbuf, (q % n_xbuf,), bm)
            x_lo = _unpack_lo(u).astype(BF16)
            x_hi = _unpack_hi(u).astype(BF16)
            hs = []
            for cc in range(nch):
                a = jnp.dot(x_lo, w1b[slot, cc, :half, :], preferred_element_type=F32)
                issue_some()
                a = a + jnp.dot(x_hi, w1b[slot, cc, half:, :], preferred_element_type=F32)
                issue_some()
                b = jnp.dot(x_lo, w3b[slot, cc, :half, :], preferred_element_type=F32)
                issue_some()
                b = b + jnp.dot(x_hi, w3b[slot, cc, half:, :], preferred_element_type=F32)
                issue_some()
                hs.append((a * _sigmoid(a) * b).astype(BF16))
            hmid = jnp.concatenate(hs, axis=1)
            ys = []
            for oc in range(n_out):
                cols = slice(oc * MOE_OUT_CHUNK, (oc + 1) * MOE_OUT_CHUNK)
                ys.append(jnp.dot(hmid, w2b[slot, :, cols], preferred_element_type=F32))
                issue_some()
            assert not pending
            y = jnp.concatenate(ys, axis=1)
            _store_tile_rows(ybuf, (q % 3,), _pack_bf16_pair(y[:, :half], y[:, half:]))
            return carry

        lax.fori_loop((nb * c) // nch, (nb * (c + 1)) // nch, block, 0)

    @pl.when((g == n_experts) & (c == nch - 1))
    def _drain():
        dst_copy(n_total - 1).wait()
        for r in range(bm):
            scatter_row(n_total - 1, r).start()

        @pl.when(n_total >= 2)
        def _():
            scatter_wait(n_total - 3)

        @pl.when(n_total >= 1)
        def _():
            scatter_wait(n_total - 2)

        scatter_wait(n_total - 1)
        for k in range(ahead):
            gather_wait(n_total + k)
        src_copy(n_total + ahead).wait()


def _moe_experts(eb_start, eb_count, src_plan, dst_plan, hp, w1, w3, w2, *, n_tokens, bm, nch):
    n_out_rows = 2 * n_tokens + 2 * bm
    E, D, F = w1.shape
    assert D == 2 * SUBLANES * LANES and hp.shape[1] == LANES and D % MOE_OUT_CHUNK == 0
    assert F % nch == 0 and src_plan.shape[1:] == (1, bm) and dst_plan.shape[1:] == (1, bm)
    assert dst_plan.shape[0] == src_plan.shape[0] + 1
    fc = F // nch

    def w_in_idx(g, c, es, ec):
        return (jnp.minimum(g, E - 1), 0, jnp.where(g < E, c, nch - 1))

    def w_out_idx(g, c, es, ec):
        return (jnp.minimum(g, E - 1), jnp.where(g < E, c, nch - 1), 0)

    grid_spec = pltpu.PrefetchScalarGridSpec(
        num_scalar_prefetch=2,
        grid=(E + 1, nch),
        in_specs=[
            pl.BlockSpec(memory_space=pl.ANY),
            pl.BlockSpec(memory_space=pl.ANY),
            pl.BlockSpec(memory_space=pl.ANY),
            pl.BlockSpec((None, D, fc), w_in_idx),
            pl.BlockSpec((None, D, fc), w_in_idx),
            pl.BlockSpec((None, fc, D), w_out_idx),
        ],
        out_specs=pl.BlockSpec(memory_space=pl.ANY),
        scratch_shapes=[
            pltpu.VMEM((2, nch, D, fc), BF16),
            pltpu.VMEM((2, nch, D, fc), BF16),
            pltpu.VMEM((2, F, D), BF16),
            pltpu.VMEM((MOE_GATHER_AHEAD + 1, bm * SUBLANES, LANES), U32),
            pltpu.VMEM((3, bm * SUBLANES, LANES), U32),
            pltpu.SMEM((2, 1, bm), I32),
            pltpu.SMEM((2, 1, bm), I32),
            pltpu.SemaphoreType.DMA((MOE_GATHER_AHEAD + 1,)),
            pltpu.SemaphoreType.DMA((3,)),
            pltpu.SemaphoreType.DMA((2,)),
            pltpu.SemaphoreType.DMA((2,)),
        ],
    )
    body = functools.partial(_moe_body, n_experts=E, nch=nch, bm=bm, spare_row0=2 * n_tokens)
    return pl.pallas_call(
        body,
        grid_spec=grid_spec,
        out_shape=jax.ShapeDtypeStruct((n_out_rows * SUBLANES, LANES), U32),
        compiler_params=_cparams(("arbitrary", "arbitrary")),
        name="moe_experts",
    )(eb_start, eb_count, src_plan, dst_plan, hp, w1, w3, w2)


def _combine_body(h_ref, rt_ref, g_ref, b_ref, y0_ref, y1_ref, o_ref, *, alpha, eps):
    tt = h_ref.shape[0]
    u0 = _load_tile_rows(y0_ref, (), tt)
    u1 = _load_tile_rows(y1_ref, (), tt)
    half = u0.shape[1]
    c0 = rt_ref[:, 2:3]
    c1 = rt_ref[:, 3:4]
    z_lo = alpha * h_ref[:, :half] + (_unpack_lo(u0) * c0 + _unpack_lo(u1) * c1)
    z_hi = alpha * h_ref[:, half:] + (_unpack_hi(u0) * c0 + _unpack_hi(u1) * c1)
    n = 2 * half
    mu = (jnp.sum(z_lo, axis=1, keepdims=True) + jnp.sum(z_hi, axis=1, keepdims=True)) * (1.0 / n)
    d_lo = z_lo - mu
    d_hi = z_hi - mu
    var = (jnp.sum(d_lo * d_lo, axis=1, keepdims=True)
           + jnp.sum(d_hi * d_hi, axis=1, keepdims=True)) * (1.0 / n)
    inv = lax.rsqrt(var + eps)
    o_ref[:, :half] = d_lo * inv * g_ref[:, :half] + b_ref[:, :half]
    o_ref[:, half:] = d_hi * inv * g_ref[:, half:] + b_ref[:, half:]


def _combine(h, rt, y2, g2, b2, *, alpha, tt):
    T, D = h.shape
    assert T % tt == 0 and D == 2 * SUBLANES * LANES and y2.shape[1] == LANES
    body = functools.partial(_combine_body, alpha=alpha, eps=LN_EPS)
    return pl.pallas_call(
        body,
        grid=(T // tt,),
        in_specs=[
            pl.BlockSpec((tt, D), lambda i: (i, 0)),
            pl.BlockSpec((tt, LANES), lambda i: (i, 0)),
            pl.BlockSpec((1, D), lambda i: (0, 0)),
            pl.BlockSpec((1, D), lambda i: (0, 0)),
            pl.BlockSpec((tt * SUBLANES, LANES), lambda i: (i, 0)),
            pl.BlockSpec((tt * SUBLANES, LANES), lambda i: (i + T // tt, 0)),
        ],
        out_specs=pl.BlockSpec((tt, D), lambda i: (i, 0)),
        out_shape=jax.ShapeDtypeStruct((T, D), F32),
        compiler_params=_cparams(("parallel",)),
        name="moe_combine",
    )(h, rt, g2, b2, y2, y2)


def _slot_plan(expert, n_experts, bm, n_plan_blocks):
    T = expert.shape[0]
    e_flat = expert.reshape(-1)
    onehot = (e_flat[:, None] == jnp.arange(n_experts, dtype=I32)[None, :]).astype(I32)
    csum = jnp.cumsum(onehot, axis=0)
    counts = csum[-1]
    eb_count = (counts + bm - 1) // bm
    eb_start = jnp.cumsum(eb_count) - eb_count
    slot = jnp.sum(onehot * (csum - 1 + (eb_start * bm)[None, :]), axis=1)
    n_slots = n_plan_blocks * bm
    tok = jnp.arange(2 * T, dtype=I32) // 2
    k = jnp.arange(2 * T, dtype=I32) % 2
    p = jnp.arange(n_slots, dtype=I32)
    spare = 2 * T + ((p // bm) % 2) * bm + p % bm
    dst = spare.at[slot].set(k * T + tok, unique_indices=True)
    src = jnp.where(dst < 2 * T, dst % T, 0)
    dst = jnp.concatenate([2 * T + bm + jnp.arange(bm, dtype=I32), dst])
    to_blocks = lambda a: (a * SUBLANES).reshape(-1, 1, bm)
    return eb_start.astype(I32), eb_count.astype(I32), to_blocks(src), to_blocks(dst)


def _layer(x2, p, *, batch, seq, depth, n_heads, head_dim, moba_block, moba_topk,
           n_groups, e_per_group, tiles):
    T, D = x2.shape
    A = n_heads * head_dim
    C = p["w_dw"].shape[1]
    alpha = (2.0 * depth) ** 0.25
    tn = tiles["proj_tn"]
    assert A == C == D, "column-block addressing below assumes equal branch widths"

    w_in, b_in = p["w_in"], p["b_in"]

    half = head_dim // 2
    inv_freq = jnp.power(ROPE_THETA, -jnp.arange(half, dtype=F32) * (2.0 / head_dim))
    ang = jnp.arange(seq, dtype=F32)[:, None] * inv_freq[None, :]
    cos2 = jnp.concatenate([jnp.cos(ang), jnp.cos(ang)], axis=1)
    sin_s = jnp.concatenate([-jnp.sin(ang), jnp.sin(ang)], axis=1)

    u = _in_projection(x2, w_in.astype(BF16), b_in[None, :], cos2, sin_s, seq=seq, attn_w=A, conv_c=C,
                       head_dim=head_dim, tm=tiles["proj_tm"], tn=tn)
    o = _moba_attention(u, batch=batch, seq=seq, n_heads=n_heads, head_dim=head_dim,
                        blk=moba_block, topk=moba_topk, hps=ATTN_HPS)
    hc = _conv_branch(u, p["w_dw"], p["b_dw"], p["conv_ln_g"], p["conv_ln_b"], batch=batch,
                      seq=seq, conv_c=C, col_block=3, ts=tiles["conv_ts"], rt=tiles["conv_rt"],
                      halo=CONV_HALO)

    n_experts = n_groups * e_per_group
    n_route = n_groups + n_experts
    w_route = jnp.concatenate(
        [p["w_rg"], p["w_re"].transpose(1, 0, 2).reshape(D, n_experts),
         jnp.zeros((D, LANES - n_route), F32)], axis=1)
    b_route = jnp.concatenate(
        [p["b_rg"], p["b_re"].reshape(n_experts), jnp.zeros((LANES - n_route,), F32)])[None, :]
    wr_hi = w_route.astype(BF16)
    wr_lo = jnp.concatenate([wr_hi, (w_route - wr_hi.astype(F32)).astype(BF16)], axis=1)

    h, hp, rt = _post_block(
        o, hc, u, x2, p["w_o_attn"].astype(BF16), p["w_pw2"].astype(BF16), p["b_pw2"][None, :],
        p["w_out"].astype(BF16), p["ln1_g"][None, :], p["ln1_b"][None, :], wr_hi, wr_lo, b_route,
        gate_block=4, alpha=alpha, n_groups=n_groups, e_per_group=e_per_group,
        tm=tiles["post_tm"])

    bm = tiles["moe_bm"]
    n_plan_blocks = -(-2 * T // bm) + n_experts + MOE_GATHER_AHEAD + 1
    expert = rt[:, :2].astype(I32)
    eb_start, eb_count, src_plan, dst_plan = _slot_plan(expert, n_experts, bm, n_plan_blocks)
    y2 = _moe_experts(eb_start, eb_count, src_plan, dst_plan, hp, p["w1"], p["w3"], p["w2"],
                      n_tokens=T, bm=bm, nch=tiles["moe_nch"])
    return _combine(h, rt, y2, p["ln2_g"][None, :], p["ln2_b"][None, :], alpha=alpha,
                    tt=tiles["comb_tt"])


_PARAM_NAMES = ("w_in", "b_in", "w_o_attn", "w_dw", "b_dw", "conv_ln_g", "conv_ln_b", "w_pw2",
                "b_pw2", "w_out", "ln1_g", "ln1_b", "w_rg", "b_rg", "w_re", "b_re", "w1", "w3",
                "w2", "ln2_g", "ln2_b")

_TILES = dict(proj_tm=PROJ_TM, proj_tn=PROJ_TN, conv_ts=CONV_TS, conv_rt=CONV_RT,
              post_tm=POST_TM, comb_tt=COMB_TT, moe_bm=MOE_BM, moe_nch=MOE_NCH)


def _forward(x, params, *, n_heads=N_HEADS, head_dim=HEAD_DIM, moba_block=MOBA_BLOCK,
             moba_topk=MOBA_TOPK, n_groups=N_GROUPS, e_per_group=EXPERTS_PER_GROUP, tiles=None):
    tiles = dict(_TILES, **(tiles or {}))
    B, S, D = x.shape
    depth = params["w_in"].shape[0]
    x2 = x.reshape(B * S, D)
    for l in range(depth):
        p = {k: v[l] for k, v in params.items()}
        x2 = _layer(x2, p, batch=B, seq=S, depth=depth, n_heads=n_heads, head_dim=head_dim,
                    moba_block=moba_block, moba_topk=moba_topk, n_groups=n_groups,
                    e_per_group=e_per_group, tiles=tiles)
    return x2.reshape(B, S, D)


def kernel(x, w_in, b_in, w_o_attn, w_dw, b_dw, conv_ln_g, conv_ln_b, w_pw2, b_pw2, w_out, ln1_g,
           ln1_b, w_rg, b_rg, w_re, b_re, w1, w3, w2, ln2_g, ln2_b):
    params = dict(zip(_PARAM_NAMES, (w_in, b_in, w_o_attn, w_dw, b_dw, conv_ln_g, conv_ln_b, w_pw2,
                                     b_pw2, w_out, ln1_g, ln1_b, w_rg, b_rg, w_re, b_re, w1, w3,
                                     w2, ln2_g, ln2_b)))
    return _forward(x, params)
```

```python
import functools

import jax
import jax.numpy as jnp
from jax import lax
from jax.experimental import pallas as pl
from jax.experimental.pallas import tpu as pltpu

F32 = jnp.float32
BF16 = jnp.bfloat16
U32 = jnp.uint32
I32 = jnp.int32

N_HEADS = 16
HEAD_DIM = 128
ROPE_THETA = 10000.0
MOBA_BLOCK = 256
MOBA_TOPK = 3
CONV_WIDTH = 31
N_GROUPS = 4
EXPERTS_PER_GROUP = 8
LN_EPS = 1e-5
LOG2_E = 1.4426950408889634

LANES = 128
SUBLANES = 8
VMEM_LIMIT = 56 * 1024 * 1024

PROJ_TM = 1024
PROJ_TN = 1024
ATTN_HPS = 2
ATTN_LOOKAHEAD = 2
ATTN_ONES_ROWS = 16
CONV_TS = 256
CONV_RT = 16
CONV_HALO = 32
CONV_TAP_GROUP = 4
POST_TM = 256
COMB_TT = 512
MOE_BM = 256
MOE_NCH = 4
MOE_OUT_CHUNK = 256
MOE_GATHER_AHEAD = 2


def _cparams(sem):
    return pltpu.CompilerParams(dimension_semantics=sem, vmem_limit_bytes=VMEM_LIMIT)


def _sigmoid(x):
    return 1.0 / (1.0 + jnp.exp(-x))


def _pack_bf16_pair(lo_f32, hi_f32):
    lo = lax.bitcast_convert_type(lo_f32.astype(BF16).astype(F32), U32) >> 16
    hi = lax.bitcast_convert_type(hi_f32.astype(BF16).astype(F32), U32)
    return hi | lo


def _unpack_lo(u):
    return lax.bitcast_convert_type(u << 16, F32)


def _unpack_hi(u):
    return lax.bitcast_convert_type(u & jnp.uint32(0xFFFF0000), F32)


def _store_tile_rows(ref, lead, x):
    n = x.shape[0]
    for s in range(SUBLANES):
        ref[(*lead, pl.ds(s, n, stride=SUBLANES), slice(None))] = x[:, s * LANES:(s + 1) * LANES]


def _load_tile_rows(ref, lead, n):
    return jnp.concatenate(
        [ref[(*lead, pl.ds(s, n, stride=SUBLANES), slice(None))] for s in range(SUBLANES)], axis=1)


def _inproj_body(x_ref, w_ref, wg_ref, b_ref, bg_ref, cos_ref, sin_ref, o_ref, xb_ref,
                 *, n_q, n_qk, n_qkv, n_glu, scale, head_dim):
    j = pl.program_id(1)
    tn = o_ref.shape[1]

    @pl.when(j == 0)
    def _cast():
        xb_ref[...] = x_ref[...].astype(BF16)

    def project(w, b):
        return jnp.dot(xb_ref[...], w[...], preferred_element_type=F32) + b[...]

    @pl.when(j < n_qk)
    def _rope():
        acc = project(w_ref, b_ref)
        s = jnp.where(j < n_q, scale, 1.0).astype(F32)
        cos = cos_ref[...] * s
        sin = sin_ref[...] * s
        for h in range(tn // head_dim):
            t = acc[:, h * head_dim:(h + 1) * head_dim]
            r = pltpu.roll(t, head_dim // 2, axis=1)
            o_ref[:, h * head_dim:(h + 1) * head_dim] = (t * cos + r * sin).astype(o_ref.dtype)

    @pl.when((j >= n_qk) & (j < n_qkv))
    def _plain():
        o_ref[...] = project(w_ref, b_ref).astype(o_ref.dtype)

    @pl.when((j >= n_qkv) & (j < n_qkv + n_glu))
    def _glu():
        a = project(w_ref, b_ref)
        g = project(wg_ref, bg_ref)
        o_ref[...] = (a * _sigmoid(g)).astype(o_ref.dtype)

    @pl.when(j >= n_qkv + n_glu)
    def _gate():
        o_ref[...] = _sigmoid(project(w_ref, b_ref)).astype(o_ref.dtype)


def _in_projection(x2, w_all, b_all, cos2, sin_s, *, seq, attn_w, conv_c, head_dim, tm, tn):
    T, D = x2.shape
    W = w_all.shape[1] - conv_c
    assert T % tm == 0 and seq % tm == 0 and W % tn == 0
    assert attn_w % tn == 0 and conv_c % tn == 0 and tn % head_dim == 0
    assert head_dim == LANES
    n_q = attn_w // tn
    n_glu = conv_c // tn
    n_qkv = 3 * n_q
    pos_tiles = seq // tm

    def main_idx(i, j):
        return (0, jnp.where(j < n_qkv + n_glu, j, j + n_glu))

    def g_idx(i, j):
        return (0, n_qkv + n_glu + jnp.clip(j - n_qkv, 0, n_glu - 1))

    body = functools.partial(_inproj_body, n_q=n_q, n_qk=2 * n_q, n_qkv=n_qkv, n_glu=n_glu,
                             scale=float(head_dim) ** -0.5 * LOG2_E, head_dim=head_dim)
    return pl.pallas_call(
        body,
        grid=(T // tm, W // tn),
        in_specs=[
            pl.BlockSpec((tm, D), lambda i, j: (i, 0)),
            pl.BlockSpec((D, tn), main_idx),
            pl.BlockSpec((D, tn), g_idx),
            pl.BlockSpec((1, tn), main_idx),
            pl.BlockSpec((1, tn), g_idx),
            pl.BlockSpec((tm, head_dim), lambda i, j: (i % pos_tiles, 0)),
            pl.BlockSpec((tm, head_dim), lambda i, j: (i % pos_tiles, 0)),
        ],
        out_specs=pl.BlockSpec((tm, tn), lambda i, j: (i, j)),
        out_shape=jax.ShapeDtypeStruct((T, W), BF16),
        scratch_shapes=[pltpu.VMEM((tm, D), BF16)],
        compiler_params=_cparams(("parallel", "arbitrary")),
        name="in_projection",
    )(x2, w_all, w_all, b_all, b_all, cos2, sin_s)


def _attn_prep(q, k, v, *, seq, blk):
    nb = seq // blk
    nbp = -(-nb // SUBLANES) * SUBLANES
    dh = q.shape[1]
    contract_last = (((1,), (1,)), ((), ()))
    kmean = jnp.mean(k.astype(F32).reshape(nb, blk, dh), axis=1)
    if nbp > nb:
        kmean = jnp.concatenate([kmean, jnp.zeros((nbp - nb, dh), F32)], axis=0)
    gate_t = lax.dot_general(kmean.astype(BF16), q, contract_last, preferred_element_type=F32)
    ones_rows = (lax.broadcasted_iota(I32, (ATTN_ONES_ROWS, seq), 0) == 0).astype(BF16)
    vt = jnp.concatenate([v.astype(F32).T.astype(BF16), ones_rows], axis=0)
    return gate_t, vt


def _attn_scores(q, k, gate_t, i, s_ref, *, blk, topk, nbp):
    contract_last = (((1,), (1,)), ((), ()))
    neg_inf = jnp.float32(-jnp.inf)
    sub = lax.broadcasted_iota(I32, (nbp, blk), 0)
    qi = q[i * blk:(i + 1) * blk]
    nk = (i + 1) * blk
    st = lax.dot_general(k[:nk], qi, contract_last, preferred_element_type=F32)
    if i > topk:
        gm = jnp.where(sub < i, gate_t[:, i * blk:(i + 1) * blk], neg_inf)
        rank = jnp.zeros((nbp, blk), I32)
        for other in range(i):
            g_other = gm[other:other + 1, :]
            beats = (g_other > gm) | ((g_other == gm) & (sub > other))
            rank = rank + beats.astype(I32)
        bias_t = jnp.where((sub < i) & (rank < topk), 0.0, neg_inf).astype(F32)
    m = None
    for n in range(i + 1):
        t = st[n * blk:(n + 1) * blk]
        if n == i:
            key_r = lax.broadcasted_iota(I32, (blk, blk), 0)
            qry_c = lax.broadcasted_iota(I32, (blk, blk), 1)
            t = jnp.where(key_r <= qry_c, t, neg_inf)
        elif i > topk:
            t = t + bias_t[n:n + 1, :]
        s_ref[n * blk:(n + 1) * blk, :] = t
        t_max = jnp.max(t, axis=0, keepdims=True)
        m = t_max if m is None else jnp.maximum(m, t_max)
    return m


def _attn_output(vt, m, i, s_ref, *, blk, dh):
    nk = (i + 1) * blk
    p = jnp.exp2(s_ref[:nk, :] - m).astype(BF16)
    ot = jnp.dot(vt[:, :nk], p, preferred_element_type=F32)
    return (ot[:dh] * (1.0 / ot[dh:dh + 1])).T


def _attn_body(q_ref, k_ref, v_ref, o_ref, s_scr, *, seq, blk, topk, dh):
    nb = seq // blk
    nbp = -(-nb // SUBLANES) * SUBLANES
    heads = []
    for hh in range(q_ref.shape[1] // dh):
        c0 = hh * dh
        q, k, v = q_ref[:, c0:c0 + dh], k_ref[:, c0:c0 + dh], v_ref[:, c0:c0 + dh]
        heads.append((c0, q, k) + _attn_prep(q, k, v, seq=seq, blk=blk))
    units = [(h, i) for h in range(len(heads)) for i in range(nb)]

    n_buf = s_scr.shape[0]
    ahead = n_buf - 1

    def scores(u):
        h, i = units[u]
        _, q, k, gate_t, _ = heads[h]
        return _attn_scores(q, k, gate_t, i, s_scr.at[u % n_buf], blk=blk, topk=topk, nbp=nbp)

    col_max = {u: scores(u) for u in range(min(ahead, len(units)))}
    for u, (h, i) in enumerate(units):
        if u + ahead < len(units):
            col_max[u + ahead] = scores(u + ahead)
        c0, _, _, _, vt = heads[h]
        o = _attn_output(vt, col_max.pop(u), i, s_scr.at[u % n_buf], blk=blk, dh=dh)
        o_ref[i * blk:(i + 1) * blk, c0:c0 + dh] = o.astype(o_ref.dtype)


def _moba_attention(u, *, batch, seq, n_heads, head_dim, blk, topk, hps):
    T = u.shape[0]
    assert seq % blk == 0 and blk % LANES == 0 and n_heads % hps == 0
    n_hb = n_heads // hps
    body = functools.partial(_attn_body, seq=seq, blk=blk, topk=topk, dh=head_dim)
    return pl.pallas_call(
        body,
        grid=(batch, n_hb),
        in_specs=[
            pl.BlockSpec((seq, hps * head_dim), lambda b, h: (b, h)),
            pl.BlockSpec((seq, hps * head_dim), lambda b, h: (b, n_hb + h)),
            pl.BlockSpec((seq, hps * head_dim), lambda b, h: (b, 2 * n_hb + h)),
        ],
        out_specs=pl.BlockSpec((seq, hps * head_dim), lambda b, h: (b, h)),
        out_shape=jax.ShapeDtypeStruct((T, n_heads * head_dim), BF16),
        scratch_shapes=[pltpu.VMEM((ATTN_LOOKAHEAD + 1, seq, blk), F32)],
        compiler_params=_cparams(("parallel", "parallel")),
        name="moba_attention",
    )(u, u, u)


def _conv_body(x_ref, w_ref, bdw_ref, g_ref, b_ref, o_ref, win_ref, conv_ref,
               *, width, ts, rt, halo, eps):
    sub = SUBLANES
    half = sub * LANES
    ng = conv_ref.shape[0]

    @pl.when(pl.program_id(1) == 0)
    def _zero_halo():
        win_ref[0:halo * sub, :] = jnp.zeros((halo * sub, LANES), U32)

    for s in range(sub):
        lo = x_ref[:, s * LANES:(s + 1) * LANES].astype(F32)
        hi = x_ref[:, half + s * LANES:half + (s + 1) * LANES].astype(F32)
        win_ref[pl.ds(halo * sub + s, ts, stride=sub), :] = _pack_bf16_pair(lo, hi)

    first = halo - (width - 1)

    def chunk(r, carry):
        t0 = r * rt
        acc_lo = jnp.broadcast_to(bdw_ref[0][None], (rt, sub, LANES))
        acc_hi = jnp.broadcast_to(bdw_ref[1][None], (rt, sub, LANES))
        for j0 in range(0, width, CONV_TAP_GROUP):
            part = None
            for j in range(j0, min(j0 + CONV_TAP_GROUP, width)):
                start = pl.multiple_of((t0 + first + j) * sub, sub)
                slab = pltpu.bitcast(win_ref[pl.ds(start, rt * sub), :], BF16)
                prod = slab.reshape(rt, 2 * sub, LANES) * pltpu.bitcast(w_ref[j], BF16)[None]
                part = prod if part is None else part + prod
            u = pltpu.bitcast(part.reshape(rt * 2 * sub, LANES), U32)
            acc_lo = acc_lo + _unpack_lo(u).reshape(rt, sub, LANES)
            acc_hi = acc_hi + _unpack_hi(u).reshape(rt, sub, LANES)
        rows = pl.ds(pl.multiple_of(t0 * sub, sub), rt * sub)
        conv_ref[0, rows, :] = acc_lo.reshape(rt * sub, LANES)
        conv_ref[1, rows, :] = acc_hi.reshape(rt * sub, LANES)
        return carry

    lax.fori_loop(0, ts // rt, chunk, 0)

    win_ref[0:halo * sub, :] = win_ref[ts * sub:(ts + halo) * sub, :]

    def channel_chunks():
        for g in range(ng):
            for s in range(sub):
                yield (g * sub + s) * LANES, conv_ref[g, pl.ds(s, ts, stride=sub), :]

    n = ng * sub * LANES
    total = jnp.zeros((ts, 1), F32)
    for _, y in channel_chunks():
        total = total + jnp.sum(y, axis=1, keepdims=True)
    mu = total * (1.0 / n)
    sq = jnp.zeros((ts, 1), F32)
    for _, y in channel_chunks():
        d = y - mu
        sq = sq + jnp.sum(d * d, axis=1, keepdims=True)
    inv = lax.rsqrt(sq * (1.0 / n) + eps)
    for c0, y in channel_chunks():
        z = (y - mu) * inv * g_ref[:, c0:c0 + LANES] + b_ref[:, c0:c0 + LANES]
        o_ref[:, c0:c0 + LANES] = (z * _sigmoid(z)).astype(o_ref.dtype)


def _conv_branch(u, w_dw, b_dw, ln_g, ln_b, *, batch, seq, conv_c, col_block, ts, rt, halo):
    T = u.shape[0]
    width = w_dw.shape[0]
    half = SUBLANES * LANES
    assert seq % ts == 0 and conv_c == 2 * half and ts % rt == 0
    assert halo >= width - 1 and ts >= halo
    w16 = lax.bitcast_convert_type(w_dw.astype(BF16), jnp.uint16).astype(U32)
    w_pk = (w16[:, :half] | (w16[:, half:] << 16)).reshape(width, SUBLANES, LANES)
    n_s = seq // ts
    body = functools.partial(_conv_body, width=width, ts=ts, rt=rt, halo=halo, eps=LN_EPS)
    return pl.pallas_call(
        body,
        grid=(batch, n_s),
        in_specs=[
            pl.BlockSpec((ts, conv_c), lambda b, s: (b * n_s + s, col_block)),
            pl.BlockSpec((width, SUBLANES, LANES), lambda b, s: (0, 0, 0)),
            pl.BlockSpec((2, SUBLANES, LANES), lambda b, s: (0, 0, 0)),
            pl.BlockSpec((1, conv_c), lambda b, s: (0, 0)),
            pl.BlockSpec((1, conv_c), lambda b, s: (0, 0)),
        ],
        out_specs=pl.BlockSpec((ts, conv_c), lambda b, s: (b * n_s + s, 0)),
        out_shape=jax.ShapeDtypeStruct((T, conv_c), BF16),
        scratch_shapes=[pltpu.VMEM(((halo + ts) * SUBLANES, LANES), U32),
                        pltpu.VMEM((2, ts * SUBLANES, LANES), F32)],
        compiler_params=_cparams(("parallel", "arbitrary")),
        name="conv_branch",
    )(u, w_pk, b_dw.reshape(2, SUBLANES, LANES), ln_g[None, :], ln_b[None, :])


def _post_body(o_ref, hc_ref, ga_ref, gb_ref, x_ref, wo_ref, wp_ref, bp_ref, wout_ref,
               g1_ref, b1_ref, wrh_ref, wrl_ref, br_ref, h_ref, hp_ref, rt_ref, m_scr, z_scr,
               *, alpha, eps, n_groups, e_per_group):
    i = pl.program_id(0)

    @pl.when(i == 0)
    def _no_previous_tiles():
        m_scr[1] = jnp.zeros(m_scr.shape[1:], BF16)
        z_scr[1] = jnp.zeros(z_scr.shape[1:], F32)
        z_scr[2] = jnp.zeros(z_scr.shape[1:], F32)

    ya = jnp.dot(o_ref[...], wo_ref[...], preferred_element_type=F32)

    z = z_scr[(i + 1) % 3]
    mu = jnp.mean(z, axis=1, keepdims=True)
    zc = z - mu
    var = jnp.mean(zc * zc, axis=1, keepdims=True)
    h = zc * lax.rsqrt(var + eps) * g1_ref[...] + b1_ref[...]
    h_ref[...] = h
    half = h.shape[1] // 2
    _store_tile_rows(hp_ref, (), _pack_bf16_pair(h[:, :half], h[:, half:]))

    yc = jnp.dot(hc_ref[...], wp_ref[...], preferred_element_type=F32) + bp_ref[...]

    h_hi = h.astype(BF16)
    h_lo = (h - h_hi.astype(F32)).astype(BF16)
    hi_terms = jnp.dot(h_hi, wrl_ref[...], preferred_element_type=F32)
    logits = (hi_terms[:, :LANES] + hi_terms[:, LANES:]
              + jnp.dot(h_lo, wrh_ref[...], preferred_element_type=F32) + br_ref[...])
    tm = logits.shape[0]
    lane = lax.broadcasted_iota(I32, (tm, LANES), 1)
    neg_inf = jnp.float32(-jnp.inf)
    big = jnp.int32(LANES)

    def first_argmax(vals):
        top = jnp.max(vals, axis=1, keepdims=True)
        idx = jnp.min(jnp.where(vals == top, lane, big), axis=1, keepdims=True)
        return top, idx

    gl = jnp.where(lane < n_groups, logits, neg_inf)
    gmax, grp = first_argmax(gl)
    grp_w = 1.0 / jnp.sum(jnp.exp(gl - gmax), axis=1, keepdims=True)
    lo_lane = n_groups + grp * e_per_group
    el = jnp.where((lane >= lo_lane) & (lane < lo_lane + e_per_group), logits, neg_inf)
    v1, i1 = first_argmax(el)
    v2, i2 = first_argmax(jnp.where(lane == i1, neg_inf, el))
    t = jnp.exp(v2 - v1)
    p1 = 1.0 / (1.0 + t)
    c1 = p1 * grp_w
    c2 = (t * p1) * grp_w
    e1 = (i1 - n_groups).astype(F32)
    e2 = (i2 - n_groups).astype(F32)
    rt_ref[...] = jnp.where(lane == 0, e1, jnp.where(lane == 1, e2,
                            jnp.where(lane == 2, c1, jnp.where(lane == 3, c2, 0.0))))

    prev = (i + 1) % 2
    z_scr[(i + 2) % 3] += jnp.dot(m_scr[prev], wout_ref[...], preferred_element_type=F32)
    m_scr[i % 2] = (ga_ref[...].astype(F32) * ya + gb_ref[...].astype(F32) * yc).astype(BF16)
    z_scr[i % 3] = alpha * x_ref[...]


def _post_block(o, hc, u, x2, wo, wp, bp, wout, g1, b1, wr_hi, wr_lo, br, *, gate_block,
                alpha, n_groups, e_per_group, tm):
    T, D = x2.shape
    A = o.shape[1]
    C = hc.shape[1]
    assert T % tm == 0 and D == 2 * SUBLANES * LANES, "packed rows are stored as one (8, 128) tile"
    assert n_groups * (1 + e_per_group) <= LANES
    const = lambda i: (0, 0)
    resident = lambda shape: pl.BlockSpec(shape, const, pipeline_mode=pl.Buffered(1))
    body = functools.partial(_post_body, alpha=alpha, eps=LN_EPS, n_groups=n_groups,
                             e_per_group=e_per_group)
    n = T // tm
    cur = lambda i: jnp.minimum(i, n - 1)
    prev = lambda i: jnp.maximum(i - 2, 0)
    return pl.pallas_call(
        body,
        grid=(n + 2,),
        in_specs=[
            pl.BlockSpec((tm, A), lambda i: (cur(i), 0)),
            pl.BlockSpec((tm, C), lambda i: (cur(i), 0)),
            pl.BlockSpec((tm, D), lambda i: (cur(i), gate_block)),
            pl.BlockSpec((tm, D), lambda i: (cur(i), gate_block + 1)),
            pl.BlockSpec((tm, D), lambda i: (cur(i), 0)),
            resident((A, D)),
            resident((C, D)),
            resident((1, D)),
            resident((D, D)),
            resident((1, D)),
            resident((1, D)),
            resident((D, LANES)),
            resident((D, 2 * LANES)),
            resident((1, LANES)),
        ],
        out_specs=[
            pl.BlockSpec((tm, D), lambda i: (prev(i), 0)),
            pl.BlockSpec((tm * SUBLANES, LANES), lambda i: (prev(i), 0)),
            pl.BlockSpec((tm, LANES), lambda i: (prev(i), 0)),
        ],
        out_shape=[
            jax.ShapeDtypeStruct((T, D), F32),
            jax.ShapeDtypeStruct((T * SUBLANES, LANES), U32),
            jax.ShapeDtypeStruct((T, LANES), F32),
        ],
        scratch_shapes=[pltpu.VMEM((2, tm, D), BF16), pltpu.VMEM((3, tm, D), F32)],
        compiler_params=_cparams(("arbitrary",)),
        name="merge_project_route",
    )(o, hc, u, u, x2, wo, wp, bp, wout, g1, b1, wr_hi, wr_lo, br)


def _row_copy(src_ref, src_row8, dst_ref, dst_row8, sem):
    aligned = lambda v: v if isinstance(v, int) else pl.multiple_of(v, SUBLANES)
    return pltpu.make_async_copy(src_ref.at[pl.ds(aligned(src_row8), SUBLANES)],
                                 dst_ref.at[pl.ds(aligned(dst_row8), SUBLANES)], sem)


def _moe_body(es_ref, ec_ref, src_ref, dst_ref, hp_ref, w1_ref, w3_ref, w2_ref, y2_ref,
              w1b, w3b, w2b, xbuf, ybuf, src, dst, gsem, ssem, src_sem, dst_sem,
              *, n_experts, nch, bm, spare_row0):
    g = pl.program_id(0)
    c = pl.program_id(1)
    fc = w1_ref.shape[1]
    half = SUBLANES * LANES
    n_total = es_ref[n_experts - 1] + ec_ref[n_experts - 1]
    n_xbuf = xbuf.shape[0]
    ahead = n_xbuf - 1

    def src_copy(q):
        return pltpu.make_async_copy(src_ref.at[q], src.at[q % 2], src_sem.at[q % 2])

    def dst_copy(q):
        return pltpu.make_async_copy(dst_ref.at[q + 1], dst.at[(q + 1) % 2], dst_sem.at[(q + 1) % 2])

    def gather_row(q, r):
        return _row_copy(hp_ref, src[q % 2, 0, r], xbuf.at[q % n_xbuf], r * SUBLANES,
                         gsem.at[q % n_xbuf])

    def scatter_row(q, r):
        return _row_copy(ybuf.at[(q + 3) % 3], r * SUBLANES, y2_ref, dst[(q + 1) % 2, 0, r],
                         ssem.at[(q + 3) % 3])

    def gather_wait(q):
        for r in range(bm):
            _row_copy(hp_ref, 0, xbuf.at[q % n_xbuf], r * SUBLANES, gsem.at[q % n_xbuf]).wait()

    def scatter_wait(q):
        for r in range(bm):
            _row_copy(ybuf.at[(q + 3) % 3], r * SUBLANES, y2_ref, 0, ssem.at[(q + 3) % 3]).wait()

    @pl.when((g == 0) & (c == 0))
    def _prime():
        ybuf[...] = jnp.zeros(ybuf.shape, U32)
        spare = [pltpu.make_async_copy(
            ybuf.at[s], y2_ref.at[pl.ds((spare_row0 + s * bm) * SUBLANES, bm * SUBLANES)], ssem.at[s])
            for s in range(2)]
        for cp in spare:
            cp.start()
        for cp in spare:
            cp.wait()
        dst_copy(-1).start()
        for q in range(ahead):
            src_copy(q).start()
            src_copy(q).wait()
            for r in range(bm):
                gather_row(q, r).start()
        src_copy(ahead).start()

    @pl.when(g < n_experts)
    def _cast_next_expert_chunk():
        slot = g % 2
        w1b[slot, c] = w1_ref[...].astype(BF16)
        w3b[slot, c] = w3_ref[...].astype(BF16)
        w2b[slot, pl.ds(pl.multiple_of(c * fc, fc), fc), :] = w2_ref[...].astype(BF16)

    @pl.when(g >= 1)
    def _compute_previous_expert_share():
        e = g - 1
        slot = e % 2
        nb = ec_ref[e]
        base = es_ref[e]
        n_out = w2b.shape[2] // MOE_OUT_CHUNK

        def block(r, carry):
            q = base + r
            src_copy(q + ahead).wait()
            src_copy(q + ahead + 1).start()
            dst_copy(q - 1).wait()
            dst_copy(q).start()
            gather_wait(q)

            @pl.when(q >= 2)
            def _():
                scatter_wait(q - 3)

            pending = []
            for rr in range(bm):
                pending.append(scatter_row(q - 1, rr))
                pending.append(gather_row(q + ahead, rr))
            per_gap = -(-len(pending) // (4 * nch + n_out))

            def issue_some():
                for _ in range(min(per_gap, len(pending))):
                    pending.pop(0).start()

            u = _load_tile_rows(xbuf, (q % n_xbuf,), bm)
            x_lo = _unpack_lo(u).astype(BF16)
            x_hi = _unpack_hi(u).astype(BF16)
            hs = []
            for cc in range(nch):
                a = jnp.dot(x_lo, w1b[slot, cc, :half, :], preferred_element_type=F32)
                issue_some()
                a = a + jnp.dot(x_hi, w1b[slot, cc, half:, :], preferred_element_type=F32)
                issue_some()
                b = jnp.dot(x_lo, w3b[slot, cc, :half, :], preferred_element_type=F32)
                issue_some()
                b = b + jnp.dot(x_hi, w3b[slot, cc, half:, :], preferred_element_type=F32)
                issue_some()
                hs.append((a * _sigmoid(a) * b).astype(BF16))
            hmid = jnp.concatenate(hs, axis=1)
            ys = []
            for oc in range(n_out):
                cols = slice(oc * MOE_OUT_CHUNK, (oc + 1) * MOE_OUT_CHUNK)
                ys.append(jnp.dot(hmid, w2b[slot, :, cols], preferred_element_type=F32))
                issue_some()
            assert not pending
            y = jnp.concatenate(ys, axis=1)
            _store_tile_rows(ybuf, (q % 3,), _pack_bf16_pair(y[:, :half], y[:, half:]))
            return carry

        lax.fori_loop((nb * c) // nch, (nb * (c + 1)) // nch, block, 0)

    @pl.when((g == n_experts) & (c == nch - 1))
    def _drain():
        dst_copy(n_total - 1).wait()
        for r in range(bm):
            scatter_row(n_total - 1, r).start()

        @pl.when(n_total >= 2)
        def _():
            scatter_wait(n_total - 3)

        @pl.when(n_total >= 1)
        def _():
            scatter_wait(n_total - 2)

        scatter_wait(n_total - 1)
        for k in range(ahead):
            gather_wait(n_total + k)
        src_copy(n_total + ahead).wait()


def _moe_experts(eb_start, eb_count, src_plan, dst_plan, hp, w1, w3, w2, *, n_tokens, bm, nch):
    n_out_rows = 2 * n_tokens + 2 * bm
    E, D, F = w1.shape
    assert D == 2 * SUBLANES * LANES and hp.shape[1] == LANES and D % MOE_OUT_CHUNK == 0
    assert F % nch == 0 and src_plan.shape[1:] == (1, bm) and dst_plan.shape[1:] == (1, bm)
    assert dst_plan.shape[0] == src_plan.shape[0] + 1
    fc = F // nch

    def w_in_idx(g, c, es, ec):
        return (jnp.minimum(g, E - 1), 0, jnp.where(g < E, c, nch - 1))

    def w_out_idx(g, c, es, ec):
        return (jnp.minimum(g, E - 1), jnp.where(g < E, c, nch - 1), 0)

    grid_spec = pltpu.PrefetchScalarGridSpec(
        num_scalar_prefetch=2,
        grid=(E + 1, nch),
        in_specs=[
            pl.BlockSpec(memory_space=pl.ANY),
            pl.BlockSpec(memory_space=pl.ANY),
            pl.BlockSpec(memory_space=pl.ANY),
            pl.BlockSpec((None, D, fc), w_in_idx),
            pl.BlockSpec((None, D, fc), w_in_idx),
            pl.BlockSpec((None, fc, D), w_out_idx),
        ],
        out_specs=pl.BlockSpec(memory_space=pl.ANY),
        scratch_shapes=[
            pltpu.VMEM((2, nch, D, fc), BF16),
            pltpu.VMEM((2, nch, D, fc), BF16),
            pltpu.VMEM((2, F, D), BF16),
            pltpu.VMEM((MOE_GATHER_AHEAD + 1, bm * SUBLANES, LANES), U32),
            pltpu.VMEM((3, bm * SUBLANES, LANES), U32),
            pltpu.SMEM((2, 1, bm), I32),
            pltpu.SMEM((2, 1, bm), I32),
            pltpu.SemaphoreType.DMA((MOE_GATHER_AHEAD + 1,)),
            pltpu.SemaphoreType.DMA((3,)),
            pltpu.SemaphoreType.DMA((2,)),
            pltpu.SemaphoreType.DMA((2,)),
        ],
    )
    body = functools.partial(_moe_body, n_experts=E, nch=nch, bm=bm, spare_row0=2 * n_tokens)
    return pl.pallas_call(
        body,
        grid_spec=grid_spec,
        out_shape=jax.ShapeDtypeStruct((n_out_rows * SUBLANES, LANES), U32),
        compiler_params=_cparams(("arbitrary", "arbitrary")),
        name="moe_experts",
    )(eb_start, eb_count, src_plan, dst_plan, hp, w1, w3, w2)


def _combine_body(h_ref, rt_ref, g_ref, b_ref, y0_ref, y1_ref, o_ref, *, alpha, eps):
    tt = h_ref.shape[0]
    u0 = _load_tile_rows(y0_ref, (), tt)
    u1 = _load_tile_rows(y1_ref, (), tt)
    half = u0.shape[1]
    c0 = rt_ref[:, 2:3]
    c1 = rt_ref[:, 3:4]
    z_lo = alpha * h_ref[:, :half] + (_unpack_lo(u0) * c0 + _unpack_lo(u1) * c1)
    z_hi = alpha * h_ref[:, half:] + (_unpack_hi(u0) * c0 + _unpack_hi(u1) * c1)
    n = 2 * half
    mu = (jnp.sum(z_lo, axis=1, keepdims=True) + jnp.sum(z_hi, axis=1, keepdims=True)) * (1.0 / n)
    d_lo = z_lo - mu
    d_hi = z_hi - mu
    var = (jnp.sum(d_lo * d_lo, axis=1, keepdims=True)
           + jnp.sum(d_hi * d_hi, axis=1, keepdims=True)) * (1.0 / n)
    inv = lax.rsqrt(var + eps)
    o_ref[:, :half] = d_lo * inv * g_ref[:, :half] + b_ref[:, :half]
    o_ref[:, half:] = d_hi * inv * g_ref[:, half:] + b_ref[:, half:]


def _combine(h, rt, y2, g2, b2, *, alpha, tt):
    T, D = h.shape
    assert T % tt == 0 and D == 2 * SUBLANES * LANES and y2.shape[1] == LANES
    body = functools.partial(_combine_body, alpha=alpha, eps=LN_EPS)
    return pl.pallas_call(
        body,
        grid=(T // tt,),
        in_specs=[
            pl.BlockSpec((tt, D), lambda i: (i, 0)),
            pl.BlockSpec((tt, LANES), lambda i: (i, 0)),
            pl.BlockSpec((1, D), lambda i: (0, 0)),
            pl.BlockSpec((1, D), lambda i: (0, 0)),
            pl.BlockSpec((tt * SUBLANES, LANES), lambda i: (i, 0)),
            pl.BlockSpec((tt * SUBLANES, LANES), lambda i: (i + T // tt, 0)),
        ],
        out_specs=pl.BlockSpec((tt, D), lambda i: (i, 0)),
        out_shape=jax.ShapeDtypeStruct((T, D), F32),
        compiler_params=_cparams(("parallel",)),
        name="moe_combine",
    )(h, rt, g2, b2, y2, y2)


def _slot_plan(expert, n_experts, bm, n_plan_blocks):
    T = expert.shape[0]
    e_flat = expert.reshape(-1)
    onehot = (e_flat[:, None] == jnp.arange(n_experts, dtype=I32)[None, :]).astype(I32)
    csum = jnp.cumsum(onehot, axis=0)
    counts = csum[-1]
    eb_count = (counts + bm - 1) // bm
    eb_start = jnp.cumsum(eb_count) - eb_count
    slot = jnp.sum(onehot * (csum - 1 + (eb_start * bm)[None, :]), axis=1)
    n_slots = n_plan_blocks * bm
    tok = jnp.arange(2 * T, dtype=I32) // 2
    k = jnp.arange(2 * T, dtype=I32) % 2
    p = jnp.arange(n_slots, dtype=I32)
    spare = 2 * T + ((p // bm) % 2) * bm + p % bm
    dst = spare.at[slot].set(k * T + tok, unique_indices=True)
    src = jnp.where(dst < 2 * T, dst % T, 0)
    dst = jnp.concatenate([2 * T + bm + jnp.arange(bm, dtype=I32), dst])
    to_blocks = lambda a: (a * SUBLANES).reshape(-1, 1, bm)
    return eb_start.astype(I32), eb_count.astype(I32), to_blocks(src), to_blocks(dst)


def _layer(x2, p, *, batch, seq, depth, n_heads, head_dim, moba_block, moba_topk,
           n_groups, e_per_group, tiles):
    T, D = x2.shape
    A = n_heads * head_dim
    C = p["w_dw"].shape[1]
    alpha = (2.0 * depth) ** 0.25
    tn = tiles["proj_tn"]
    assert A == C == D, "column-block addressing below assumes equal branch widths"

    w_in, b_in = p["w_in"], p["b_in"]

    half = head_dim // 2
    inv_freq = jnp.power(ROPE_THETA, -jnp.arange(half, dtype=F32) * (2.0 / head_dim))
    ang = jnp.arange(seq, dtype=F32)[:, None] * inv_freq[None, :]
    cos2 = jnp.concatenate([jnp.cos(ang), jnp.cos(ang)], axis=1)
    sin_s = jnp.concatenate([-jnp.sin(ang), jnp.sin(ang)], axis=1)

    u = _in_projection(x2, w_in.astype(BF16), b_in[None, :], cos2, sin_s, seq=seq, attn_w=A, conv_c=C,
                       head_dim=head_dim, tm=tiles["proj_tm"], tn=tn)
    o = _moba_attention(u, batch=batch, seq=seq, n_heads=n_heads, head_dim=head_dim,
                        blk=moba_block, topk=moba_topk, hps=ATTN_HPS)
    hc = _conv_branch(u, p["w_dw"], p["b_dw"], p["conv_ln_g"], p["conv_ln_b"], batch=batch,
                      seq=seq, conv_c=C, col_block=3, ts=tiles["conv_ts"], rt=tiles["conv_rt"],
                      halo=CONV_HALO)

    n_experts = n_groups * e_per_group
    n_route = n_groups + n_experts
    w_route = jnp.concatenate(
        [p["w_rg"], p["w_re"].transpose(1, 0, 2).reshape(D, n_experts),
         jnp.zeros((D, LANES - n_route), F32)], axis=1)
    b_route = jnp.concatenate(
        [p["b_rg"], p["b_re"].reshape(n_experts), jnp.zeros((LANES - n_route,), F32)])[None, :]
    wr_hi = w_route.astype(BF16)
    wr_lo = jnp.concatenate([wr_hi, (w_route - wr_hi.astype(F32)).astype(BF16)], axis=1)

    h, hp, rt = _post_block(
        o, hc, u, x2, p["w_o_attn"].astype(BF16), p["w_pw2"].astype(BF16), p["b_pw2"][None, :],
        p["w_out"].astype(BF16), p["ln1_g"][None, :], p["ln1_b"][None, :], wr_hi, wr_lo, b_route,
        gate_block=4, alpha=alpha, n_groups=n_groups, e_per_group=e_per_group,
        tm=tiles["post_tm"])

    bm = tiles["moe_bm"]
    n_plan_blocks = -(-2 * T // bm) + n_experts + MOE_GATHER_AHEAD + 1
    expert = rt[:, :2].astype(I32)
    eb_start, eb_count, src_plan, dst_plan = _slot_plan(expert, n_experts, bm, n_plan_blocks)
    y2 = _moe_experts(eb_start, eb_count, src_plan, dst_plan, hp, p["w1"], p["w3"], p["w2"],
                      n_tokens=T, bm=bm, nch=tiles["moe_nch"])
    return _combine(h, rt, y2, p["ln2_g"][None, :], p["ln2_b"][None, :], alpha=alpha,
                    tt=tiles["comb_tt"])


_PARAM_NAMES = ("w_in", "b_in", "w_o_attn", "w_dw", "b_dw", "conv_ln_g", "conv_ln_b", "w_pw2",
                "b_pw2", "w_out", "ln1_g", "ln1_b", "w_rg", "b_rg", "w_re", "b_re", "w1", "w3",
                "w2", "ln2_g", "ln2_b")

_TILES = dict(proj_tm=PROJ_TM, proj_tn=PROJ_TN, conv_ts=CONV_TS, conv_rt=CONV_RT,
              post_tm=POST_TM, comb_tt=COMB_TT, moe_bm=MOE_BM, moe_nch=MOE_NCH)


def _forward(x, params, *, n_heads=N_HEADS, head_dim=HEAD_DIM, moba_block=MOBA_BLOCK,
             moba_topk=MOBA_TOPK, n_groups=N_GROUPS, e_per_group=EXPERTS_PER_GROUP, tiles=None):
    tiles = dict(_TILES, **(tiles or {}))
    B, S, D = x.shape
    depth = params["w_in"].shape[0]
    x2 = x.reshape(B * S, D)
    for l in range(depth):
        p = {k: v[l] for k, v in params.items()}
        x2 = _layer(x2, p, batch=B, seq=S, depth=depth, n_heads=n_heads, head_dim=head_dim,
                    moba_block=moba_block, moba_topk=moba_topk, n_groups=n_groups,
                    e_per_group=e_per_group, tiles=tiles)
    return x2.reshape(B, S, D)


def kernel(x, w_in, b_in, w_o_attn, w_dw, b_dw, conv_ln_g, conv_ln_b, w_pw2, b_pw2, w_out, ln1_g,
           ln1_b, w_rg, b_rg, w_re, b_re, w1, w3, w2, ln2_g, ln2_b):
    params = dict(zip(_PARAM_NAMES, (w_in, b_in, w_o_attn, w_dw, b_dw, conv_ln_g, conv_ln_b, w_pw2,
                                     b_pw2, w_out, ln1_g, ln1_b, w_rg, b_rg, w_re, b_re, w1, w3,
                                     w2, ln2_g, ln2_b)))
    return _forward(x, params)
```

```python
import functools

import jax
import jax.numpy as jnp
from jax import lax
from jax.experimental import pallas as pl
from jax.experimental.pallas import tpu as pltpu

F32 = jnp.float32
BF16 = jnp.bfloat16
U32 = jnp.uint32
I32 = jnp.int32

N_HEADS = 16
HEAD_DIM = 128
ROPE_THETA = 10000.0
MOBA_BLOCK = 256
MOBA_TOPK = 3
CONV_WIDTH = 31
N_GROUPS = 4
EXPERTS_PER_GROUP = 8
LN_EPS = 1e-5
LOG2_E = 1.4426950408889634

LANES = 128
SUBLANES = 8
VMEM_LIMIT = 56 * 1024 * 1024

PROJ_TM = 1024
PROJ_TN = 1024
ATTN_HPS = 2
ATTN_LOOKAHEAD = 2
ATTN_ONES_ROWS = 16
CONV_TS = 256
CONV_RT = 16
CONV_HALO = 32
CONV_TAP_GROUP = 4
POST_TM = 256
COMB_TT = 512
MOE_BM = 256
MOE_NCH = 4
MOE_OUT_CHUNK = 256
MOE_GATHER_AHEAD = 2


def _cparams(sem):
    return pltpu.CompilerParams(dimension_semantics=sem, vmem_limit_bytes=VMEM_LIMIT)


def _sigmoid(x):
    return 1.0 / (1.0 + jnp.exp(-x))


def _pack_bf16_pair(lo_f32, hi_f32):
    lo = lax.bitcast_convert_type(lo_f32.astype(BF16).astype(F32), U32) >> 16
    hi = lax.bitcast_convert_type(hi_f32.astype(BF16).astype(F32), U32)
    return hi | lo


def _unpack_lo(u):
    return lax.bitcast_convert_type(u << 16, F32)


def _unpack_hi(u):
    return lax.bitcast_convert_type(u & jnp.uint32(0xFFFF0000), F32)


def _store_tile_rows(ref, lead, x):
    n = x.shape[0]
    for s in range(SUBLANES):
        ref[(*lead, pl.ds(s, n, stride=SUBLANES), slice(None))] = x[:, s * LANES:(s + 1) * LANES]


def _load_tile_rows(ref, lead, n):
    return jnp.concatenate(
        [ref[(*lead, pl.ds(s, n, stride=SUBLANES), slice(None))] for s in range(SUBLANES)], axis=1)


def _inproj_body(x_ref, w_ref, wg_ref, b_ref, bg_ref, cos_ref, sin_ref, o_ref, xb_ref,
                 *, n_q, n_qk, n_qkv, n_glu, scale, head_dim):
    j = pl.program_id(1)
    tn = o_ref.shape[1]

    @pl.when(j == 0)
    def _cast():
        xb_ref[...] = x_ref[...].astype(BF16)

    def project(w, b):
        return jnp.dot(xb_ref[...], w[...], preferred_element_type=F32) + b[...]

    @pl.when(j < n_qk)
    def _rope():
        acc = project(w_ref, b_ref)
        s = jnp.where(j < n_q, scale, 1.0).astype(F32)
        cos = cos_ref[...] * s
        sin = sin_ref[...] * s
        for h in range(tn // head_dim):
            t = acc[:, h * head_dim:(h + 1) * head_dim]
            r = pltpu.roll(t, head_dim // 2, axis=1)
            o_ref[:, h * head_dim:(h + 1) * head_dim] = (t * cos + r * sin).astype(o_ref.dtype)

    @pl.when((j >= n_qk) & (j < n_qkv))
    def _plain():
        o_ref[...] = project(w_ref, b_ref).astype(o_ref.dtype)

    @pl.when((j >= n_qkv) & (j < n_qkv + n_glu))
    def _glu():
        a = project(w_ref, b_ref)
        g = project(wg_ref, bg_ref)
        o_ref[...] = (a * _sigmoid(g)).astype(o_ref.dtype)

    @pl.when(j >= n_qkv + n_glu)
    def _gate():
        o_ref[...] = _sigmoid(project(w_ref, b_ref)).astype(o_ref.dtype)


def _in_projection(x2, w_all, b_all, cos2, sin_s, *, seq, attn_w, conv_c, head_dim, tm, tn):
    T, D = x2.shape
    W = w_all.shape[1] - conv_c
    assert T % tm == 0 and seq % tm == 0 and W % tn == 0
    assert attn_w % tn == 0 and conv_c % tn == 0 and tn % head_dim == 0
    assert head_dim == LANES
    n_q = attn_w // tn
    n_glu = conv_c // tn
    n_qkv = 3 * n_q
    pos_tiles = seq // tm

    def main_idx(i, j):
        return (0, jnp.where(j < n_qkv + n_glu, j, j + n_glu))

    def g_idx(i, j):
        return (0, n_qkv + n_glu + jnp.clip(j - n_qkv, 0, n_glu - 1))

    body = functools.partial(_inproj_body, n_q=n_q, n_qk=2 * n_q, n_qkv=n_qkv, n_glu=n_glu,
                             scale=float(head_dim) ** -0.5 * LOG2_E, head_dim=head_dim)
    return pl.pallas_call(
        body,
        grid=(T // tm, W // tn),
        in_specs=[
            pl.BlockSpec((tm, D), lambda i, j: (i, 0)),
            pl.BlockSpec((D, tn), main_idx),
            pl.BlockSpec((D, tn), g_idx),
            pl.BlockSpec((1, tn), main_idx),
            pl.BlockSpec((1, tn), g_idx),
            pl.BlockSpec((tm, head_dim), lambda i, j: (i % pos_tiles, 0)),
            pl.BlockSpec((tm, head_dim), lambda i, j: (i % pos_tiles, 0)),
        ],
        out_specs=pl.BlockSpec((tm, tn), lambda i, j: (i, j)),
        out_shape=jax.ShapeDtypeStruct((T, W), BF16),
        scratch_shapes=[pltpu.VMEM((tm, D), BF16)],
        compiler_params=_cparams(("parallel", "arbitrary")),
        name="in_projection",
    )(x2, w_all, w_all, b_all, b_all, cos2, sin_s)


def _attn_prep(q, k, v, *, seq, blk):
    nb = seq // blk
    nbp = -(-nb // SUBLANES) * SUBLANES
    dh = q.shape[1]
    contract_last = (((1,), (1,)), ((), ()))
    kmean = jnp.mean(k.astype(F32).reshape(nb, blk, dh), axis=1)
    if nbp > nb:
        kmean = jnp.concatenate([kmean, jnp.zeros((nbp - nb, dh), F32)], axis=0)
    gate_t = lax.dot_general(kmean.astype(BF16), q, contract_last, preferred_element_type=F32)
    ones_rows = (lax.broadcasted_iota(I32, (ATTN_ONES_ROWS, seq), 0) == 0).astype(BF16)
    vt = jnp.concatenate([v.astype(F32).T.astype(BF16), ones_rows], axis=0)
    return gate_t, vt


def _attn_scores(q, k, gate_t, i, s_ref, *, blk, topk, nbp):
    contract_last = (((1,), (1,)), ((), ()))
    neg_inf = jnp.float32(-jnp.inf)
    sub = lax.broadcasted_iota(I32, (nbp, blk), 0)
    qi = q[i * blk:(i + 1) * blk]
    nk = (i + 1) * blk
    st = lax.dot_general(k[:nk], qi, contract_last, preferred_element_type=F32)
    if i > topk:
        gm = jnp.where(sub < i, gate_t[:, i * blk:(i + 1) * blk], neg_inf)
        rank = jnp.zeros((nbp, blk), I32)
        for other in range(i):
            g_other = gm[other:other + 1, :]
            beats = (g_other > gm) | ((g_other == gm) & (sub > other))
            rank = rank + beats.astype(I32)
        bias_t = jnp.where((sub < i) & (rank < topk), 0.0, neg_inf).astype(F32)
    m = None
    for n in range(i + 1):
        t = st[n * blk:(n + 1) * blk]
        if n == i:
            key_r = lax.broadcasted_iota(I32, (blk, blk), 0)
            qry_c = lax.broadcasted_iota(I32, (blk, blk), 1)
            t = jnp.where(key_r <= qry_c, t, neg_inf)
        elif i > topk:
            t = t + bias_t[n:n + 1, :]
        s_ref[n * blk:(n + 1) * blk, :] = t
        t_max = jnp.max(t, axis=0, keepdims=True)
        m = t_max if m is None else jnp.maximum(m, t_max)
    return m


def _attn_output(vt, m, i, s_ref, *, blk, dh):
    nk = (i + 1) * blk
    p = jnp.exp2(s_ref[:nk, :] - m).astype(BF16)
    ot = jnp.dot(vt[:, :nk], p, preferred_element_type=F32)
    return (ot[:dh] * (1.0 / ot[dh:dh + 1])).T


def _attn_body(q_ref, k_ref, v_ref, o_ref, s_scr, *, seq, blk, topk, dh):
    nb = seq // blk
    nbp = -(-nb // SUBLANES) * SUBLANES
    heads = []
    for hh in range(q_ref.shape[1] // dh):
        c0 = hh * dh
        q, k, v = q_ref[:, c0:c0 + dh], k_ref[:, c0:c0 + dh], v_ref[:, c0:c0 + dh]
        heads.append((c0, q, k) + _attn_prep(q, k, v, seq=seq, blk=blk))
    units = [(h, i) for h in range(len(heads)) for i in range(nb)]

    n_buf = s_scr.shape[0]
    ahead = n_buf - 1

    def scores(u):
        h, i = units[u]
        _, q, k, gate_t, _ = heads[h]
        return _attn_scores(q, k, gate_t, i, s_scr.at[u % n_buf], blk=blk, topk=topk, nbp=nbp)

    col_max = {u: scores(u) for u in range(min(ahead, len(units)))}
    for u, (h, i) in enumerate(units):
        if u + ahead < len(units):
            col_max[u + ahead] = scores(u + ahead)
        c0, _, _, _, vt = heads[h]
        o = _attn_output(vt, col_max.pop(u), i, s_scr.at[u % n_buf], blk=blk, dh=dh)
        o_ref[i * blk:(i + 1) * blk, c0:c0 + dh] = o.astype(o_ref.dtype)


def _moba_attention(u, *, batch, seq, n_heads, head_dim, blk, topk, hps):
    T = u.shape[0]
    assert seq % blk == 0 and blk % LANES == 0 and n_heads % hps == 0
    n_hb = n_heads // hps
    body = functools.partial(_attn_body, seq=seq, blk=blk, topk=topk, dh=head_dim)
    return pl.pallas_call(
        body,
        grid=(batch, n_hb),
        in_specs=[
            pl.BlockSpec((seq, hps * head_dim), lambda b, h: (b, h)),
            pl.BlockSpec((seq, hps * head_dim), lambda b, h: (b, n_hb + h)),
            pl.BlockSpec((seq, hps * head_dim), lambda b, h: (b, 2 * n_hb + h)),
        ],
        out_specs=pl.BlockSpec((seq, hps * head_dim), lambda b, h: (b, h)),
        out_shape=jax.ShapeDtypeStruct((T, n_heads * head_dim), BF16),
        scratch_shapes=[pltpu.VMEM((ATTN_LOOKAHEAD + 1, seq, blk), F32)],
        compiler_params=_cparams(("parallel", "parallel")),
        name="moba_attention",
    )(u, u, u)


def _conv_body(x_ref, w_ref, bdw_ref, g_ref, b_ref, o_ref, win_ref, conv_ref,
               *, width, ts, rt, halo, eps):
    sub = SUBLANES
    half = sub * LANES
    ng = conv_ref.shape[0]

    @pl.when(pl.program_id(1) == 0)
    def _zero_halo():
        win_ref[0:halo * sub, :] = jnp.zeros((halo * sub, LANES), U32)

    for s in range(sub):
        lo = x_ref[:, s * LANES:(s + 1) * LANES].astype(F32)
        hi = x_ref[:, half + s * LANES:half + (s + 1) * LANES].astype(F32)
        win_ref[pl.ds(halo * sub + s, ts, stride=sub), :] = _pack_bf16_pair(lo, hi)

    first = halo - (width - 1)

    def chunk(r, carry):
        t0 = r * rt
        acc_lo = jnp.broadcast_to(bdw_ref[0][None], (rt, sub, LANES))
        acc_hi = jnp.broadcast_to(bdw_ref[1][None], (rt, sub, LANES))
        for j0 in range(0, width, CONV_TAP_GROUP):
            part = None
            for j in range(j0, min(j0 + CONV_TAP_GROUP, width)):
                start = pl.multiple_of((t0 + first + j) * sub, sub)
                slab = pltpu.bitcast(win_ref[pl.ds(start, rt * sub), :], BF16)
                prod = slab.reshape(rt, 2 * sub, LANES) * pltpu.bitcast(w_ref[j], BF16)[None]
                part = prod if part is None else part + prod
            u = pltpu.bitcast(part.reshape(rt * 2 * sub, LANES), U32)
            acc_lo = acc_lo + _unpack_lo(u).reshape(rt, sub, LANES)
            acc_hi = acc_hi + _unpack_hi(u).reshape(rt, sub, LANES)
        rows = pl.ds(pl.multiple_of(t0 * sub, sub), rt * sub)
        conv_ref[0, rows, :] = acc_lo.reshape(rt * sub, LANES)
        conv_ref[1, rows, :] = acc_hi.reshape(rt * sub, LANES)
        return carry

    lax.fori_loop(0, ts // rt, chunk, 0)

    win_ref[0:halo * sub, :] = win_ref[ts * sub:(ts + halo) * sub, :]

    def channel_chunks():
        for g in range(ng):
            for s in range(sub):
                yield (g * sub + s) * LANES, conv_ref[g, pl.ds(s, ts, stride=sub), :]

    n = ng * sub * LANES
    total = jnp.zeros((ts, 1), F32)
    for _, y in channel_chunks():
        total = total + jnp.sum(y, axis=1, keepdims=True)
    mu = total * (1.0 / n)
    sq = jnp.zeros((ts, 1), F32)
    for _, y in channel_chunks():
        d = y - mu
        sq = sq + jnp.sum(d * d, axis=1, keepdims=True)
    inv = lax.rsqrt(sq * (1.0 / n) + eps)
    for c0, y in channel_chunks():
        z = (y - mu) * inv * g_ref[:, c0:c0 + LANES] + b_ref[:, c0:c0 + LANES]
        o_ref[:, c0:c0 + LANES] = (z * _sigmoid(z)).astype(o_ref.dtype)


def _conv_branch(u, w_dw, b_dw, ln_g, ln_b, *, batch, seq, conv_c, col_block, ts, rt, halo):
    T = u.shape[0]
    width = w_dw.shape[0]
    half = SUBLANES * LANES
    assert seq % ts == 0 and conv_c == 2 * half and ts % rt == 0
    assert halo >= width - 1 and ts >= halo
    w16 = lax.bitcast_convert_type(w_dw.astype(BF16), jnp.uint16).astype(U32)
    w_pk = (w16[:, :half] | (w16[:, half:] << 16)).reshape(width, SUBLANES, LANES)
    n_s = seq // ts
    body = functools.partial(_conv_body, width=width, ts=ts, rt=rt, halo=halo, eps=LN_EPS)
    return pl.pallas_call(
        body,
        grid=(batch, n_s),
        in_specs=[
            pl.BlockSpec((ts, conv_c), lambda b, s: (b * n_s + s, col_block)),
            pl.BlockSpec((width, SUBLANES, LANES), lambda b, s: (0, 0, 0)),
            pl.BlockSpec((2, SUBLANES, LANES), lambda b, s: (0, 0, 0)),
            pl.BlockSpec((1, conv_c), lambda b, s: (0, 0)),
            pl.BlockSpec((1, conv_c), lambda b, s: (0, 0)),
        ],
        out_specs=pl.BlockSpec((ts, conv_c), lambda b, s: (b * n_s + s, 0)),
        out_shape=jax.ShapeDtypeStruct((T, conv_c), BF16),
        scratch_shapes=[pltpu.VMEM(((halo + ts) * SUBLANES, LANES), U32),
                        pltpu.VMEM((2, ts * SUBLANES, LANES), F32)],
        compiler_params=_cparams(("parallel", "arbitrary")),
        name="conv_branch",
    )(u, w_pk, b_dw.reshape(2, SUBLANES, LANES), ln_g[None, :], ln_b[None, :])


def _post_body(o_ref, hc_ref, ga_ref, gb_ref, x_ref, wo_ref, wp_ref, bp_ref, wout_ref,
               g1_ref, b1_ref, wrh_ref, wrl_ref, br_ref, h_ref, hp_ref, rt_ref, m_scr, z_scr,
               *, alpha, eps, n_groups, e_per_group):
    i = pl.program_id(0)

    @pl.when(i == 0)
    def _no_previous_tiles():
        m_scr[1] = jnp.zeros(m_scr.shape[1:], BF16)
        z_scr[1] = jnp.zeros(z_scr.shape[1:], F32)
        z_scr[2] = jnp.zeros(z_scr.shape[1:], F32)

    ya = jnp.dot(o_ref[...], wo_ref[...], preferred_element_type=F32)

    z = z_scr[(i + 1) % 3]
    mu = jnp.mean(z, axis=1, keepdims=True)
    zc = z - mu
    var = jnp.mean(zc * zc, axis=1, keepdims=True)
    h = zc * lax.rsqrt(var + eps) * g1_ref[...] + b1_ref[...]
    h_ref[...] = h
    half = h.shape[1] // 2
    _store_tile_rows(hp_ref, (), _pack_bf16_pair(h[:, :half], h[:, half:]))

    yc = jnp.dot(hc_ref[...], wp_ref[...], preferred_element_type=F32) + bp_ref[...]

    h_hi = h.astype(BF16)
    h_lo = (h - h_hi.astype(F32)).astype(BF16)
    hi_terms = jnp.dot(h_hi, wrl_ref[...], preferred_element_type=F32)
    logits = (hi_terms[:, :LANES] + hi_terms[:, LANES:]
              + jnp.dot(h_lo, wrh_ref[...], preferred_element_type=F32) + br_ref[...])
    tm = logits.shape[0]
    lane = lax.broadcasted_iota(I32, (tm, LANES), 1)
    neg_inf = jnp.float32(-jnp.inf)
    big = jnp.int32(LANES)

    def first_argmax(vals):
        top = jnp.max(vals, axis=1, keepdims=True)
        idx = jnp.min(jnp.where(vals == top, lane, big), axis=1, keepdims=True)
        return top, idx

    gl = jnp.where(lane < n_groups, logits, neg_inf)
    gmax, grp = first_argmax(gl)
    grp_w = 1.0 / jnp.sum(jnp.exp(gl - gmax), axis=1, keepdims=True)
    lo_lane = n_groups + grp * e_per_group
    el = jnp.where((lane >= lo_lane) & (lane < lo_lane + e_per_group), logits, neg_inf)
    v1, i1 = first_argmax(el)
    v2, i2 = first_argmax(jnp.where(lane == i1, neg_inf, el))
    t = jnp.exp(v2 - v1)
    p1 = 1.0 / (1.0 + t)
    c1 = p1 * grp_w
    c2 = (t * p1) * grp_w
    e1 = (i1 - n_groups).astype(F32)
    e2 = (i2 - n_groups).astype(F32)
    rt_ref[...] = jnp.where(lane == 0, e1, jnp.where(lane == 1, e2,
                            jnp.where(lane == 2, c1, jnp.where(lane == 3, c2, 0.0))))

    prev = (i + 1) % 2
    z_scr[(i + 2) % 3] += jnp.dot(m_scr[prev], wout_ref[...], preferred_element_type=F32)
    m_scr[i % 2] = (ga_ref[...].astype(F32) * ya + gb_ref[...].astype(F32) * yc).astype(BF16)
    z_scr[i % 3] = alpha * x_ref[...]


def _post_block(o, hc, u, x2, wo, wp, bp, wout, g1, b1, wr_hi, wr_lo, br, *, gate_block,
                alpha, n_groups, e_per_group, tm):
    T, D = x2.shape
    A = o.shape[1]
    C = hc.shape[1]
    assert T % tm == 0 and D == 2 * SUBLANES * LANES, "packed rows are stored as one (8, 128) tile"
    assert n_groups * (1 + e_per_group) <= LANES
    const = lambda i: (0, 0)
    resident = lambda shape: pl.BlockSpec(shape, const, pipeline_mode=pl.Buffered(1))
    body = functools.partial(_post_body, alpha=alpha, eps=LN_EPS, n_groups=n_groups,
                             e_per_group=e_per_group)
    n = T // tm
    cur = lambda i: jnp.minimum(i, n - 1)
    prev = lambda i: jnp.maximum(i - 2, 0)
    return pl.pallas_call(
        body,
        grid=(n + 2,),
        in_specs=[
            pl.BlockSpec((tm, A), lambda i: (cur(i), 0)),
            pl.BlockSpec((tm, C), lambda i: (cur(i), 0)),
            pl.BlockSpec((tm, D), lambda i: (cur(i), gate_block)),
            pl.BlockSpec((tm, D), lambda i: (cur(i), gate_block + 1)),
            pl.BlockSpec((tm, D), lambda i: (cur(i), 0)),
            resident((A, D)),
            resident((C, D)),
            resident((1, D)),
            resident((D, D)),
            resident((1, D)),
            resident((1, D)),
            resident((D, LANES)),
            resident((D, 2 * LANES)),
            resident((1, LANES)),
        ],
        out_specs=[
            pl.BlockSpec((tm, D), lambda i: (prev(i), 0)),
            pl.BlockSpec((tm * SUBLANES, LANES), lambda i: (prev(i), 0)),
            pl.BlockSpec((tm, LANES), lambda i: (prev(i), 0)),
        ],
        out_shape=[
            jax.ShapeDtypeStruct((T, D), F32),
            jax.ShapeDtypeStruct((T * SUBLANES, LANES), U32),
            jax.ShapeDtypeStruct((T, LANES), F32),
        ],
        scratch_shapes=[pltpu.VMEM((2, tm, D), BF16), pltpu.VMEM((3, tm, D), F32)],
        compiler_params=_cparams(("arbitrary",)),
        name="merge_project_route",
    )(o, hc, u, u, x2, wo, wp, bp, wout, g1, b1, wr_hi, wr_lo, br)


def _row_copy(src_ref, src_row8, dst_ref, dst_row8, sem):
    aligned = lambda v: v if isinstance(v, int) else pl.multiple_of(v, SUBLANES)
    return pltpu.make_async_copy(src_ref.at[pl.ds(aligned(src_row8), SUBLANES)],
                                 dst_ref.at[pl.ds(aligned(dst_row8), SUBLANES)], sem)


def _moe_body(es_ref, ec_ref, src_ref, dst_ref, hp_ref, w1_ref, w3_ref, w2_ref, y2_ref,
              w1b, w3b, w2b, xbuf, ybuf, src, dst, gsem, ssem, src_sem, dst_sem,
              *, n_experts, nch, bm, spare_row0):
    g = pl.program_id(0)
    c = pl.program_id(1)
    fc = w1_ref.shape[1]
    half = SUBLANES * LANES
    n_total = es_ref[n_experts - 1] + ec_ref[n_experts - 1]
    n_xbuf = xbuf.shape[0]
    ahead = n_xbuf - 1

    def src_copy(q):
        return pltpu.make_async_copy(src_ref.at[q], src.at[q % 2], src_sem.at[q % 2])

    def dst_copy(q):
        return pltpu.make_async_copy(dst_ref.at[q + 1], dst.at[(q + 1) % 2], dst_sem.at[(q + 1) % 2])

    def gather_row(q, r):
        return _row_copy(hp_ref, src[q % 2, 0, r], xbuf.at[q % n_xbuf], r * SUBLANES,
                         gsem.at[q % n_xbuf])

    def scatter_row(q, r):
        return _row_copy(ybuf.at[(q + 3) % 3], r * SUBLANES, y2_ref, dst[(q + 1) % 2, 0, r],
                         ssem.at[(q + 3) % 3])

    def gather_wait(q):
        for r in range(bm):
            _row_copy(hp_ref, 0, xbuf.at[q % n_xbuf], r * SUBLANES, gsem.at[q % n_xbuf]).wait()

    def scatter_wait(q):
        for r in range(bm):
            _row_copy(ybuf.at[(q + 3) % 3], r * SUBLANES, y2_ref, 0, ssem.at[(q + 3) % 3]).wait()

    @pl.when((g == 0) & (c == 0))
    def _prime():
        ybuf[...] = jnp.zeros(ybuf.shape, U32)
        spare = [pltpu.make_async_copy(
            ybuf.at[s], y2_ref.at[pl.ds((spare_row0 + s * bm) * SUBLANES, bm * SUBLANES)], ssem.at[s])
            for s in range(2)]
        for cp in spare:
            cp.start()
        for cp in spare:
            cp.wait()
        dst_copy(-1).start()
        for q in range(ahead):
            src_copy(q).start()
            src_copy(q).wait()
            for r in range(bm):
                gather_row(q, r).start()
        src_copy(ahead).start()

    @pl.when(g < n_experts)
    def _cast_next_expert_chunk():
        slot = g % 2
        w1b[slot, c] = w1_ref[...].astype(BF16)
        w3b[slot, c] = w3_ref[...].astype(BF16)
        w2b[slot, pl.ds(pl.multiple_of(c * fc, fc), fc), :] = w2_ref[...].astype(BF16)

    @pl.when(g >= 1)
    def _compute_previous_expert_share():
        e = g - 1
        slot = e % 2
        nb = ec_ref[e]
        base = es_ref[e]
        n_out = w2b.shape[2] // MOE_OUT_CHUNK

        def block(r, carry):
            q = base + r
            src_copy(q + ahead).wait()
            src_copy(q + ahead + 1).start()
            dst_copy(q - 1).wait()
            dst_copy(q).start()
            gather_wait(q)

            @pl.when(q >= 2)
            def _():
                scatter_wait(q - 3)

            pending = [gather_row(q + ahead, rr) for rr in range(bm)]
            pending += [scatter_row(q - 1, rr) for rr in range(bm)]
            per_gap = -(-len(pending) // (4 * nch + n_out))

            def issue_some():
                for _ in range(min(per_gap, len(pending))):
                    pending.pop(0).start()

            u = _load_tile_rows(xbuf, (q % n_xbuf,), bm)
            x_lo = _unpack_lo(u).astype(BF16)
            x_hi = _unpack_hi(u).astype(BF16)
            hs = []
            for cc in range(nch):
                a = jnp.dot(x_lo, w1b[slot, cc, :half, :], preferred_element_type=F32)
                issue_some()
                a = a + jnp.dot(x_hi, w1b[slot, cc, half:, :], preferred_element_type=F32)
                issue_some()
                b = jnp.dot(x_lo, w3b[slot, cc, :half, :], preferred_element_type=F32)
                issue_some()
                b = b + jnp.dot(x_hi, w3b[slot, cc, half:, :], preferred_element_type=F32)
                issue_some()
                hs.append((a * _sigmoid(a) * b).astype(BF16))
            hmid = jnp.concatenate(hs, axis=1)
            ys = []
            for oc in range(n_out):
                cols = slice(oc * MOE_OUT_CHUNK, (oc + 1) * MOE_OUT_CHUNK)
                ys.append(jnp.dot(hmid, w2b[slot, :, cols], preferred_element_type=F32))
                issue_some()
            assert not pending
            y = jnp.concatenate(ys, axis=1)
            _store_tile_rows(ybuf, (q % 3,), _pack_bf16_pair(y[:, :half], y[:, half:]))
            return carry

        lax.fori_loop((nb * c) // nch, (nb * (c + 1)) // nch, block, 0)

    @pl.when((g == n_experts) & (c == nch - 1))
    def _drain():
        dst_copy(n_total - 1).wait()
        for r in range(bm):
            scatter_row(n_total - 1, r).start()

        @pl.when(n_total >= 2)
        def _():
            scatter_wait(n_total - 3)

        @pl.when(n_total >= 1)
        def _():
            scatter_wait(n_total - 2)

        scatter_wait(n_total - 1)
        for k in range(ahead):
            gather_wait(n_total + k)
        src_copy(n_total + ahead).wait()


def _moe_experts(eb_start, eb_count, src_plan, dst_plan, hp, w1, w3, w2, *, n_tokens, bm, nch):
    n_out_rows = 2 * n_tokens + 2 * bm
    E, D, F = w1.shape
    assert D == 2 * SUBLANES * LANES and hp.shape[1] == LANES and D % MOE_OUT_CHUNK == 0
    assert F % nch == 0 and src_plan.shape[1:] == (1, bm) and dst_plan.shape[1:] == (1, bm)
    assert dst_plan.shape[0] == src_plan.shape[0] + 1
    fc = F // nch

    def w_in_idx(g, c, es, ec):
        return (jnp.minimum(g, E - 1), 0, jnp.where(g < E, c, nch - 1))

    def w_out_idx(g, c, es, ec):
        return (jnp.minimum(g, E - 1), jnp.where(g < E, c, nch - 1), 0)

    grid_spec = pltpu.PrefetchScalarGridSpec(
        num_scalar_prefetch=2,
        grid=(E + 1, nch),
        in_specs=[
            pl.BlockSpec(memory_space=pl.ANY),
            pl.BlockSpec(memory_space=pl.ANY),
            pl.BlockSpec(memory_space=pl.ANY),
            pl.BlockSpec((None, D, fc), w_in_idx),
            pl.BlockSpec((None, D, fc), w_in_idx),
            pl.BlockSpec((None, fc, D), w_out_idx),
        ],
        out_specs=pl.BlockSpec(memory_space=pl.ANY),
        scratch_shapes=[
            pltpu.VMEM((2, nch, D, fc), BF16),
            pltpu.VMEM((2, nch, D, fc), BF16),
            pltpu.VMEM((2, F, D), BF16),
            pltpu.VMEM((MOE_GATHER_AHEAD + 1, bm * SUBLANES, LANES), U32),
            pltpu.VMEM((3, bm * SUBLANES, LANES), U32),
            pltpu.SMEM((2, 1, bm), I32),
            pltpu.SMEM((2, 1, bm), I32),
            pltpu.SemaphoreType.DMA((MOE_GATHER_AHEAD + 1,)),
            pltpu.SemaphoreType.DMA((3,)),
            pltpu.SemaphoreType.DMA((2,)),
            pltpu.SemaphoreType.DMA((2,)),
        ],
    )
    body = functools.partial(_moe_body, n_experts=E, nch=nch, bm=bm, spare_row0=2 * n_tokens)
    return pl.pallas_call(
        body,
        grid_spec=grid_spec,
        out_shape=jax.ShapeDtypeStruct((n_out_rows * SUBLANES, LANES), U32),
        compiler_params=_cparams(("arbitrary", "arbitrary")),
        name="moe_experts",
    )(eb_start, eb_count, src_plan, dst_plan, hp, w1, w3, w2)


def _combine_body(h_ref, rt_ref, g_ref, b_ref, y0_ref, y1_ref, o_ref, *, alpha, eps):
    tt = h_ref.shape[0]
    u0 = _load_tile_rows(y0_ref, (), tt)
    u1 = _load_tile_rows(y1_ref, (), tt)
    half = u0.shape[1]
    c0 = rt_ref[:, 2:3]
    c1 = rt_ref[:, 3:4]
    z_lo = alpha * h_ref[:, :half] + (_unpack_lo(u0) * c0 + _unpack_lo(u1) * c1)
    z_hi = alpha * h_ref[:, half:] + (_unpack_hi(u0) * c0 + _unpack_hi(u1) * c1)
    n = 2 * half
    mu = (jnp.sum(z_lo, axis=1, keepdims=True) + jnp.sum(z_hi, axis=1, keepdims=True)) * (1.0 / n)
    d_lo = z_lo - mu
    d_hi = z_hi - mu
    var = (jnp.sum(d_lo * d_lo, axis=1, keepdims=True)
           + jnp.sum(d_hi * d_hi, axis=1, keepdims=True)) * (1.0 / n)
    inv = lax.rsqrt(var + eps)
    o_ref[:, :half] = d_lo * inv * g_ref[:, :half] + b_ref[:, :half]
    o_ref[:, half:] = d_hi * inv * g_ref[:, half:] + b_ref[:, half:]


def _combine(h, rt, y2, g2, b2, *, alpha, tt):
    T, D = h.shape
    assert T % tt == 0 and D == 2 * SUBLANES * LANES and y2.shape[1] == LANES
    body = functools.partial(_combine_body, alpha=alpha, eps=LN_EPS)
    return pl.pallas_call(
        body,
        grid=(T // tt,),
        in_specs=[
            pl.BlockSpec((tt, D), lambda i: (i, 0)),
            pl.BlockSpec((tt, LANES), lambda i: (i, 0)),
            pl.BlockSpec((1, D), lambda i: (0, 0)),
            pl.BlockSpec((1, D), lambda i: (0, 0)),
            pl.BlockSpec((tt * SUBLANES, LANES), lambda i: (i, 0)),
            pl.BlockSpec((tt * SUBLANES, LANES), lambda i: (i + T // tt, 0)),
        ],
        out_specs=pl.BlockSpec((tt, D), lambda i: (i, 0)),
        out_shape=jax.ShapeDtypeStruct((T, D), F32),
        compiler_params=_cparams(("parallel",)),
        name="moe_combine",
    )(h, rt, g2, b2, y2, y2)


def _slot_plan(expert, n_experts, bm, n_plan_blocks):
    T = expert.shape[0]
    e_flat = expert.reshape(-1)
    onehot = (e_flat[:, None] == jnp.arange(n_experts, dtype=I32)[None, :]).astype(I32)
    csum = jnp.cumsum(onehot, axis=0)
    counts = csum[-1]
    eb_count = (counts + bm - 1) // bm
    eb_start = jnp.cumsum(eb_count) - eb_count
    slot = jnp.sum(onehot * (csum - 1 + (eb_start * bm)[None, :]), axis=1)
    n_slots = n_plan_blocks * bm
    tok = jnp.arange(2 * T, dtype=I32) // 2
    k = jnp.arange(2 * T, dtype=I32) % 2
    p = jnp.arange(n_slots, dtype=I32)
    spare = 2 * T + ((p // bm) % 2) * bm + p % bm
    dst = spare.at[slot].set(k * T + tok, unique_indices=True)
    src = jnp.where(dst < 2 * T, dst % T, 0)
    dst = jnp.concatenate([2 * T + bm + jnp.arange(bm, dtype=I32), dst])
    to_blocks = lambda a: (a * SUBLANES).reshape(-1, 1, bm)
    return eb_start.astype(I32), eb_count.astype(I32), to_blocks(src), to_blocks(dst)


def _layer(x2, p, *, batch, seq, depth, n_heads, head_dim, moba_block, moba_topk,
           n_groups, e_per_group, tiles):
    T, D = x2.shape
    A = n_heads * head_dim
    C = p["w_dw"].shape[1]
    alpha = (2.0 * depth) ** 0.25
    tn = tiles["proj_tn"]
    assert A == C == D, "column-block addressing below assumes equal branch widths"

    w_in, b_in = p["w_in"], p["b_in"]

    half = head_dim // 2
    inv_freq = jnp.power(ROPE_THETA, -jnp.arange(half, dtype=F32) * (2.0 / head_dim))
    ang = jnp.arange(seq, dtype=F32)[:, None] * inv_freq[None, :]
    cos2 = jnp.concatenate([jnp.cos(ang), jnp.cos(ang)], axis=1)
    sin_s = jnp.concatenate([-jnp.sin(ang), jnp.sin(ang)], axis=1)

    u = _in_projection(x2, w_in.astype(BF16), b_in[None, :], cos2, sin_s, seq=seq, attn_w=A, conv_c=C,
                       head_dim=head_dim, tm=tiles["proj_tm"], tn=tn)
    o = _moba_attention(u, batch=batch, seq=seq, n_heads=n_heads, head_dim=head_dim,
                        blk=moba_block, topk=moba_topk, hps=ATTN_HPS)
    hc = _conv_branch(u, p["w_dw"], p["b_dw"], p["conv_ln_g"], p["conv_ln_b"], batch=batch,
                      seq=seq, conv_c=C, col_block=3, ts=tiles["conv_ts"], rt=tiles["conv_rt"],
                      halo=CONV_HALO)

    n_experts = n_groups * e_per_group
    n_route = n_groups + n_experts
    w_route = jnp.concatenate(
        [p["w_rg"], p["w_re"].transpose(1, 0, 2).reshape(D, n_experts),
         jnp.zeros((D, LANES - n_route), F32)], axis=1)
    b_route = jnp.concatenate(
        [p["b_rg"], p["b_re"].reshape(n_experts), jnp.zeros((LANES - n_route,), F32)])[None, :]
    wr_hi = w_route.astype(BF16)
    wr_lo = jnp.concatenate([wr_hi, (w_route - wr_hi.astype(F32)).astype(BF16)], axis=1)

    h, hp, rt = _post_block(
        o, hc, u, x2, p["w_o_attn"].astype(BF16), p["w_pw2"].astype(BF16), p["b_pw2"][None, :],
        p["w_out"].astype(BF16), p["ln1_g"][None, :], p["ln1_b"][None, :], wr_hi, wr_lo, b_route,
        gate_block=4, alpha=alpha, n_groups=n_groups, e_per_group=e_per_group,
        tm=tiles["post_tm"])

    bm = tiles["moe_bm"]
    n_plan_blocks = -(-2 * T // bm) + n_experts + MOE_GATHER_AHEAD + 1
    expert = rt[:, :2].astype(I32)
    eb_start, eb_count, src_plan, dst_plan = _slot_plan(expert, n_experts, bm, n_plan_blocks)
    y2 = _moe_experts(eb_start, eb_count, src_plan, dst_plan, hp, p["w1"], p["w3"], p["w2"],
                      n_tokens=T, bm=bm, nch=tiles["moe_nch"])
    return _combine(h, rt, y2, p["ln2_g"][None, :], p["ln2_b"][None, :], alpha=alpha,
                    tt=tiles["comb_tt"])


_PARAM_NAMES = ("w_in", "b_in", "w_o_attn", "w_dw", "b_dw", "conv_ln_g", "conv_ln_b", "w_pw2",
                "b_pw2", "w_out", "ln1_g", "ln1_b", "w_rg", "b_rg", "w_re", "b_re", "w1", "w3",
                "w2", "ln2_g", "ln2_b")

_TILES = dict(proj_tm=PROJ_TM, proj_tn=PROJ_TN, conv_ts=CONV_TS, conv_rt=CONV_RT,
              post_tm=POST_TM, comb_tt=COMB_TT, moe_bm=MOE_BM, moe_nch=MOE_NCH)


def _forward(x, params, *, n_heads=N_HEADS, head_dim=HEAD_DIM, moba_block=MOBA_BLOCK,
             moba_topk=MOBA_TOPK, n_groups=N_GROUPS, e_per_group=EXPERTS_PER_GROUP, tiles=None):
    tiles = dict(_TILES, **(tiles or {}))
    B, S, D = x.shape
    depth = params["w_in"].shape[0]
    x2 = x.reshape(B * S, D)
    for l in range(depth):
        p = {k: v[l] for k, v in params.items()}
        x2 = _layer(x2, p, batch=B, seq=S, depth=depth, n_heads=n_heads, head_dim=head_dim,
                    moba_block=moba_block, moba_topk=moba_topk, n_groups=n_groups,
                    e_per_group=e_per_group, tiles=tiles)
    return x2.reshape(B, S, D)


def kernel(x, w_in, b_in, w_o_attn, w_dw, b_dw, conv_ln_g, conv_ln_b, w_pw2, b_pw2, w_out, ln1_g,
           ln1_b, w_rg, b_rg, w_re, b_re, w1, w3, w2, ln2_g, ln2_b):
    params = dict(zip(_PARAM_NAMES, (w_in, b_in, w_o_attn, w_dw, b_dw, conv_ln_g, conv_ln_b, w_pw2,
                                     b_pw2, w_out, ln1_g, ln1_b, w_rg, b_rg, w_re, b_re, w1, w3,
                                     w2, ln2_g, ln2_b)))
    return _forward(x, params)
```

```python
import functools

import jax
import jax.numpy as jnp
from jax import lax
from jax.experimental import pallas as pl
from jax.experimental.pallas import tpu as pltpu

F32 = jnp.float32
BF16 = jnp.bfloat16
U32 = jnp.uint32
I32 = jnp.int32

N_HEADS = 16
HEAD_DIM = 128
ROPE_THETA = 10000.0
MOBA_BLOCK = 256
MOBA_TOPK = 3
CONV_WIDTH = 31
N_GROUPS = 4
EXPERTS_PER_GROUP = 8
LN_EPS = 1e-5
LOG2_E = 1.4426950408889634

LANES = 128
SUBLANES = 8
VMEM_LIMIT = 56 * 1024 * 1024

PROJ_TM = 1024
PROJ_TN = 1024
ATTN_HPS = 4
ATTN_LOOKAHEAD = 2
ATTN_ONES_ROWS = 16
CONV_TS = 256
CONV_RT = 16
CONV_HALO = 32
CONV_TAP_GROUP = 4
POST_TM = 256
COMB_TT = 512
MOE_BM = 256
MOE_NCH = 4
MOE_OUT_CHUNK = 256
MOE_GATHER_AHEAD = 2


def _cparams(sem):
    return pltpu.CompilerParams(dimension_semantics=sem, vmem_limit_bytes=VMEM_LIMIT)


def _sigmoid(x):
    return 1.0 / (1.0 + jnp.exp(-x))


def _pack_bf16_pair(lo_f32, hi_f32):
    lo = lax.bitcast_convert_type(lo_f32.astype(BF16).astype(F32), U32) >> 16
    hi = lax.bitcast_convert_type(hi_f32.astype(BF16).astype(F32), U32)
    return hi | lo


def _unpack_lo(u):
    return lax.bitcast_convert_type(u << 16, F32)


def _unpack_hi(u):
    return lax.bitcast_convert_type(u & jnp.uint32(0xFFFF0000), F32)


def _store_tile_rows(ref, lead, x):
    n = x.shape[0]
    for s in range(SUBLANES):
        ref[(*lead, pl.ds(s, n, stride=SUBLANES), slice(None))] = x[:, s * LANES:(s + 1) * LANES]


def _load_tile_rows(ref, lead, n):
    return jnp.concatenate(
        [ref[(*lead, pl.ds(s, n, stride=SUBLANES), slice(None))] for s in range(SUBLANES)], axis=1)


def _inproj_body(x_ref, w_ref, wg_ref, b_ref, bg_ref, cos_ref, sin_ref, o_ref, xb_ref,
                 *, n_q, n_qk, n_qkv, n_glu, scale, head_dim):
    j = pl.program_id(1)
    tn = o_ref.shape[1]

    @pl.when(j == 0)
    def _cast():
        xb_ref[...] = x_ref[...].astype(BF16)

    def project(w, b):
        return jnp.dot(xb_ref[...], w[...], preferred_element_type=F32) + b[...]

    @pl.when(j < n_qk)
    def _rope():
        acc = project(w_ref, b_ref)
        s = jnp.where(j < n_q, scale, 1.0).astype(F32)
        cos = cos_ref[...] * s
        sin = sin_ref[...] * s
        for h in range(tn // head_dim):
            t = acc[:, h * head_dim:(h + 1) * head_dim]
            r = pltpu.roll(t, head_dim // 2, axis=1)
            o_ref[:, h * head_dim:(h + 1) * head_dim] = (t * cos + r * sin).astype(o_ref.dtype)

    @pl.when((j >= n_qk) & (j < n_qkv))
    def _plain():
        o_ref[...] = project(w_ref, b_ref).astype(o_ref.dtype)

    @pl.when((j >= n_qkv) & (j < n_qkv + n_glu))
    def _glu():
        a = project(w_ref, b_ref)
        g = project(wg_ref, bg_ref)
        o_ref[...] = (a * _sigmoid(g)).astype(o_ref.dtype)

    @pl.when(j >= n_qkv + n_glu)
    def _gate():
        o_ref[...] = _sigmoid(project(w_ref, b_ref)).astype(o_ref.dtype)


def _in_projection(x2, w_all, b_all, cos2, sin_s, *, seq, attn_w, conv_c, head_dim, tm, tn):
    T, D = x2.shape
    W = w_all.shape[1] - conv_c
    assert T % tm == 0 and seq % tm == 0 and W % tn == 0
    assert attn_w % tn == 0 and conv_c % tn == 0 and tn % head_dim == 0
    assert head_dim == LANES
    n_q = attn_w // tn
    n_glu = conv_c // tn
    n_qkv = 3 * n_q
    pos_tiles = seq // tm

    def main_idx(i, j):
        return (0, jnp.where(j < n_qkv + n_glu, j, j + n_glu))

    def g_idx(i, j):
        return (0, n_qkv + n_glu + jnp.clip(j - n_qkv, 0, n_glu - 1))

    body = functools.partial(_inproj_body, n_q=n_q, n_qk=2 * n_q, n_qkv=n_qkv, n_glu=n_glu,
                             scale=float(head_dim) ** -0.5 * LOG2_E, head_dim=head_dim)
    return pl.pallas_call(
        body,
        grid=(T // tm, W // tn),
        in_specs=[
            pl.BlockSpec((tm, D), lambda i, j: (i, 0)),
            pl.BlockSpec((D, tn), main_idx),
            pl.BlockSpec((D, tn), g_idx),
            pl.BlockSpec((1, tn), main_idx),
            pl.BlockSpec((1, tn), g_idx),
            pl.BlockSpec((tm, head_dim), lambda i, j: (i % pos_tiles, 0)),
            pl.BlockSpec((tm, head_dim), lambda i, j: (i % pos_tiles, 0)),
        ],
        out_specs=pl.BlockSpec((tm, tn), lambda i, j: (i, j)),
        out_shape=jax.ShapeDtypeStruct((T, W), BF16),
        scratch_shapes=[pltpu.VMEM((tm, D), BF16)],
        compiler_params=_cparams(("parallel", "arbitrary")),
        name="in_projection",
    )(x2, w_all, w_all, b_all, b_all, cos2, sin_s)


def _attn_prep(q, k, v, *, seq, blk):
    nb = seq // blk
    nbp = -(-nb // SUBLANES) * SUBLANES
    dh = q.shape[1]
    contract_last = (((1,), (1,)), ((), ()))
    kmean = jnp.mean(k.astype(F32).reshape(nb, blk, dh), axis=1)
    if nbp > nb:
        kmean = jnp.concatenate([kmean, jnp.zeros((nbp - nb, dh), F32)], axis=0)
    gate_t = lax.dot_general(kmean.astype(BF16), q, contract_last, preferred_element_type=F32)
    ones_rows = (lax.broadcasted_iota(I32, (ATTN_ONES_ROWS, seq), 0) == 0).astype(BF16)
    vt = jnp.concatenate([v.astype(F32).T.astype(BF16), ones_rows], axis=0)
    return gate_t, vt


def _attn_scores(q, k, gate_t, i, s_ref, *, blk, topk, nbp):
    contract_last = (((1,), (1,)), ((), ()))
    neg_inf = jnp.float32(-jnp.inf)
    sub = lax.broadcasted_iota(I32, (nbp, blk), 0)
    qi = q[i * blk:(i + 1) * blk]
    nk = (i + 1) * blk
    st = lax.dot_general(k[:nk], qi, contract_last, preferred_element_type=F32)
    if i > topk:
        gm = jnp.where(sub < i, gate_t[:, i * blk:(i + 1) * blk], neg_inf)
        rank = jnp.zeros((nbp, blk), I32)
        for other in range(i):
            g_other = gm[other:other + 1, :]
            beats = (g_other > gm) | ((g_other == gm) & (sub > other))
            rank = rank + beats.astype(I32)
        bias_t = jnp.where((sub < i) & (rank < topk), 0.0, neg_inf).astype(F32)
    m = None
    for n in range(i + 1):
        t = st[n * blk:(n + 1) * blk]
        if n == i:
            key_r = lax.broadcasted_iota(I32, (blk, blk), 0)
            qry_c = lax.broadcasted_iota(I32, (blk, blk), 1)
            t = jnp.where(key_r <= qry_c, t, neg_inf)
        elif i > topk:
            t = t + bias_t[n:n + 1, :]
        s_ref[n * blk:(n + 1) * blk, :] = t
        t_max = jnp.max(t, axis=0, keepdims=True)
        m = t_max if m is None else jnp.maximum(m, t_max)
    return m


def _attn_output(vt, m, i, s_ref, *, blk, dh):
    nk = (i + 1) * blk
    p = jnp.exp2(s_ref[:nk, :] - m).astype(BF16)
    ot = jnp.dot(vt[:, :nk], p, preferred_element_type=F32)
    return (ot[:dh] * (1.0 / ot[dh:dh + 1])).T


def _attn_body(q_ref, k_ref, v_ref, o_ref, s_scr, *, seq, blk, topk, dh):
    nb = seq // blk
    nbp = -(-nb // SUBLANES) * SUBLANES
    heads = []
    for hh in range(q_ref.shape[1] // dh):
        c0 = hh * dh
        q, k, v = q_ref[:, c0:c0 + dh], k_ref[:, c0:c0 + dh], v_ref[:, c0:c0 + dh]
        heads.append((c0, q, k) + _attn_prep(q, k, v, seq=seq, blk=blk))
    units = [(h, i) for h in range(len(heads)) for i in range(nb)]

    n_buf = s_scr.shape[0]
    ahead = n_buf - 1

    def scores(u):
        h, i = units[u]
        _, q, k, gate_t, _ = heads[h]
        return _attn_scores(q, k, gate_t, i, s_scr.at[u % n_buf], blk=blk, topk=topk, nbp=nbp)

    col_max = {u: scores(u) for u in range(min(ahead, len(units)))}
    for u, (h, i) in enumerate(units):
        if u + ahead < len(units):
            col_max[u + ahead] = scores(u + ahead)
        c0, _, _, _, vt = heads[h]
        o = _attn_output(vt, col_max.pop(u), i, s_scr.at[u % n_buf], blk=blk, dh=dh)
        o_ref[i * blk:(i + 1) * blk, c0:c0 + dh] = o.astype(o_ref.dtype)


def _moba_attention(u, *, batch, seq, n_heads, head_dim, blk, topk, hps):
    T = u.shape[0]
    assert seq % blk == 0 and blk % LANES == 0 and n_heads % hps == 0
    n_hb = n_heads // hps
    body = functools.partial(_attn_body, seq=seq, blk=blk, topk=topk, dh=head_dim)
    return pl.pallas_call(
        body,
        grid=(batch, n_hb),
        in_specs=[
            pl.BlockSpec((seq, hps * head_dim), lambda b, h: (b, h)),
            pl.BlockSpec((seq, hps * head_dim), lambda b, h: (b, n_hb + h)),
            pl.BlockSpec((seq, hps * head_dim), lambda b, h: (b, 2 * n_hb + h)),
        ],
        out_specs=pl.BlockSpec((seq, hps * head_dim), lambda b, h: (b, h)),
        out_shape=jax.ShapeDtypeStruct((T, n_heads * head_dim), BF16),
        scratch_shapes=[pltpu.VMEM((ATTN_LOOKAHEAD + 1, seq, blk), F32)],
        compiler_params=_cparams(("parallel", "parallel")),
        name="moba_attention",
    )(u, u, u)


def _conv_body(x_ref, w_ref, bdw_ref, g_ref, b_ref, o_ref, win_ref, conv_ref,
               *, width, ts, rt, halo, eps):
    sub = SUBLANES
    half = sub * LANES
    ng = conv_ref.shape[0]

    @pl.when(pl.program_id(1) == 0)
    def _zero_halo():
        win_ref[0:halo * sub, :] = jnp.zeros((halo * sub, LANES), U32)

    for s in range(sub):
        lo = x_ref[:, s * LANES:(s + 1) * LANES].astype(F32)
        hi = x_ref[:, half + s * LANES:half + (s + 1) * LANES].astype(F32)
        win_ref[pl.ds(halo * sub + s, ts, stride=sub), :] = _pack_bf16_pair(lo, hi)

    first = halo - (width - 1)

    def chunk(r, carry):
        t0 = r * rt
        acc_lo = jnp.broadcast_to(bdw_ref[0][None], (rt, sub, LANES))
        acc_hi = jnp.broadcast_to(bdw_ref[1][None], (rt, sub, LANES))
        for j0 in range(0, width, CONV_TAP_GROUP):
            part = None
            for j in range(j0, min(j0 + CONV_TAP_GROUP, width)):
                start = pl.multiple_of((t0 + first + j) * sub, sub)
                slab = pltpu.bitcast(win_ref[pl.ds(start, rt * sub), :], BF16)
                prod = slab.reshape(rt, 2 * sub, LANES) * pltpu.bitcast(w_ref[j], BF16)[None]
                part = prod if part is None else part + prod
            u = pltpu.bitcast(part.reshape(rt * 2 * sub, LANES), U32)
            acc_lo = acc_lo + _unpack_lo(u).reshape(rt, sub, LANES)
            acc_hi = acc_hi + _unpack_hi(u).reshape(rt, sub, LANES)
        rows = pl.ds(pl.multiple_of(t0 * sub, sub), rt * sub)
        conv_ref[0, rows, :] = acc_lo.reshape(rt * sub, LANES)
        conv_ref[1, rows, :] = acc_hi.reshape(rt * sub, LANES)
        return carry

    lax.fori_loop(0, ts // rt, chunk, 0)

    win_ref[0:halo * sub, :] = win_ref[ts * sub:(ts + halo) * sub, :]

    def channel_chunks():
        for g in range(ng):
            for s in range(sub):
                yield (g * sub + s) * LANES, conv_ref[g, pl.ds(s, ts, stride=sub), :]

    n = ng * sub * LANES
    total = jnp.zeros((ts, 1), F32)
    for _, y in channel_chunks():
        total = total + jnp.sum(y, axis=1, keepdims=True)
    mu = total * (1.0 / n)
    sq = jnp.zeros((ts, 1), F32)
    for _, y in channel_chunks():
        d = y - mu
        sq = sq + jnp.sum(d * d, axis=1, keepdims=True)
    inv = lax.rsqrt(sq * (1.0 / n) + eps)
    for c0, y in channel_chunks():
        z = (y - mu) * inv * g_ref[:, c0:c0 + LANES] + b_ref[:, c0:c0 + LANES]
        o_ref[:, c0:c0 + LANES] = (z * _sigmoid(z)).astype(o_ref.dtype)


def _conv_branch(u, w_dw, b_dw, ln_g, ln_b, *, batch, seq, conv_c, col_block, ts, rt, halo):
    T = u.shape[0]
    width = w_dw.shape[0]
    half = SUBLANES * LANES
    assert seq % ts == 0 and conv_c == 2 * half and ts % rt == 0
    assert halo >= width - 1 and ts >= halo
    w16 = lax.bitcast_convert_type(w_dw.astype(BF16), jnp.uint16).astype(U32)
    w_pk = (w16[:, :half] | (w16[:, half:] << 16)).reshape(width, SUBLANES, LANES)
    n_s = seq // ts
    body = functools.partial(_conv_body, width=width, ts=ts, rt=rt, halo=halo, eps=LN_EPS)
    return pl.pallas_call(
        body,
        grid=(batch, n_s),
        in_specs=[
            pl.BlockSpec((ts, conv_c), lambda b, s: (b * n_s + s, col_block)),
            pl.BlockSpec((width, SUBLANES, LANES), lambda b, s: (0, 0, 0)),
            pl.BlockSpec((2, SUBLANES, LANES), lambda b, s: (0, 0, 0)),
            pl.BlockSpec((1, conv_c), lambda b, s: (0, 0)),
            pl.BlockSpec((1, conv_c), lambda b, s: (0, 0)),
        ],
        out_specs=pl.BlockSpec((ts, conv_c), lambda b, s: (b * n_s + s, 0)),
        out_shape=jax.ShapeDtypeStruct((T, conv_c), BF16),
        scratch_shapes=[pltpu.VMEM(((halo + ts) * SUBLANES, LANES), U32),
                        pltpu.VMEM((2, ts * SUBLANES, LANES), F32)],
        compiler_params=_cparams(("parallel", "arbitrary")),
        name="conv_branch",
    )(u, w_pk, b_dw.reshape(2, SUBLANES, LANES), ln_g[None, :], ln_b[None, :])


def _post_body(o_ref, hc_ref, ga_ref, gb_ref, x_ref, wo_ref, wp_ref, bp_ref, wout_ref,
               g1_ref, b1_ref, wrh_ref, wrl_ref, br_ref, h_ref, hp_ref, rt_ref, m_scr, z_scr,
               *, alpha, eps, n_groups, e_per_group):
    i = pl.program_id(0)

    @pl.when(i == 0)
    def _no_previous_tiles():
        m_scr[1] = jnp.zeros(m_scr.shape[1:], BF16)
        z_scr[1] = jnp.zeros(z_scr.shape[1:], F32)
        z_scr[2] = jnp.zeros(z_scr.shape[1:], F32)

    ya = jnp.dot(o_ref[...], wo_ref[...], preferred_element_type=F32)

    z = z_scr[(i + 1) % 3]
    mu = jnp.mean(z, axis=1, keepdims=True)
    zc = z - mu
    var = jnp.mean(zc * zc, axis=1, keepdims=True)
    h = zc * lax.rsqrt(var + eps) * g1_ref[...] + b1_ref[...]
    h_ref[...] = h
    half = h.shape[1] // 2
    _store_tile_rows(hp_ref, (), _pack_bf16_pair(h[:, :half], h[:, half:]))

    yc = jnp.dot(hc_ref[...], wp_ref[...], preferred_element_type=F32) + bp_ref[...]

    h_hi = h.astype(BF16)
    h_lo = (h - h_hi.astype(F32)).astype(BF16)
    hi_terms = jnp.dot(h_hi, wrl_ref[...], preferred_element_type=F32)
    logits = (hi_terms[:, :LANES] + hi_terms[:, LANES:]
              + jnp.dot(h_lo, wrh_ref[...], preferred_element_type=F32) + br_ref[...])
    tm = logits.shape[0]
    lane = lax.broadcasted_iota(I32, (tm, LANES), 1)
    neg_inf = jnp.float32(-jnp.inf)
    big = jnp.int32(LANES)

    def first_argmax(vals):
        top = jnp.max(vals, axis=1, keepdims=True)
        idx = jnp.min(jnp.where(vals == top, lane, big), axis=1, keepdims=True)
        return top, idx

    gl = jnp.where(lane < n_groups, logits, neg_inf)
    gmax, grp = first_argmax(gl)
    grp_w = 1.0 / jnp.sum(jnp.exp(gl - gmax), axis=1, keepdims=True)
    lo_lane = n_groups + grp * e_per_group
    el = jnp.where((lane >= lo_lane) & (lane < lo_lane + e_per_group), logits, neg_inf)
    v1, i1 = first_argmax(el)
    v2, i2 = first_argmax(jnp.where(lane == i1, neg_inf, el))
    t = jnp.exp(v2 - v1)
    p1 = 1.0 / (1.0 + t)
    c1 = p1 * grp_w
    c2 = (t * p1) * grp_w
    e1 = (i1 - n_groups).astype(F32)
    e2 = (i2 - n_groups).astype(F32)
    rt_ref[...] = jnp.where(lane == 0, e1, jnp.where(lane == 1, e2,
                            jnp.where(lane == 2, c1, jnp.where(lane == 3, c2, 0.0))))

    prev = (i + 1) % 2
    z_scr[(i + 2) % 3] += jnp.dot(m_scr[prev], wout_ref[...], preferred_element_type=F32)
    m_scr[i % 2] = (ga_ref[...].astype(F32) * ya + gb_ref[...].astype(F32) * yc).astype(BF16)
    z_scr[i % 3] = alpha * x_ref[...]


def _post_block(o, hc, u, x2, wo, wp, bp, wout, g1, b1, wr_hi, wr_lo, br, *, gate_block,
                alpha, n_groups, e_per_group, tm):
    T, D = x2.shape
    A = o.shape[1]
    C = hc.shape[1]
    assert T % tm == 0 and D == 2 * SUBLANES * LANES, "packed rows are stored as one (8, 128) tile"
    assert n_groups * (1 + e_per_group) <= LANES
    const = lambda i: (0, 0)
    resident = lambda shape: pl.BlockSpec(shape, const, pipeline_mode=pl.Buffered(1))
    body = functools.partial(_post_body, alpha=alpha, eps=LN_EPS, n_groups=n_groups,
                             e_per_group=e_per_group)
    n = T // tm
    cur = lambda i: jnp.minimum(i, n - 1)
    prev = lambda i: jnp.maximum(i - 2, 0)
    return pl.pallas_call(
        body,
        grid=(n + 2,),
        in_specs=[
            pl.BlockSpec((tm, A), lambda i: (cur(i), 0)),
            pl.BlockSpec((tm, C), lambda i: (cur(i), 0)),
            pl.BlockSpec((tm, D), lambda i: (cur(i), gate_block)),
            pl.BlockSpec((tm, D), lambda i: (cur(i), gate_block + 1)),
            pl.BlockSpec((tm, D), lambda i: (cur(i), 0)),
            resident((A, D)),
            resident((C, D)),
            resident((1, D)),
            resident((D, D)),
            resident((1, D)),
            resident((1, D)),
            resident((D, LANES)),
            resident((D, 2 * LANES)),
            resident((1, LANES)),
        ],
        out_specs=[
            pl.BlockSpec((tm, D), lambda i: (prev(i), 0)),
            pl.BlockSpec((tm * SUBLANES, LANES), lambda i: (prev(i), 0)),
            pl.BlockSpec((tm, LANES), lambda i: (prev(i), 0)),
        ],
        out_shape=[
            jax.ShapeDtypeStruct((T, D), F32),
            jax.ShapeDtypeStruct((T * SUBLANES, LANES), U32),
            jax.ShapeDtypeStruct((T, LANES), F32),
        ],
        scratch_shapes=[pltpu.VMEM((2, tm, D), BF16), pltpu.VMEM((3, tm, D), F32)],
        compiler_params=_cparams(("arbitrary",)),
        name="merge_project_route",
    )(o, hc, u, u, x2, wo, wp, bp, wout, g1, b1, wr_hi, wr_lo, br)


def _row_copy(src_ref, src_row8, dst_ref, dst_row8, sem):
    aligned = lambda v: v if isinstance(v, int) else pl.multiple_of(v, SUBLANES)
    return pltpu.make_async_copy(src_ref.at[pl.ds(aligned(src_row8), SUBLANES)],
                                 dst_ref.at[pl.ds(aligned(dst_row8), SUBLANES)], sem)


def _moe_body(es_ref, ec_ref, src_ref, dst_ref, hp_ref, w1_ref, w3_ref, w2_ref, y2_ref,
              w1b, w3b, w2b, xbuf, ybuf, src, dst, gsem, ssem, src_sem, dst_sem,
              *, n_experts, nch, bm, spare_row0):
    g = pl.program_id(0)
    c = pl.program_id(1)
    fc = w1_ref.shape[1]
    half = SUBLANES * LANES
    n_total = es_ref[n_experts - 1] + ec_ref[n_experts - 1]
    n_xbuf = xbuf.shape[0]
    ahead = n_xbuf - 1

    def src_copy(q):
        return pltpu.make_async_copy(src_ref.at[q], src.at[q % 2], src_sem.at[q % 2])

    def dst_copy(q):
        return pltpu.make_async_copy(dst_ref.at[q + 1], dst.at[(q + 1) % 2], dst_sem.at[(q + 1) % 2])

    def gather_row(q, r):
        return _row_copy(hp_ref, src[q % 2, 0, r], xbuf.at[q % n_xbuf], r * SUBLANES,
                         gsem.at[q % n_xbuf])

    def scatter_row(q, r):
        return _row_copy(ybuf.at[(q + 3) % 3], r * SUBLANES, y2_ref, dst[(q + 1) % 2, 0, r],
                         ssem.at[(q + 3) % 3])

    def gather_wait(q):
        for r in range(bm):
            _row_copy(hp_ref, 0, xbuf.at[q % n_xbuf], r * SUBLANES, gsem.at[q % n_xbuf]).wait()

    def scatter_wait(q):
        for r in range(bm):
            _row_copy(ybuf.at[(q + 3) % 3], r * SUBLANES, y2_ref, 0, ssem.at[(q + 3) % 3]).wait()

    @pl.when((g == 0) & (c == 0))
    def _prime():
        ybuf[...] = jnp.zeros(ybuf.shape, U32)
        spare = [pltpu.make_async_copy(
            ybuf.at[s], y2_ref.at[pl.ds((spare_row0 + s * bm) * SUBLANES, bm * SUBLANES)], ssem.at[s])
            for s in range(2)]
        for cp in spare:
            cp.start()
        for cp in spare:
            cp.wait()
        dst_copy(-1).start()
        for q in range(ahead):
            src_copy(q).start()
            src_copy(q).wait()
            for r in range(bm):
                gather_row(q, r).start()
        src_copy(ahead).start()

    @pl.when(g < n_experts)
    def _cast_next_expert_chunk():
        slot = g % 2
        w1b[slot, c] = w1_ref[...].astype(BF16)
        w3b[slot, c] = w3_ref[...].astype(BF16)
        w2b[slot, pl.ds(pl.multiple_of(c * fc, fc), fc), :] = w2_ref[...].astype(BF16)

    @pl.when(g >= 1)
    def _compute_previous_expert_share():
        e = g - 1
        slot = e % 2
        nb = ec_ref[e]
        base = es_ref[e]
        n_out = w2b.shape[2] // MOE_OUT_CHUNK

        def block(r, carry):
            q = base + r
            src_copy(q + ahead).wait()
            src_copy(q + ahead + 1).start()
            dst_copy(q - 1).wait()
            dst_copy(q).start()
            gather_wait(q)

            @pl.when(q >= 2)
            def _():
                scatter_wait(q - 3)

            pending = []
            for rr in range(bm):
                pending.append(scatter_row(q - 1, rr))
                pending.append(gather_row(q + ahead, rr))
            per_gap = -(-len(pending) // (4 * nch + n_out))

            def issue_some():
                for _ in range(min(per_gap, len(pending))):
                    pending.pop(0).start()

            u = _load_tile_rows(xbuf, (q % n_xbuf,), bm)
            x_lo = _unpack_lo(u).astype(BF16)
            x_hi = _unpack_hi(u).astype(BF16)
            hs = []
            for cc in range(nch):
                a = jnp.dot(x_lo, w1b[slot, cc, :half, :], preferred_element_type=F32)
                issue_some()
                a = a + jnp.dot(x_hi, w1b[slot, cc, half:, :], preferred_element_type=F32)
                issue_some()
                b = jnp.dot(x_lo, w3b[slot, cc, :half, :], preferred_element_type=F32)
                issue_some()
                b = b + jnp.dot(x_hi, w3b[slot, cc, half:, :], preferred_element_type=F32)
                issue_some()
                hs.append((a * _sigmoid(a) * b).astype(BF16))
            hmid = jnp.concatenate(hs, axis=1)
            ys = []
            for oc in range(n_out):
                cols = slice(oc * MOE_OUT_CHUNK, (oc + 1) * MOE_OUT_CHUNK)
                ys.append(jnp.dot(hmid, w2b[slot, :, cols], preferred_element_type=F32))
                issue_some()
            assert not pending
            y = jnp.concatenate(ys, axis=1)
            _store_tile_rows(ybuf, (q % 3,), _pack_bf16_pair(y[:, :half], y[:, half:]))
            return carry

        lax.fori_loop((nb * c) // nch, (nb * (c + 1)) // nch, block, 0)

    @pl.when((g == n_experts) & (c == nch - 1))
    def _drain():
        dst_copy(n_total - 1).wait()
        for r in range(bm):
            scatter_row(n_total - 1, r).start()

        @pl.when(n_total >= 2)
        def _():
            scatter_wait(n_total - 3)

        @pl.when(n_total >= 1)
        def _():
            scatter_wait(n_total - 2)

        scatter_wait(n_total - 1)
        for k in range(ahead):
            gather_wait(n_total + k)
        src_copy(n_total + ahead).wait()


def _moe_experts(eb_start, eb_count, src_plan, dst_plan, hp, w1, w3, w2, *, n_tokens, bm, nch):
    n_out_rows = 2 * n_tokens + 2 * bm
    E, D, F = w1.shape
    assert D == 2 * SUBLANES * LANES and hp.shape[1] == LANES and D % MOE_OUT_CHUNK == 0
    assert F % nch == 0 and src_plan.shape[1:] == (1, bm) and dst_plan.shape[1:] == (1, bm)
    assert dst_plan.shape[0] == src_plan.shape[0] + 1
    fc = F // nch

    def w_in_idx(g, c, es, ec):
        return (jnp.minimum(g, E - 1), 0, jnp.where(g < E, c, nch - 1))

    def w_out_idx(g, c, es, ec):
        return (jnp.minimum(g, E - 1), jnp.where(g < E, c, nch - 1), 0)

    grid_spec = pltpu.PrefetchScalarGridSpec(
        num_scalar_prefetch=2,
        grid=(E + 1, nch),
        in_specs=[
            pl.BlockSpec(memory_space=pl.ANY),
            pl.BlockSpec(memory_space=pl.ANY),
            pl.BlockSpec(memory_space=pl.ANY),
            pl.BlockSpec((None, D, fc), w_in_idx),
            pl.BlockSpec((None, D, fc), w_in_idx),
            pl.BlockSpec((None, fc, D), w_out_idx),
        ],
        out_specs=pl.BlockSpec(memory_space=pl.ANY),
        scratch_shapes=[
            pltpu.VMEM((2, nch, D, fc), BF16),
            pltpu.VMEM((2, nch, D, fc), BF16),
            pltpu.VMEM((2, F, D), BF16),
            pltpu.VMEM((MOE_GATHER_AHEAD + 1, bm * SUBLANES, LANES), U32),
            pltpu.VMEM((3, bm * SUBLANES, LANES), U32),
            pltpu.SMEM((2, 1, bm), I32),
            pltpu.SMEM((2, 1, bm), I32),
            pltpu.SemaphoreType.DMA((MOE_GATHER_AHEAD + 1,)),
            pltpu.SemaphoreType.DMA((3,)),
            pltpu.SemaphoreType.DMA((2,)),
            pltpu.SemaphoreType.DMA((2,)),
        ],
    )
    body = functools.partial(_moe_body, n_experts=E, nch=nch, bm=bm, spare_row0=2 * n_tokens)
    return pl.pallas_call(
        body,
        grid_spec=grid_spec,
        out_shape=jax.ShapeDtypeStruct((n_out_rows * SUBLANES, LANES), U32),
        compiler_params=_cparams(("arbitrary", "arbitrary")),
        name="moe_experts",
    )(eb_start, eb_count, src_plan, dst_plan, hp, w1, w3, w2)


def _combine_body(h_ref, rt_ref, g_ref, b_ref, y0_ref, y1_ref, o_ref, *, alpha, eps):
    tt = h_ref.shape[0]
    u0 = _load_tile_rows(y0_ref, (), tt)
    u1 = _load_tile_rows(y1_ref, (), tt)
    half = u0.shape[1]
    c0 = rt_ref[:, 2:3]
    c1 = rt_ref[:, 3:4]
    z_lo = alpha * h_ref[:, :half] + (_unpack_lo(u0) * c0 + _unpack_lo(u1) * c1)
    z_hi = alpha * h_ref[:, half:] + (_unpack_hi(u0) * c0 + _unpack_hi(u1) * c1)
    n = 2 * half
    mu = (jnp.sum(z_lo, axis=1, keepdims=True) + jnp.sum(z_hi, axis=1, keepdims=True)) * (1.0 / n)
    d_lo = z_lo - mu
    d_hi = z_hi - mu
    var = (jnp.sum(d_lo * d_lo, axis=1, keepdims=True)
           + jnp.sum(d_hi * d_hi, axis=1, keepdims=True)) * (1.0 / n)
    inv = lax.rsqrt(var + eps)
    o_ref[:, :half] = d_lo * inv * g_ref[:, :half] + b_ref[:, :half]
    o_ref[:, half:] = d_hi * inv * g_ref[:, half:] + b_ref[:, half:]


def _combine(h, rt, y2, g2, b2, *, alpha, tt):
    T, D = h.shape
    assert T % tt == 0 and D == 2 * SUBLANES * LANES and y2.shape[1] == LANES
    body = functools.partial(_combine_body, alpha=alpha, eps=LN_EPS)
    return pl.pallas_call(
        body,
        grid=(T // tt,),
        in_specs=[
            pl.BlockSpec((tt, D), lambda i: (i, 0)),
            pl.BlockSpec((tt, LANES), lambda i: (i, 0)),
            pl.BlockSpec((1, D), lambda i: (0, 0)),
            pl.BlockSpec((1, D), lambda i: (0, 0)),
            pl.BlockSpec((tt * SUBLANES, LANES), lambda i: (i, 0)),
            pl.BlockSpec((tt * SUBLANES, LANES), lambda i: (i + T // tt, 0)),
        ],
        out_specs=pl.BlockSpec((tt, D), lambda i: (i, 0)),
        out_shape=jax.ShapeDtypeStruct((T, D), F32),
        compiler_params=_cparams(("parallel",)),
        name="moe_combine",
    )(h, rt, g2, b2, y2, y2)


def _invert_slots_body(slot_ref, spare_ref, dst_ref, *, n_tokens):
    pltpu.sync_copy(spare_ref, dst_ref)

    def place(a, carry):
        dst_ref[slot_ref[a]] = (a & 1) * n_tokens + lax.shift_right_logical(a, 1)
        return carry

    lax.fori_loop(0, 2 * n_tokens, place, 0, unroll=32)


def _invert_slots(slot, spare, *, n_tokens):
    body = functools.partial(_invert_slots_body, n_tokens=n_tokens)
    return pl.pallas_call(
        body,
        in_specs=[pl.BlockSpec(memory_space=pltpu.SMEM), pl.BlockSpec(memory_space=pl.ANY)],
        out_specs=pl.BlockSpec(memory_space=pltpu.SMEM),
        out_shape=jax.ShapeDtypeStruct(spare.shape, I32),
        name="invert_slots",
    )(slot, spare)


def _slot_plan(expert, n_experts, bm, n_plan_blocks):
    T = expert.shape[0]
    e_flat = expert.reshape(-1)
    onehot = (e_flat[:, None] == jnp.arange(n_experts, dtype=I32)[None, :]).astype(I32)
    csum = jnp.cumsum(onehot, axis=0)
    counts = csum[-1]
    eb_count = (counts + bm - 1) // bm
    eb_start = jnp.cumsum(eb_count) - eb_count
    slot = jnp.sum(onehot * (csum - 1 + (eb_start * bm)[None, :]), axis=1)
    n_slots = n_plan_blocks * bm
    p = jnp.arange(n_slots, dtype=I32)
    spare = 2 * T + ((p // bm) % 2) * bm + p % bm
    dst = _invert_slots(slot.astype(I32), spare, n_tokens=T)
    src = jnp.where(dst < 2 * T, dst % T, 0)
    dst = jnp.concatenate([2 * T + bm + jnp.arange(bm, dtype=I32), dst])
    to_blocks = lambda a: (a * SUBLANES).reshape(-1, 1, bm)
    return eb_start.astype(I32), eb_count.astype(I32), to_blocks(src), to_blocks(dst)


def _layer(x2, p, *, batch, seq, depth, n_heads, head_dim, moba_block, moba_topk,
           n_groups, e_per_group, tiles):
    T, D = x2.shape
    A = n_heads * head_dim
    C = p["w_dw"].shape[1]
    alpha = (2.0 * depth) ** 0.25
    tn = tiles["proj_tn"]
    assert A == C == D, "column-block addressing below assumes equal branch widths"

    w_in, b_in = p["w_in"], p["b_in"]

    half = head_dim // 2
    inv_freq = jnp.power(ROPE_THETA, -jnp.arange(half, dtype=F32) * (2.0 / head_dim))
    ang = jnp.arange(seq, dtype=F32)[:, None] * inv_freq[None, :]
    cos2 = jnp.concatenate([jnp.cos(ang), jnp.cos(ang)], axis=1)
    sin_s = jnp.concatenate([-jnp.sin(ang), jnp.sin(ang)], axis=1)

    u = _in_projection(x2, w_in.astype(BF16), b_in[None, :], cos2, sin_s, seq=seq, attn_w=A, conv_c=C,
                       head_dim=head_dim, tm=tiles["proj_tm"], tn=tn)
    o = _moba_attention(u, batch=batch, seq=seq, n_heads=n_heads, head_dim=head_dim,
                        blk=moba_block, topk=moba_topk, hps=ATTN_HPS)
    hc = _conv_branch(u, p["w_dw"], p["b_dw"], p["conv_ln_g"], p["conv_ln_b"], batch=batch,
                      seq=seq, conv_c=C, col_block=3, ts=tiles["conv_ts"], rt=tiles["conv_rt"],
                      halo=CONV_HALO)

    n_experts = n_groups * e_per_group
    n_route = n_groups + n_experts
    w_route = jnp.concatenate(
        [p["w_rg"], p["w_re"].transpose(1, 0, 2).reshape(D, n_experts),
         jnp.zeros((D, LANES - n_route), F32)], axis=1)
    b_route = jnp.concatenate(
        [p["b_rg"], p["b_re"].reshape(n_experts), jnp.zeros((LANES - n_route,), F32)])[None, :]
    wr_hi = w_route.astype(BF16)
    wr_lo = jnp.concatenate([wr_hi, (w_route - wr_hi.astype(F32)).astype(BF16)], axis=1)

    h, hp, rt = _post_block(
        o, hc, u, x2, p["w_o_attn"].astype(BF16), p["w_pw2"].astype(BF16), p["b_pw2"][None, :],
        p["w_out"].astype(BF16), p["ln1_g"][None, :], p["ln1_b"][None, :], wr_hi, wr_lo, b_route,
        gate_block=4, alpha=alpha, n_groups=n_groups, e_per_group=e_per_group,
        tm=tiles["post_tm"])

    bm = tiles["moe_bm"]
    n_plan_blocks = -(-2 * T // bm) + n_experts + MOE_GATHER_AHEAD + 1
    expert = rt[:, :2].astype(I32)
    eb_start, eb_count, src_plan, dst_plan = _slot_plan(expert, n_experts, bm, n_plan_blocks)
    y2 = _moe_experts(eb_start, eb_count, src_plan, dst_plan, hp, p["w1"], p["w3"], p["w2"],
                      n_tokens=T, bm=bm, nch=tiles["moe_nch"])
    return _combine(h, rt, y2, p["ln2_g"][None, :], p["ln2_b"][None, :], alpha=alpha,
                    tt=tiles["comb_tt"])


_PARAM_NAMES = ("w_in", "b_in", "w_o_attn", "w_dw", "b_dw", "conv_ln_g", "conv_ln_b", "w_pw2",
                "b_pw2", "w_out", "ln1_g", "ln1_b", "w_rg", "b_rg", "w_re", "b_re", "w1", "w3",
                "w2", "ln2_g", "ln2_b")

_TILES = dict(proj_tm=PROJ_TM, proj_tn=PROJ_TN, conv_ts=CONV_TS, conv_rt=CONV_RT,
              post_tm=POST_TM, comb_tt=COMB_TT, moe_bm=MOE_BM, moe_nch=MOE_NCH)


def _forward(x, params, *, n_heads=N_HEADS, head_dim=HEAD_DIM, moba_block=MOBA_BLOCK,
             moba_topk=MOBA_TOPK, n_groups=N_GROUPS, e_per_group=EXPERTS_PER_GROUP, tiles=None):
    tiles = dict(_TILES, **(tiles or {}))
    B, S, D = x.shape
    depth = params["w_in"].shape[0]
    x2 = x.reshape(B * S, D)
    for l in range(depth):
        p = {k: v[l] for k, v in params.items()}
        x2 = _layer(x2, p, batch=B, seq=S, depth=depth, n_heads=n_heads, head_dim=head_dim,
                    moba_block=moba_block, moba_topk=moba_topk, n_groups=n_groups,
                    e_per_group=e_per_group, tiles=tiles)
    return x2.reshape(B, S, D)


def kernel(x, w_in, b_in, w_o_attn, w_dw, b_dw, conv_ln_g, conv_ln_b, w_pw2, b_pw2, w_out, ln1_g,
           ln1_b, w_rg, b_rg, w_re, b_re, w1, w3, w2, ln2_g, ln2_b):
    params = dict(zip(_PARAM_NAMES, (w_in, b_in, w_o_attn, w_dw, b_dw, conv_ln_g, conv_ln_b, w_pw2,
                                     b_pw2, w_out, ln1_g, ln1_b, w_rg, b_rg, w_re, b_re, w1, w3,
                                     w2, ln2_g, ln2_b)))
    return _forward(x, params)
```

```python
import functools

import jax
import jax.numpy as jnp
from jax import lax
from jax.experimental import pallas as pl
from jax.experimental.pallas import tpu as pltpu

F32 = jnp.float32
BF16 = jnp.bfloat16
U32 = jnp.uint32
I32 = jnp.int32

N_HEADS = 16
HEAD_DIM = 128
ROPE_THETA = 10000.0
MOBA_BLOCK = 256
MOBA_TOPK = 3
N_GROUPS = 4
EXPERTS_PER_GROUP = 8
LN_EPS = 1e-5
LOG2_E = 1.4426950408889634

LANES = 128
SUBLANES = 8
VMEM_LIMIT = 56 * 1024 * 1024

PROJ_TM = 1024
PROJ_TN = 1024
ATTN_HPS = 4
ATTN_LOOKAHEAD = 2
ATTN_ONES_ROWS = 16
CONV_TS = 512
CONV_RT = 16
CONV_HALO = 32
CONV_TAP_GROUP = 4
POST_TM = 256
COMB_TT = 512
MOE_BM = 256
MOE_NCH = 4
MOE_OUT_CHUNK = 256
MOE_GATHER_AHEAD = 2


def _cparams(sem):
    return pltpu.CompilerParams(dimension_semantics=sem, vmem_limit_bytes=VMEM_LIMIT)


def _sigmoid(x):
    return 1.0 / (1.0 + jnp.exp(-x))


def _pack_bf16_pair(lo_f32, hi_f32):
    lo = lax.bitcast_convert_type(lo_f32.astype(BF16).astype(F32), U32) >> 16
    hi = lax.bitcast_convert_type(hi_f32.astype(BF16).astype(F32), U32)
    return hi | lo


def _unpack_lo(u):
    return lax.bitcast_convert_type(u << 16, F32)


def _unpack_hi(u):
    return lax.bitcast_convert_type(u & jnp.uint32(0xFFFF0000), F32)


def _store_tile_rows(ref, lead, x):
    n = x.shape[0]
    for s in range(SUBLANES):
        ref[(*lead, pl.ds(s, n, stride=SUBLANES), slice(None))] = x[:, s * LANES:(s + 1) * LANES]


def _load_tile_rows(ref, lead, n):
    return jnp.concatenate(
        [ref[(*lead, pl.ds(s, n, stride=SUBLANES), slice(None))] for s in range(SUBLANES)], axis=1)


def _inproj_body(x_ref, w_ref, wg_ref, b_ref, bg_ref, cos_ref, sin_ref, o_ref, xb_ref,
                 *, n_q, n_qk, n_qkv, n_glu, scale, head_dim):
    j = pl.program_id(1)
    tn = o_ref.shape[1]

    @pl.when(j == 0)
    def _cast():
        xb_ref[...] = x_ref[...].astype(BF16)

    def project(w, b):
        return jnp.dot(xb_ref[...], w[...], preferred_element_type=F32) + b[...]

    @pl.when(j < n_qk)
    def _rope():
        acc = project(w_ref, b_ref)
        s = jnp.where(j < n_q, scale, 1.0).astype(F32)
        cos = cos_ref[...] * s
        sin = sin_ref[...] * s
        for h in range(tn // head_dim):
            t = acc[:, h * head_dim:(h + 1) * head_dim]
            r = pltpu.roll(t, head_dim // 2, axis=1)
            o_ref[:, h * head_dim:(h + 1) * head_dim] = (t * cos + r * sin).astype(o_ref.dtype)

    @pl.when((j >= n_qk) & (j < n_qkv))
    def _plain():
        o_ref[...] = project(w_ref, b_ref).astype(o_ref.dtype)

    @pl.when((j >= n_qkv) & (j < n_qkv + n_glu))
    def _glu():
        a = project(w_ref, b_ref)
        g = project(wg_ref, bg_ref)
        o_ref[...] = (a * _sigmoid(g)).astype(o_ref.dtype)

    @pl.when(j >= n_qkv + n_glu)
    def _gate():
        o_ref[...] = _sigmoid(project(w_ref, b_ref)).astype(o_ref.dtype)


def _in_projection(x2, w_all, b_all, cos2, sin_s, *, seq, attn_w, conv_c, head_dim, tm, tn):
    T, D = x2.shape
    W = w_all.shape[1] - conv_c
    assert T % tm == 0 and seq % tm == 0 and W % tn == 0
    assert attn_w % tn == 0 and conv_c % tn == 0 and tn % head_dim == 0
    assert head_dim == LANES
    n_q = attn_w // tn
    n_glu = conv_c // tn
    n_qkv = 3 * n_q
    pos_tiles = seq // tm

    def main_idx(i, j):
        return (0, jnp.where(j < n_qkv + n_glu, j, j + n_glu))

    def g_idx(i, j):
        return (0, n_qkv + n_glu + jnp.clip(j - n_qkv, 0, n_glu - 1))

    body = functools.partial(_inproj_body, n_q=n_q, n_qk=2 * n_q, n_qkv=n_qkv, n_glu=n_glu,
                             scale=float(head_dim) ** -0.5 * LOG2_E, head_dim=head_dim)
    return pl.pallas_call(
        body,
        grid=(T // tm, W // tn),
        in_specs=[
            pl.BlockSpec((tm, D), lambda i, j: (i, 0)),
            pl.BlockSpec((D, tn), main_idx),
            pl.BlockSpec((D, tn), g_idx),
            pl.BlockSpec((1, tn), main_idx),
            pl.BlockSpec((1, tn), g_idx),
            pl.BlockSpec((tm, head_dim), lambda i, j: (i % pos_tiles, 0)),
            pl.BlockSpec((tm, head_dim), lambda i, j: (i % pos_tiles, 0)),
        ],
        out_specs=pl.BlockSpec((tm, tn), lambda i, j: (i, j)),
        out_shape=jax.ShapeDtypeStruct((T, W), BF16),
        scratch_shapes=[pltpu.VMEM((tm, D), BF16)],
        compiler_params=_cparams(("parallel", "arbitrary")),
        name="in_projection",
    )(x2, w_all, w_all, b_all, b_all, cos2, sin_s)


def _attn_prep(q, k, v, *, seq, blk):
    nb = seq // blk
    nbp = -(-nb // SUBLANES) * SUBLANES
    dh = q.shape[1]
    contract_last = (((1,), (1,)), ((), ()))
    kmean = jnp.mean(k.astype(F32).reshape(nb, blk, dh), axis=1)
    if nbp > nb:
        kmean = jnp.concatenate([kmean, jnp.zeros((nbp - nb, dh), F32)], axis=0)
    gate_t = lax.dot_general(kmean.astype(BF16), q, contract_last, preferred_element_type=F32)
    ones_rows = (lax.broadcasted_iota(I32, (ATTN_ONES_ROWS, seq), 0) == 0).astype(BF16)
    vt = jnp.concatenate([v.astype(F32).T.astype(BF16), ones_rows], axis=0)
    return gate_t, vt


def _attn_scores(q, k, gate_t, i, s_ref, *, blk, topk, nbp):
    contract_last = (((1,), (1,)), ((), ()))
    neg_inf = jnp.float32(-jnp.inf)
    sub = lax.broadcasted_iota(I32, (nbp, blk), 0)
    qi = q[i * blk:(i + 1) * blk]
    nk = (i + 1) * blk
    st = lax.dot_general(k[:nk], qi, contract_last, preferred_element_type=F32)
    if i > topk:
        gm = jnp.where(sub < i, gate_t[:, i * blk:(i + 1) * blk], neg_inf)
        rank = jnp.zeros((nbp, blk), I32)
        for other in range(i):
            g_other = gm[other:other + 1, :]
            beats = (g_other > gm) | ((g_other == gm) & (sub > other))
            rank = rank + beats.astype(I32)
        bias_t = jnp.where((sub < i) & (rank < topk), 0.0, neg_inf).astype(F32)
    m = None
    for n in range(i + 1):
        t = st[n * blk:(n + 1) * blk]
        if n == i:
            key_r = lax.broadcasted_iota(I32, (blk, blk), 0)
            qry_c = lax.broadcasted_iota(I32, (blk, blk), 1)
            t = jnp.where(key_r <= qry_c, t, neg_inf)
        elif i > topk:
            t = t + bias_t[n:n + 1, :]
        s_ref[n * blk:(n + 1) * blk, :] = t
        t_max = jnp.max(t, axis=0, keepdims=True)
        m = t_max if m is None else jnp.maximum(m, t_max)
    return m


def _attn_output(vt, m, i, s_ref, *, blk, dh):
    nk = (i + 1) * blk
    p = jnp.exp2(s_ref[:nk, :] - m).astype(BF16)
    ot = jnp.dot(vt[:, :nk], p, preferred_element_type=F32)
    return (ot[:dh] * (1.0 / ot[dh:dh + 1])).T


def _attn_body(q_ref, k_ref, v_ref, o_ref, s_scr, *, seq, blk, topk, dh):
    nb = seq // blk
    nbp = -(-nb // SUBLANES) * SUBLANES
    heads = []
    for hh in range(q_ref.shape[1] // dh):
        c0 = hh * dh
        q, k, v = q_ref[:, c0:c0 + dh], k_ref[:, c0:c0 + dh], v_ref[:, c0:c0 + dh]
        heads.append((c0, q, k) + _attn_prep(q, k, v, seq=seq, blk=blk))
    units = [(h, i) for h in range(len(heads)) for i in range(nb)]

    n_buf = s_scr.shape[0]
    ahead = n_buf - 1

    def scores(u):
        h, i = units[u]
        _, q, k, gate_t, _ = heads[h]
        return _attn_scores(q, k, gate_t, i, s_scr.at[u % n_buf], blk=blk, topk=topk, nbp=nbp)

    col_max = {u: scores(u) for u in range(min(ahead, len(units)))}
    for u, (h, i) in enumerate(units):
        if u + ahead < len(units):
            col_max[u + ahead] = scores(u + ahead)
        c0, _, _, _, vt = heads[h]
        o = _attn_output(vt, col_max.pop(u), i, s_scr.at[u % n_buf], blk=blk, dh=dh)
        o_ref[i * blk:(i + 1) * blk, c0:c0 + dh] = o.astype(o_ref.dtype)


def _moba_attention(u, *, batch, seq, n_heads, head_dim, blk, topk, hps):
    T = u.shape[0]
    assert seq % blk == 0 and blk % LANES == 0 and n_heads % hps == 0
    n_hb = n_heads // hps
    body = functools.partial(_attn_body, seq=seq, blk=blk, topk=topk, dh=head_dim)
    return pl.pallas_call(
        body,
        grid=(batch, n_hb),
        in_specs=[
            pl.BlockSpec((seq, hps * head_dim), lambda b, h: (b, h)),
            pl.BlockSpec((seq, hps * head_dim), lambda b, h: (b, n_hb + h)),
            pl.BlockSpec((seq, hps * head_dim), lambda b, h: (b, 2 * n_hb + h)),
        ],
        out_specs=pl.BlockSpec((seq, hps * head_dim), lambda b, h: (b, h)),
        out_shape=jax.ShapeDtypeStruct((T, n_heads * head_dim), BF16),
        scratch_shapes=[pltpu.VMEM((ATTN_LOOKAHEAD + 1, seq, blk), F32)],
        compiler_params=_cparams(("parallel", "parallel")),
        name="moba_attention",
    )(u, u, u)


def _conv_body(x_ref, w_ref, bdw_ref, g_ref, b_ref, o_ref, win_ref, conv_ref,
               *, width, ts, rt, halo, eps):
    sub = SUBLANES
    half = sub * LANES
    ng = conv_ref.shape[0]

    @pl.when(pl.program_id(1) == 0)
    def _zero_halo():
        win_ref[0:halo * sub, :] = jnp.zeros((halo * sub, LANES), U32)

    for s in range(sub):
        lo = x_ref[:, s * LANES:(s + 1) * LANES].astype(F32)
        hi = x_ref[:, half + s * LANES:half + (s + 1) * LANES].astype(F32)
        win_ref[pl.ds(halo * sub + s, ts, stride=sub), :] = _pack_bf16_pair(lo, hi)

    first = halo - (width - 1)

    def chunk(r, carry):
        t0 = r * rt
        acc_lo = jnp.broadcast_to(bdw_ref[0][None], (rt, sub, LANES))
        acc_hi = jnp.broadcast_to(bdw_ref[1][None], (rt, sub, LANES))
        for j0 in range(0, width, CONV_TAP_GROUP):
            part = None
            for j in range(j0, min(j0 + CONV_TAP_GROUP, width)):
                start = pl.multiple_of((t0 + first + j) * sub, sub)
                slab = pltpu.bitcast(win_ref[pl.ds(start, rt * sub), :], BF16)
                prod = slab.reshape(rt, 2 * sub, LANES) * pltpu.bitcast(w_ref[j], BF16)[None]
                part = prod if part is None else part + prod
            u = pltpu.bitcast(part.reshape(rt * 2 * sub, LANES), U32)
            acc_lo = acc_lo + _unpack_lo(u).reshape(rt, sub, LANES)
            acc_hi = acc_hi + _unpack_hi(u).reshape(rt, sub, LANES)
        rows = pl.ds(pl.multiple_of(t0 * sub, sub), rt * sub)
        conv_ref[0, rows, :] = acc_lo.reshape(rt * sub, LANES)
        conv_ref[1, rows, :] = acc_hi.reshape(rt * sub, LANES)
        return carry

    lax.fori_loop(0, ts // rt, chunk, 0)

    win_ref[0:halo * sub, :] = win_ref[ts * sub:(ts + halo) * sub, :]

    def channel_chunks():
        for g in range(ng):
            for s in range(sub):
                yield (g * sub + s) * LANES, conv_ref[g, pl.ds(s, ts, stride=sub), :]

    n = ng * sub * LANES
    total = jnp.zeros((ts, 1), F32)
    for _, y in channel_chunks():
        total = total + jnp.sum(y, axis=1, keepdims=True)
    mu = total * (1.0 / n)
    sq = jnp.zeros((ts, 1), F32)
    for _, y in channel_chunks():
        d = y - mu
        sq = sq + jnp.sum(d * d, axis=1, keepdims=True)
    inv = lax.rsqrt(sq * (1.0 / n) + eps)
    for c0, y in channel_chunks():
        z = (y - mu) * inv * g_ref[:, c0:c0 + LANES] + b_ref[:, c0:c0 + LANES]
        o_ref[:, c0:c0 + LANES] = (z * _sigmoid(z)).astype(o_ref.dtype)


def _conv_branch(u, w_dw, b_dw, ln_g, ln_b, *, batch, seq, conv_c, col_block, ts, rt, halo):
    T = u.shape[0]
    width = w_dw.shape[0]
    half = SUBLANES * LANES
    assert seq % ts == 0 and conv_c == 2 * half and ts % rt == 0
    assert halo >= width - 1 and ts >= halo
    w16 = lax.bitcast_convert_type(w_dw.astype(BF16), jnp.uint16).astype(U32)
    w_pk = (w16[:, :half] | (w16[:, half:] << 16)).reshape(width, SUBLANES, LANES)
    n_s = seq // ts
    body = functools.partial(_conv_body, width=width, ts=ts, rt=rt, halo=halo, eps=LN_EPS)
    return pl.pallas_call(
        body,
        grid=(batch, n_s),
        in_specs=[
            pl.BlockSpec((ts, conv_c), lambda b, s: (b * n_s + s, col_block)),
            pl.BlockSpec((width, SUBLANES, LANES), lambda b, s: (0, 0, 0)),
            pl.BlockSpec((2, SUBLANES, LANES), lambda b, s: (0, 0, 0)),
            pl.BlockSpec((1, conv_c), lambda b, s: (0, 0)),
            pl.BlockSpec((1, conv_c), lambda b, s: (0, 0)),
        ],
        out_specs=pl.BlockSpec((ts, conv_c), lambda b, s: (b * n_s + s, 0)),
        out_shape=jax.ShapeDtypeStruct((T, conv_c), BF16),
        scratch_shapes=[pltpu.VMEM(((halo + ts) * SUBLANES, LANES), U32),
                        pltpu.VMEM((2, ts * SUBLANES, LANES), F32)],
        compiler_params=_cparams(("parallel", "arbitrary")),
        name="conv_branch",
    )(u, w_pk, b_dw.reshape(2, SUBLANES, LANES), ln_g[None, :], ln_b[None, :])


def _post_body(o_ref, hc_ref, ga_ref, gb_ref, x_ref, wo_ref, wp_ref, bp_ref, wout_ref,
               g1_ref, b1_ref, wrh_ref, wrl_ref, br_ref, h_ref, hp_ref, rt_ref, m_scr, z_scr,
               *, alpha, eps, n_groups, e_per_group):
    i = pl.program_id(0)

    @pl.when(i == 0)
    def _no_previous_tiles():
        m_scr[1] = jnp.zeros(m_scr.shape[1:], BF16)
        z_scr[1] = jnp.zeros(z_scr.shape[1:], F32)
        z_scr[2] = jnp.zeros(z_scr.shape[1:], F32)

    ya = jnp.dot(o_ref[...], wo_ref[...], preferred_element_type=F32)

    z = z_scr[(i + 1) % 3]
    mu = jnp.mean(z, axis=1, keepdims=True)
    zc = z - mu
    var = jnp.mean(zc * zc, axis=1, keepdims=True)
    h = zc * lax.rsqrt(var + eps) * g1_ref[...] + b1_ref[...]
    h_ref[...] = h
    half = h.shape[1] // 2
    _store_tile_rows(hp_ref, (), _pack_bf16_pair(h[:, :half], h[:, half:]))

    yc = jnp.dot(hc_ref[...], wp_ref[...], preferred_element_type=F32) + bp_ref[...]

    h_hi = h.astype(BF16)
    h_lo = (h - h_hi.astype(F32)).astype(BF16)
    hi_terms = jnp.dot(h_hi, wrl_ref[...], preferred_element_type=F32)
    logits = (hi_terms[:, :LANES] + hi_terms[:, LANES:]
              + jnp.dot(h_lo, wrh_ref[...], preferred_element_type=F32) + br_ref[...])
    tm = logits.shape[0]
    lane = lax.broadcasted_iota(I32, (tm, LANES), 1)
    neg_inf = jnp.float32(-jnp.inf)
    big = jnp.int32(LANES)

    def first_argmax(vals):
        top = jnp.max(vals, axis=1, keepdims=True)
        idx = jnp.min(jnp.where(vals == top, lane, big), axis=1, keepdims=True)
        return top, idx

    gl = jnp.where(lane < n_groups, logits, neg_inf)
    gmax, grp = first_argmax(gl)
    grp_w = 1.0 / jnp.sum(jnp.exp(gl - gmax), axis=1, keepdims=True)
    lo_lane = n_groups + grp * e_per_group
    el = jnp.where((lane >= lo_lane) & (lane < lo_lane + e_per_group), logits, neg_inf)
    v1, i1 = first_argmax(el)
    v2, i2 = first_argmax(jnp.where(lane == i1, neg_inf, el))
    t = jnp.exp(v2 - v1)
    p1 = 1.0 / (1.0 + t)
    c1 = p1 * grp_w
    c2 = (t * p1) * grp_w
    e1 = (i1 - n_groups).astype(F32)
    e2 = (i2 - n_groups).astype(F32)
    rt_ref[...] = jnp.where(lane == 0, e1, jnp.where(lane == 1, e2,
                            jnp.where(lane == 2, c1, jnp.where(lane == 3, c2, 0.0))))

    prev = (i + 1) % 2
    z_scr[(i + 2) % 3] += jnp.dot(m_scr[prev], wout_ref[...], preferred_element_type=F32)
    m_scr[i % 2] = (ga_ref[...].astype(F32) * ya + gb_ref[...].astype(F32) * yc).astype(BF16)
    z_scr[i % 3] = alpha * x_ref[...]


def _post_block(o, hc, u, x2, wo, wp, bp, wout, g1, b1, wr_hi, wr_lo, br, *, gate_block,
                alpha, n_groups, e_per_group, tm):
    T, D = x2.shape
    A = o.shape[1]
    C = hc.shape[1]
    assert T % tm == 0 and D == 2 * SUBLANES * LANES, "packed rows are stored as one (8, 128) tile"
    assert n_groups * (1 + e_per_group) <= LANES
    const = lambda i: (0, 0)
    resident = lambda shape: pl.BlockSpec(shape, const, pipeline_mode=pl.Buffered(1))
    body = functools.partial(_post_body, alpha=alpha, eps=LN_EPS, n_groups=n_groups,
                             e_per_group=e_per_group)
    n = T // tm
    cur = lambda i: jnp.minimum(i, n - 1)
    prev = lambda i: jnp.maximum(i - 2, 0)
    return pl.pallas_call(
        body,
        grid=(n + 2,),
        in_specs=[
            pl.BlockSpec((tm, A), lambda i: (cur(i), 0)),
            pl.BlockSpec((tm, C), lambda i: (cur(i), 0)),
            pl.BlockSpec((tm, D), lambda i: (cur(i), gate_block)),
            pl.BlockSpec((tm, D), lambda i: (cur(i), gate_block + 1)),
            pl.BlockSpec((tm, D), lambda i: (cur(i), 0)),
            resident((A, D)),
            resident((C, D)),
            resident((1, D)),
            resident((D, D)),
            resident((1, D)),
            resident((1, D)),
            resident((D, LANES)),
            resident((D, 2 * LANES)),
            resident((1, LANES)),
        ],
        out_specs=[
            pl.BlockSpec((tm, D), lambda i: (prev(i), 0)),
            pl.BlockSpec((tm * SUBLANES, LANES), lambda i: (prev(i), 0)),
            pl.BlockSpec((tm, LANES), lambda i: (prev(i), 0)),
        ],
        out_shape=[
            jax.ShapeDtypeStruct((T, D), F32),
            jax.ShapeDtypeStruct((T * SUBLANES, LANES), U32),
            jax.ShapeDtypeStruct((T, LANES), F32),
        ],
        scratch_shapes=[pltpu.VMEM((2, tm, D), BF16), pltpu.VMEM((3, tm, D), F32)],
        compiler_params=_cparams(("arbitrary",)),
        name="merge_project_route",
    )(o, hc, u, u, x2, wo, wp, bp, wout, g1, b1, wr_hi, wr_lo, br)


def _row_copy(src_ref, src_row8, dst_ref, dst_row8, sem):
    aligned = lambda v: v if isinstance(v, int) else pl.multiple_of(v, SUBLANES)
    return pltpu.make_async_copy(src_ref.at[pl.ds(aligned(src_row8), SUBLANES)],
                                 dst_ref.at[pl.ds(aligned(dst_row8), SUBLANES)], sem)


def _moe_body(es_ref, ec_ref, src_ref, dst_ref, hp_ref, w1_ref, w3_ref, w2_ref, y2_ref,
              w1b, w3b, w2b, xbuf, ybuf, src, dst, gsem, ssem, src_sem, dst_sem,
              *, n_experts, nch, bm, spare_row0):
    g = pl.program_id(0)
    c = pl.program_id(1)
    fc = w1_ref.shape[1]
    half = SUBLANES * LANES
    n_total = es_ref[n_experts - 1] + ec_ref[n_experts - 1]
    n_xbuf = xbuf.shape[0]
    ahead = n_xbuf - 1

    def src_copy(q):
        return pltpu.make_async_copy(src_ref.at[q], src.at[q % 2], src_sem.at[q % 2])

    def dst_copy(q):
        return pltpu.make_async_copy(dst_ref.at[q + 1], dst.at[(q + 1) % 2], dst_sem.at[(q + 1) % 2])

    def gather_row(q, r):
        return _row_copy(hp_ref, src[q % 2, 0, r], xbuf.at[q % n_xbuf], r * SUBLANES,
                         gsem.at[q % n_xbuf])

    def scatter_row(q, r):
        return _row_copy(ybuf.at[(q + 3) % 3], r * SUBLANES, y2_ref, dst[(q + 1) % 2, 0, r],
                         ssem.at[(q + 3) % 3])

    def gather_wait(q):
        for r in range(bm):
            _row_copy(hp_ref, 0, xbuf.at[q % n_xbuf], r * SUBLANES, gsem.at[q % n_xbuf]).wait()

    def scatter_wait(q):
        for r in range(bm):
            _row_copy(ybuf.at[(q + 3) % 3], r * SUBLANES, y2_ref, 0, ssem.at[(q + 3) % 3]).wait()

    @pl.when((g == 0) & (c == 0))
    def _prime():
        ybuf[...] = jnp.zeros(ybuf.shape, U32)
        spare = [pltpu.make_async_copy(
            ybuf.at[s], y2_ref.at[pl.ds((spare_row0 + s * bm) * SUBLANES, bm * SUBLANES)], ssem.at[s])
            for s in range(2)]
        for cp in spare:
            cp.start()
        for cp in spare:
            cp.wait()
        dst_copy(-1).start()
        for q in range(ahead):
            src_copy(q).start()
            src_copy(q).wait()
            for r in range(bm):
                gather_row(q, r).start()
        src_copy(ahead).start()

    @pl.when(g < n_experts)
    def _cast_next_expert_chunk():
        slot = g % 2
        w1b[slot, c] = w1_ref[...].astype(BF16)
        w3b[slot, c] = w3_ref[...].astype(BF16)
        w2b[slot, pl.ds(pl.multiple_of(c * fc, fc), fc), :] = w2_ref[...].astype(BF16)

    @pl.when(g >= 1)
    def _compute_previous_expert_share():
        e = g - 1
        slot = e % 2
        nb = ec_ref[e]
        base = es_ref[e]
        n_out = w2b.shape[2] // MOE_OUT_CHUNK

        def block(r, carry):
            q = base + r
            src_copy(q + ahead).wait()
            src_copy(q + ahead + 1).start()
            dst_copy(q - 1).wait()
            dst_copy(q).start()
            gather_wait(q)

            @pl.when(q >= 2)
            def _():
                scatter_wait(q - 3)

            pending = []
            for rr in range(bm):
                pending.append(scatter_row(q - 1, rr))
                pending.append(gather_row(q + ahead, rr))
            per_gap = -(-len(pending) // (4 * nch + n_out))

            def issue_some():
                for _ in range(min(per_gap, len(pending))):
                    pending.pop(0).start()

            u = _load_tile_rows(xbuf, (q % n_xbuf,), bm)
            x_lo = _unpack_lo(u).astype(BF16)
            x_hi = _unpack_hi(u).astype(BF16)
            hs = []
            for cc in range(nch):
                a = jnp.dot(x_lo, w1b[slot, cc, :half, :], preferred_element_type=F32)
                issue_some()
                a = a + jnp.dot(x_hi, w1b[slot, cc, half:, :], preferred_element_type=F32)
                issue_some()
                b = jnp.dot(x_lo, w3b[slot, cc, :half, :], preferred_element_type=F32)
                issue_some()
                b = b + jnp.dot(x_hi, w3b[slot, cc, half:, :], preferred_element_type=F32)
                issue_some()
                hs.append((a * _sigmoid(a) * b).astype(BF16))
            hmid = jnp.concatenate(hs, axis=1)
            ys = []
            for oc in range(n_out):
                cols = slice(oc * MOE_OUT_CHUNK, (oc + 1) * MOE_OUT_CHUNK)
                ys.append(jnp.dot(hmid, w2b[slot, :, cols], preferred_element_type=F32))
                issue_some()
            assert not pending
            y = jnp.concatenate(ys, axis=1)
            _store_tile_rows(ybuf, (q % 3,), _pack_bf16_pair(y[:, :half], y[:, half:]))
            return carry

        lax.fori_loop((nb * c) // nch, (nb * (c + 1)) // nch, block, 0)

    @pl.when((g == n_experts) & (c == nch - 1))
    def _drain():
        dst_copy(n_total - 1).wait()
        for r in range(bm):
            scatter_row(n_total - 1, r).start()

        @pl.when(n_total >= 2)
        def _():
            scatter_wait(n_total - 3)

        @pl.when(n_total >= 1)
        def _():
            scatter_wait(n_total - 2)

        scatter_wait(n_total - 1)
        for k in range(ahead):
            gather_wait(n_total + k)
        src_copy(n_total + ahead).wait()


def _moe_experts(eb_start, eb_count, src_plan, dst_plan, hp, w1, w3, w2, *, n_tokens, bm, nch):
    n_out_rows = 2 * n_tokens + 2 * bm
    E, D, F = w1.shape
    assert D == 2 * SUBLANES * LANES and hp.shape[1] == LANES and D % MOE_OUT_CHUNK == 0
    assert F % nch == 0 and src_plan.shape[1:] == (1, bm) and dst_plan.shape[1:] == (1, bm)
    assert dst_plan.shape[0] == src_plan.shape[0] + 1
    fc = F // nch

    def w_in_idx(g, c, es, ec):
        return (jnp.minimum(g, E - 1), 0, jnp.where(g < E, c, nch - 1))

    def w_out_idx(g, c, es, ec):
        return (jnp.minimum(g, E - 1), jnp.where(g < E, c, nch - 1), 0)

    grid_spec = pltpu.PrefetchScalarGridSpec(
        num_scalar_prefetch=2,
        grid=(E + 1, nch),
        in_specs=[
            pl.BlockSpec(memory_space=pl.ANY),
            pl.BlockSpec(memory_space=pl.ANY),
            pl.BlockSpec(memory_space=pl.ANY),
            pl.BlockSpec((None, D, fc), w_in_idx),
            pl.BlockSpec((None, D, fc), w_in_idx),
            pl.BlockSpec((None, fc, D), w_out_idx),
        ],
        out_specs=pl.BlockSpec(memory_space=pl.ANY),
        scratch_shapes=[
            pltpu.VMEM((2, nch, D, fc), BF16),
            pltpu.VMEM((2, nch, D, fc), BF16),
            pltpu.VMEM((2, F, D), BF16),
            pltpu.VMEM((MOE_GATHER_AHEAD + 1, bm * SUBLANES, LANES), U32),
            pltpu.VMEM((3, bm * SUBLANES, LANES), U32),
            pltpu.SMEM((2, 1, bm), I32),
            pltpu.SMEM((2, 1, bm), I32),
            pltpu.SemaphoreType.DMA((MOE_GATHER_AHEAD + 1,)),
            pltpu.SemaphoreType.DMA((3,)),
            pltpu.SemaphoreType.DMA((2,)),
            pltpu.SemaphoreType.DMA((2,)),
        ],
    )
    body = functools.partial(_moe_body, n_experts=E, nch=nch, bm=bm, spare_row0=2 * n_tokens)
    return pl.pallas_call(
        body,
        grid_spec=grid_spec,
        out_shape=jax.ShapeDtypeStruct((n_out_rows * SUBLANES, LANES), U32),
        compiler_params=_cparams(("arbitrary", "arbitrary")),
        name="moe_experts",
    )(eb_start, eb_count, src_plan, dst_plan, hp, w1, w3, w2)


def _combine_body(h_ref, rt_ref, g_ref, b_ref, y0_ref, y1_ref, o_ref, *, alpha, eps):
    tt = h_ref.shape[0]
    u0 = _load_tile_rows(y0_ref, (), tt)
    u1 = _load_tile_rows(y1_ref, (), tt)
    half = u0.shape[1]
    c0 = rt_ref[:, 2:3]
    c1 = rt_ref[:, 3:4]
    z_lo = alpha * h_ref[:, :half] + (_unpack_lo(u0) * c0 + _unpack_lo(u1) * c1)
    z_hi = alpha * h_ref[:, half:] + (_unpack_hi(u0) * c0 + _unpack_hi(u1) * c1)
    n = 2 * half
    mu = (jnp.sum(z_lo, axis=1, keepdims=True) + jnp.sum(z_hi, axis=1, keepdims=True)) * (1.0 / n)
    d_lo = z_lo - mu
    d_hi = z_hi - mu
    var = (jnp.sum(d_lo * d_lo, axis=1, keepdims=True)
           + jnp.sum(d_hi * d_hi, axis=1, keepdims=True)) * (1.0 / n)
    inv = lax.rsqrt(var + eps)
    o_ref[:, :half] = d_lo * inv * g_ref[:, :half] + b_ref[:, :half]
    o_ref[:, half:] = d_hi * inv * g_ref[:, half:] + b_ref[:, half:]


def _combine(h, rt, y2, g2, b2, *, alpha, tt):
    T, D = h.shape
    assert T % tt == 0 and D == 2 * SUBLANES * LANES and y2.shape[1] == LANES
    body = functools.partial(_combine_body, alpha=alpha, eps=LN_EPS)
    return pl.pallas_call(
        body,
        grid=(T // tt,),
        in_specs=[
            pl.BlockSpec((tt, D), lambda i: (i, 0)),
            pl.BlockSpec((tt, LANES), lambda i: (i, 0)),
            pl.BlockSpec((1, D), lambda i: (0, 0)),
            pl.BlockSpec((1, D), lambda i: (0, 0)),
            pl.BlockSpec((tt * SUBLANES, LANES), lambda i: (i, 0)),
            pl.BlockSpec((tt * SUBLANES, LANES), lambda i: (i + T // tt, 0)),
        ],
        out_specs=pl.BlockSpec((tt, D), lambda i: (i, 0)),
        out_shape=jax.ShapeDtypeStruct((T, D), F32),
        compiler_params=_cparams(("parallel",)),
        name="moe_combine",
    )(h, rt, g2, b2, y2, y2)


def _invert_slots_body(slot_ref, spare_ref, dst_ref, *, n_tokens):
    pltpu.sync_copy(spare_ref, dst_ref)

    def place(a, carry):
        dst_ref[slot_ref[a]] = (a & 1) * n_tokens + lax.shift_right_logical(a, 1)
        return carry

    lax.fori_loop(0, 2 * n_tokens, place, 0, unroll=32)


def _invert_slots(slot, spare, *, n_tokens):
    body = functools.partial(_invert_slots_body, n_tokens=n_tokens)
    return pl.pallas_call(
        body,
        in_specs=[pl.BlockSpec(memory_space=pltpu.SMEM), pl.BlockSpec(memory_space=pl.ANY)],
        out_specs=pl.BlockSpec(memory_space=pltpu.SMEM),
        out_shape=jax.ShapeDtypeStruct(spare.shape, I32),
        name="invert_slots",
    )(slot, spare)


def _slot_plan(expert, n_experts, bm, n_plan_blocks):
    T = expert.shape[0]
    e_flat = expert.reshape(-1)
    onehot = (e_flat[:, None] == jnp.arange(n_experts, dtype=I32)[None, :]).astype(I32)
    csum = jnp.cumsum(onehot, axis=0)
    counts = csum[-1]
    eb_count = (counts + bm - 1) // bm
    eb_start = jnp.cumsum(eb_count) - eb_count
    slot = jnp.sum(onehot * (csum - 1 + (eb_start * bm)[None, :]), axis=1)
    n_slots = n_plan_blocks * bm
    p = jnp.arange(n_slots, dtype=I32)
    spare = 2 * T + ((p // bm) % 2) * bm + p % bm
    dst = _invert_slots(slot.astype(I32), spare, n_tokens=T)
    src = jnp.where(dst < 2 * T, dst % T, 0)
    dst = jnp.concatenate([2 * T + bm + jnp.arange(bm, dtype=I32), dst])
    to_blocks = lambda a: (a * SUBLANES).reshape(-1, 1, bm)
    return eb_start.astype(I32), eb_count.astype(I32), to_blocks(src), to_blocks(dst)


def _layer(x2, p, *, batch, seq, depth, n_heads, head_dim, moba_block, moba_topk,
           n_groups, e_per_group, tiles):
    T, D = x2.shape
    A = n_heads * head_dim
    C = p["w_dw"].shape[1]
    alpha = (2.0 * depth) ** 0.25
    tn = tiles["proj_tn"]
    assert A == C == D, "column-block addressing below assumes equal branch widths"

    w_in, b_in = p["w_in"], p["b_in"]

    half = head_dim // 2
    inv_freq = jnp.power(ROPE_THETA, -jnp.arange(half, dtype=F32) * (2.0 / head_dim))
    ang = jnp.arange(seq, dtype=F32)[:, None] * inv_freq[None, :]
    cos2 = jnp.concatenate([jnp.cos(ang), jnp.cos(ang)], axis=1)
    sin_s = jnp.concatenate([-jnp.sin(ang), jnp.sin(ang)], axis=1)

    u = _in_projection(x2, w_in.astype(BF16), b_in[None, :], cos2, sin_s, seq=seq, attn_w=A, conv_c=C,
                       head_dim=head_dim, tm=tiles["proj_tm"], tn=tn)
    o = _moba_attention(u, batch=batch, seq=seq, n_heads=n_heads, head_dim=head_dim,
                        blk=moba_block, topk=moba_topk, hps=ATTN_HPS)
    hc = _conv_branch(u, p["w_dw"], p["b_dw"], p["conv_ln_g"], p["conv_ln_b"], batch=batch,
                      seq=seq, conv_c=C, col_block=3, ts=tiles["conv_ts"], rt=tiles["conv_rt"],
                      halo=CONV_HALO)

    n_experts = n_groups * e_per_group
    n_route = n_groups + n_experts
    w_route = jnp.concatenate(
        [p["w_rg"], p["w_re"].transpose(1, 0, 2).reshape(D, n_experts),
         jnp.zeros((D, LANES - n_route), F32)], axis=1)
    b_route = jnp.concatenate(
        [p["b_rg"], p["b_re"].reshape(n_experts), jnp.zeros((LANES - n_route,), F32)])[None, :]
    wr_hi = w_route.astype(BF16)
    wr_lo = jnp.concatenate([wr_hi, (w_route - wr_hi.astype(F32)).astype(BF16)], axis=1)

    h, hp, rt = _post_block(
        o, hc, u, x2, p["w_o_attn"].astype(BF16), p["w_pw2"].astype(BF16), p["b_pw2"][None, :],
        p["w_out"].astype(BF16), p["ln1_g"][None, :], p["ln1_b"][None, :], wr_hi, wr_lo, b_route,
        gate_block=4, alpha=alpha, n_groups=n_groups, e_per_group=e_per_group,
        tm=tiles["post_tm"])

    bm = tiles["moe_bm"]
    n_plan_blocks = -(-2 * T // bm) + n_experts + MOE_GATHER_AHEAD + 1
    expert = rt[:, :2].astype(I32)
    eb_start, eb_count, src_plan, dst_plan = _slot_plan(expert, n_experts, bm, n_plan_blocks)
    y2 = _moe_experts(eb_start, eb_count, src_plan, dst_plan, hp, p["w1"], p["w3"], p["w2"],
                      n_tokens=T, bm=bm, nch=tiles["moe_nch"])
    return _combine(h, rt, y2, p["ln2_g"][None, :], p["ln2_b"][None, :], alpha=alpha,
                    tt=tiles["comb_tt"])


_PARAM_NAMES = ("w_in", "b_in", "w_o_attn", "w_dw", "b_dw", "conv_ln_g", "conv_ln_b", "w_pw2",
                "b_pw2", "w_out", "ln1_g", "ln1_b", "w_rg", "b_rg", "w_re", "b_re", "w1", "w3",
                "w2", "ln2_g", "ln2_b")

_TILES = dict(proj_tm=PROJ_TM, proj_tn=PROJ_TN, conv_ts=CONV_TS, conv_rt=CONV_RT,
              post_tm=POST_TM, comb_tt=COMB_TT, moe_bm=MOE_BM, moe_nch=MOE_NCH)


def _forward(x, params, *, n_heads=N_HEADS, head_dim=HEAD_DIM, moba_block=MOBA_BLOCK,
             moba_topk=MOBA_TOPK, n_groups=N_GROUPS, e_per_group=EXPERTS_PER_GROUP, tiles=None):
    tiles = dict(_TILES, **(tiles or {}))
    B, S, D = x.shape
    depth = params["w_in"].shape[0]
    x2 = x.reshape(B * S, D)
    for l in range(depth):
        p = {k: v[l] for k, v in params.items()}
        x2 = _layer(x2, p, batch=B, seq=S, depth=depth, n_heads=n_heads, head_dim=head_dim,
                    moba_block=moba_block, moba_topk=moba_topk, n_groups=n_groups,
                    e_per_group=e_per_group, tiles=tiles)
    return x2.reshape(B, S, D)


def kernel(x, w_in, b_in, w_o_attn, w_dw, b_dw, conv_ln_g, conv_ln_b, w_pw2, b_pw2, w_out, ln1_g,
           ln1_b, w_rg, b_rg, w_re, b_re, w1, w3, w2, ln2_g, ln2_b):
    params = dict(zip(_PARAM_NAMES, (w_in, b_in, w_o_attn, w_dw, b_dw, conv_ln_g, conv_ln_b, w_pw2,
                                     b_pw2, w_out, ln1_g, ln1_b, w_rg, b_rg, w_re, b_re, w1, w3,
                                     w2, ln2_g, ln2_b)))
    return _forward(x, params)
```

```python
import functools

import jax
import jax.numpy as jnp
from jax import lax
from jax.experimental import pallas as pl
from jax.experimental.pallas import tpu as pltpu

F32 = jnp.float32
BF16 = jnp.bfloat16
U32 = jnp.uint32
I32 = jnp.int32

N_HEADS = 16
HEAD_DIM = 128
ROPE_THETA = 10000.0
MOBA_BLOCK = 256
MOBA_TOPK = 3
N_GROUPS = 4
EXPERTS_PER_GROUP = 8
LN_EPS = 1e-5
LOG2_E = 1.4426950408889634

LANES = 128
SUBLANES = 8
VMEM_LIMIT = 56 * 1024 * 1024

PROJ_TM = 1024
PROJ_TN = 1024
ATTN_HPS = 4
ATTN_LOOKAHEAD = 2
ATTN_ONES_ROWS = 16
CONV_TS = 512
CONV_RT = 16
CONV_HALO = 32
CONV_TAP_GROUP = 4
POST_TM = 256
COMB_TT = 512
MOE_BM = 256
MOE_NCH = 4
MOE_OUT_CHUNK = 256
MOE_GATHER_AHEAD = 2


def _cparams(sem):
    return pltpu.CompilerParams(dimension_semantics=sem, vmem_limit_bytes=VMEM_LIMIT)


def _sigmoid(x):
    return 1.0 / (1.0 + jnp.exp(-x))


def _pack_bf16_pair(lo_f32, hi_f32):
    lo = lax.bitcast_convert_type(lo_f32.astype(BF16).astype(F32), U32) >> 16
    hi = lax.bitcast_convert_type(hi_f32.astype(BF16).astype(F32), U32)
    return hi | lo


def _unpack_lo(u):
    return lax.bitcast_convert_type(u << 16, F32)


def _unpack_hi(u):
    return lax.bitcast_convert_type(u & jnp.uint32(0xFFFF0000), F32)


def _store_tile_rows(ref, lead, x):
    n = x.shape[0]
    for s in range(SUBLANES):
        ref[(*lead, pl.ds(s, n, stride=SUBLANES), slice(None))] = x[:, s * LANES:(s + 1) * LANES]


def _load_tile_rows(ref, lead, n):
    return jnp.concatenate(
        [ref[(*lead, pl.ds(s, n, stride=SUBLANES), slice(None))] for s in range(SUBLANES)], axis=1)


def _inproj_body(x_ref, w_ref, wg_ref, b_ref, bg_ref, cos_ref, sin_ref, o_ref, xb_ref,
                 *, n_q, n_qk, n_qkv, n_glu, scale, head_dim):
    j = pl.program_id(1)
    tn = o_ref.shape[1]

    @pl.when(j == 0)
    def _cast():
        xb_ref[...] = x_ref[...].astype(BF16)

    def project(w, b):
        return jnp.dot(xb_ref[...], w[...], preferred_element_type=F32) + b[...]

    @pl.when(j < n_qk)
    def _rope():
        acc = project(w_ref, b_ref)
        s = jnp.where(j < n_q, scale, 1.0).astype(F32)
        cos = cos_ref[...] * s
        sin = sin_ref[...] * s
        for h in range(tn // head_dim):
            t = acc[:, h * head_dim:(h + 1) * head_dim]
            r = pltpu.roll(t, head_dim // 2, axis=1)
            o_ref[:, h * head_dim:(h + 1) * head_dim] = (t * cos + r * sin).astype(o_ref.dtype)

    @pl.when((j >= n_qk) & (j < n_qkv))
    def _plain():
        o_ref[...] = project(w_ref, b_ref).astype(o_ref.dtype)

    @pl.when((j >= n_qkv) & (j < n_qkv + n_glu))
    def _glu():
        a = project(w_ref, b_ref)
        g = project(wg_ref, bg_ref)
        o_ref[...] = (a * _sigmoid(g)).astype(o_ref.dtype)

    @pl.when(j >= n_qkv + n_glu)
    def _gate():
        o_ref[...] = _sigmoid(project(w_ref, b_ref)).astype(o_ref.dtype)


def _in_projection(x2, w_all, b_all, cos2, sin_s, *, seq, attn_w, conv_c, head_dim, tm, tn):
    T, D = x2.shape
    W = w_all.shape[1] - conv_c
    assert T % tm == 0 and seq % tm == 0 and W % tn == 0
    assert attn_w % tn == 0 and conv_c % tn == 0 and tn % head_dim == 0
    assert head_dim == LANES
    n_q = attn_w // tn
    n_glu = conv_c // tn
    n_qkv = 3 * n_q
    pos_tiles = seq // tm

    def main_idx(i, j):
        return (0, jnp.where(j < n_qkv + n_glu, j, j + n_glu))

    def g_idx(i, j):
        return (0, n_qkv + n_glu + jnp.clip(j - n_qkv, 0, n_glu - 1))

    body = functools.partial(_inproj_body, n_q=n_q, n_qk=2 * n_q, n_qkv=n_qkv, n_glu=n_glu,
                             scale=float(head_dim) ** -0.5 * LOG2_E, head_dim=head_dim)
    return pl.pallas_call(
        body,
        grid=(T // tm, W // tn),
        in_specs=[
            pl.BlockSpec((tm, D), lambda i, j: (i, 0)),
            pl.BlockSpec((D, tn), main_idx),
            pl.BlockSpec((D, tn), g_idx),
            pl.BlockSpec((1, tn), main_idx),
            pl.BlockSpec((1, tn), g_idx),
            pl.BlockSpec((tm, head_dim), lambda i, j: (i % pos_tiles, 0)),
            pl.BlockSpec((tm, head_dim), lambda i, j: (i % pos_tiles, 0)),
        ],
        out_specs=pl.BlockSpec((tm, tn), lambda i, j: (i, j)),
        out_shape=jax.ShapeDtypeStruct((T, W), BF16),
        scratch_shapes=[pltpu.VMEM((tm, D), BF16)],
        compiler_params=_cparams(("parallel", "arbitrary")),
        name="in_projection",
    )(x2, w_all, w_all, b_all, b_all, cos2, sin_s)


def _attn_prep(q, k, v, *, seq, blk):
    nb = seq // blk
    nbp = -(-nb // SUBLANES) * SUBLANES
    dh = q.shape[1]
    contract_last = (((1,), (1,)), ((), ()))
    kmean = jnp.mean(k.astype(F32).reshape(nb, blk, dh), axis=1)
    if nbp > nb:
        kmean = jnp.concatenate([kmean, jnp.zeros((nbp - nb, dh), F32)], axis=0)
    gate_t = lax.dot_general(kmean.astype(BF16), q, contract_last, preferred_element_type=F32)
    ones_rows = (lax.broadcasted_iota(I32, (ATTN_ONES_ROWS, seq), 0) == 0).astype(BF16)
    vt = jnp.concatenate([v.T, ones_rows], axis=0)
    return gate_t, vt


def _attn_scores(q, k, gate_t, i, s_ref, *, blk, topk, nbp):
    contract_last = (((1,), (1,)), ((), ()))
    neg_inf = jnp.float32(-jnp.inf)
    sub = lax.broadcasted_iota(I32, (nbp, blk), 0)
    qi = q[i * blk:(i + 1) * blk]
    nk = (i + 1) * blk
    st = lax.dot_general(k[:nk], qi, contract_last, preferred_element_type=F32)
    if i > topk:
        gm = jnp.where(sub < i, gate_t[:, i * blk:(i + 1) * blk], neg_inf)
        rank = jnp.zeros((nbp, blk), I32)
        for other in range(i):
            g_other = gm[other:other + 1, :]
            beats = (g_other > gm) | ((g_other == gm) & (sub > other))
            rank = rank + beats.astype(I32)
        bias_t = jnp.where((sub < i) & (rank < topk), 0.0, neg_inf).astype(F32)
    m = None
    for n in range(i + 1):
        t = st[n * blk:(n + 1) * blk]
        if n == i:
            key_r = lax.broadcasted_iota(I32, (blk, blk), 0)
            qry_c = lax.broadcasted_iota(I32, (blk, blk), 1)
            t = jnp.where(key_r <= qry_c, t, neg_inf)
        elif i > topk:
            t = t + bias_t[n:n + 1, :]
        s_ref[n * blk:(n + 1) * blk, :] = t
        t_max = jnp.max(t, axis=0, keepdims=True)
        m = t_max if m is None else jnp.maximum(m, t_max)
    return m


def _attn_output(vt, m, i, s_ref, *, blk, dh):
    nk = (i + 1) * blk
    p = jnp.exp2(s_ref[:nk, :] - m).astype(BF16)
    ot = jnp.dot(vt[:, :nk], p, preferred_element_type=F32)
    return (ot[:dh] * (1.0 / ot[dh:dh + 1])).T


def _attn_body(q_ref, k_ref, v_ref, o_ref, s_scr, *, seq, blk, topk, dh):
    nb = seq // blk
    nbp = -(-nb // SUBLANES) * SUBLANES
    heads = []
    for hh in range(q_ref.shape[1] // dh):
        c0 = hh * dh
        q, k, v = q_ref[:, c0:c0 + dh], k_ref[:, c0:c0 + dh], v_ref[:, c0:c0 + dh]
        heads.append((c0, q, k) + _attn_prep(q, k, v, seq=seq, blk=blk))
    units = [(h, i) for h in range(len(heads)) for i in range(nb)]

    n_buf = s_scr.shape[0]
    ahead = n_buf - 1

    def scores(u):
        h, i = units[u]
        _, q, k, gate_t, _ = heads[h]
        return _attn_scores(q, k, gate_t, i, s_scr.at[u % n_buf], blk=blk, topk=topk, nbp=nbp)

    col_max = {u: scores(u) for u in range(min(ahead, len(units)))}
    for u, (h, i) in enumerate(units):
        if u + ahead < len(units):
            col_max[u + ahead] = scores(u + ahead)
        c0, _, _, _, vt = heads[h]
        o = _attn_output(vt, col_max.pop(u), i, s_scr.at[u % n_buf], blk=blk, dh=dh)
        o_ref[i * blk:(i + 1) * blk, c0:c0 + dh] = o.astype(o_ref.dtype)


def _moba_attention(u, *, batch, seq, n_heads, head_dim, blk, topk, hps):
    T = u.shape[0]
    assert seq % blk == 0 and blk % LANES == 0 and n_heads % hps == 0
    n_hb = n_heads // hps
    body = functools.partial(_attn_body, seq=seq, blk=blk, topk=topk, dh=head_dim)
    return pl.pallas_call(
        body,
        grid=(batch, n_hb),
        in_specs=[
            pl.BlockSpec((seq, hps * head_dim), lambda b, h: (b, h)),
            pl.BlockSpec((seq, hps * head_dim), lambda b, h: (b, n_hb + h)),
            pl.BlockSpec((seq, hps * head_dim), lambda b, h: (b, 2 * n_hb + h)),
        ],
        out_specs=pl.BlockSpec((seq, hps * head_dim), lambda b, h: (b, h)),
        out_shape=jax.ShapeDtypeStruct((T, n_heads * head_dim), BF16),
        scratch_shapes=[pltpu.VMEM((ATTN_LOOKAHEAD + 1, seq, blk), F32)],
        compiler_params=_cparams(("parallel", "parallel")),
        name="moba_attention",
    )(u, u, u)


def _conv_body(x_ref, w_ref, bdw_ref, g_ref, b_ref, o_ref, win_ref, conv_ref,
               *, width, ts, rt, halo, eps):
    sub = SUBLANES
    half = sub * LANES
    ng = conv_ref.shape[0]

    @pl.when(pl.program_id(1) == 0)
    def _zero_halo():
        win_ref[0:halo * sub, :] = jnp.zeros((halo * sub, LANES), U32)

    for s in range(sub):
        lo = x_ref[:, s * LANES:(s + 1) * LANES].astype(F32)
        hi = x_ref[:, half + s * LANES:half + (s + 1) * LANES].astype(F32)
        win_ref[pl.ds(halo * sub + s, ts, stride=sub), :] = _pack_bf16_pair(lo, hi)

    first = halo - (width - 1)

    def chunk(r, carry):
        t0 = r * rt
        acc_lo = jnp.broadcast_to(bdw_ref[0][None], (rt, sub, LANES))
        acc_hi = jnp.broadcast_to(bdw_ref[1][None], (rt, sub, LANES))
        for j0 in range(0, width, CONV_TAP_GROUP):
            part = None
            for j in range(j0, min(j0 + CONV_TAP_GROUP, width)):
                start = pl.multiple_of((t0 + first + j) * sub, sub)
                slab = pltpu.bitcast(win_ref[pl.ds(start, rt * sub), :], BF16)
                prod = slab.reshape(rt, 2 * sub, LANES) * pltpu.bitcast(w_ref[j], BF16)[None]
                part = prod if part is None else part + prod
            u = pltpu.bitcast(part.reshape(rt * 2 * sub, LANES), U32)
            acc_lo = acc_lo + _unpack_lo(u).reshape(rt, sub, LANES)
            acc_hi = acc_hi + _unpack_hi(u).reshape(rt, sub, LANES)
        rows = pl.ds(pl.multiple_of(t0 * sub, sub), rt * sub)
        conv_ref[0, rows, :] = acc_lo.reshape(rt * sub, LANES)
        conv_ref[1, rows, :] = acc_hi.reshape(rt * sub, LANES)
        return carry

    lax.fori_loop(0, ts // rt, chunk, 0)

    win_ref[0:halo * sub, :] = win_ref[ts * sub:(ts + halo) * sub, :]

    def channel_chunks():
        for g in range(ng):
            for s in range(sub):
                yield (g * sub + s) * LANES, conv_ref[g, pl.ds(s, ts, stride=sub), :]

    n = ng * sub * LANES
    total = jnp.zeros((ts, 1), F32)
    for _, y in channel_chunks():
        total = total + jnp.sum(y, axis=1, keepdims=True)
    mu = total * (1.0 / n)
    sq = jnp.zeros((ts, 1), F32)
    for _, y in channel_chunks():
        d = y - mu
        sq = sq + jnp.sum(d * d, axis=1, keepdims=True)
    inv = lax.rsqrt(sq * (1.0 / n) + eps)
    for c0, y in channel_chunks():
        z = (y - mu) * inv * g_ref[:, c0:c0 + LANES] + b_ref[:, c0:c0 + LANES]
        o_ref[:, c0:c0 + LANES] = (z * _sigmoid(z)).astype(o_ref.dtype)


def _conv_branch(u, w_dw, b_dw, ln_g, ln_b, *, batch, seq, conv_c, col_block, ts, rt, halo):
    T = u.shape[0]
    width = w_dw.shape[0]
    half = SUBLANES * LANES
    assert seq % ts == 0 and conv_c == 2 * half and ts % rt == 0
    assert halo >= width - 1 and ts >= halo
    w16 = lax.bitcast_convert_type(w_dw.astype(BF16), jnp.uint16).astype(U32)
    w_pk = (w16[:, :half] | (w16[:, half:] << 16)).reshape(width, SUBLANES, LANES)
    n_s = seq // ts
    body = functools.partial(_conv_body, width=width, ts=ts, rt=rt, halo=halo, eps=LN_EPS)
    return pl.pallas_call(
        body,
        grid=(batch, n_s),
        in_specs=[
            pl.BlockSpec((ts, conv_c), lambda b, s: (b * n_s + s, col_block)),
            pl.BlockSpec((width, SUBLANES, LANES), lambda b, s: (0, 0, 0)),
            pl.BlockSpec((2, SUBLANES, LANES), lambda b, s: (0, 0, 0)),
            pl.BlockSpec((1, conv_c), lambda b, s: (0, 0)),
            pl.BlockSpec((1, conv_c), lambda b, s: (0, 0)),
        ],
        out_specs=pl.BlockSpec((ts, conv_c), lambda b, s: (b * n_s + s, 0)),
        out_shape=jax.ShapeDtypeStruct((T, conv_c), BF16),
        scratch_shapes=[pltpu.VMEM(((halo + ts) * SUBLANES, LANES), U32),
                        pltpu.VMEM((2, ts * SUBLANES, LANES), F32)],
        compiler_params=_cparams(("parallel", "arbitrary")),
        name="conv_branch",
    )(u, w_pk, b_dw.reshape(2, SUBLANES, LANES), ln_g[None, :], ln_b[None, :])


def _post_body(o_ref, hc_ref, ga_ref, gb_ref, x_ref, wo_ref, wp_ref, bp_ref, wout_ref,
               g1_ref, b1_ref, wrh_ref, wrl_ref, br_ref, h_ref, hp_ref, rt_ref, m_scr, z_scr,
               *, alpha, eps, n_groups, e_per_group):
    i = pl.program_id(0)

    @pl.when(i == 0)
    def _no_previous_tiles():
        m_scr[1] = jnp.zeros(m_scr.shape[1:], BF16)
        z_scr[1] = jnp.zeros(z_scr.shape[1:], F32)
        z_scr[2] = jnp.zeros(z_scr.shape[1:], F32)

    ya = jnp.dot(o_ref[...], wo_ref[...], preferred_element_type=F32)

    z = z_scr[(i + 1) % 3]
    mu = jnp.mean(z, axis=1, keepdims=True)
    zc = z - mu
    var = jnp.mean(zc * zc, axis=1, keepdims=True)
    h = zc * lax.rsqrt(var + eps) * g1_ref[...] + b1_ref[...]
    h_ref[...] = h
    half = h.shape[1] // 2
    _store_tile_rows(hp_ref, (), _pack_bf16_pair(h[:, :half], h[:, half:]))

    yc = jnp.dot(hc_ref[...], wp_ref[...], preferred_element_type=F32) + bp_ref[...]

    h_hi = h.astype(BF16)
    h_lo = (h - h_hi.astype(F32)).astype(BF16)
    hi_terms = jnp.dot(h_hi, wrl_ref[...], preferred_element_type=F32)
    logits = (hi_terms[:, :LANES] + hi_terms[:, LANES:]
              + jnp.dot(h_lo, wrh_ref[...], preferred_element_type=F32) + br_ref[...])
    tm = logits.shape[0]
    lane = lax.broadcasted_iota(I32, (tm, LANES), 1)
    neg_inf = jnp.float32(-jnp.inf)
    big = jnp.int32(LANES)

    def first_argmax(vals):
        top = jnp.max(vals, axis=1, keepdims=True)
        idx = jnp.min(jnp.where(vals == top, lane, big), axis=1, keepdims=True)
        return top, idx

    gl = jnp.where(lane < n_groups, logits, neg_inf)
    gmax, grp = first_argmax(gl)
    grp_w = 1.0 / jnp.sum(jnp.exp(gl - gmax), axis=1, keepdims=True)
    lo_lane = n_groups + grp * e_per_group
    el = jnp.where((lane >= lo_lane) & (lane < lo_lane + e_per_group), logits, neg_inf)
    v1, i1 = first_argmax(el)
    v2, i2 = first_argmax(jnp.where(lane == i1, neg_inf, el))
    t = jnp.exp(v2 - v1)
    p1 = 1.0 / (1.0 + t)
    c1 = p1 * grp_w
    c2 = (t * p1) * grp_w
    e1 = (i1 - n_groups).astype(F32)
    e2 = (i2 - n_groups).astype(F32)
    rt_ref[...] = jnp.where(lane == 0, e1, jnp.where(lane == 1, e2,
                            jnp.where(lane == 2, c1, jnp.where(lane == 3, c2, 0.0))))

    prev = (i + 1) % 2
    z_scr[(i + 2) % 3] += jnp.dot(m_scr[prev], wout_ref[...], preferred_element_type=F32)
    m_scr[i % 2] = (ga_ref[...].astype(F32) * ya + gb_ref[...].astype(F32) * yc).astype(BF16)
    z_scr[i % 3] = alpha * x_ref[...]


def _post_block(o, hc, u, x2, wo, wp, bp, wout, g1, b1, wr_hi, wr_lo, br, *, gate_block,
                alpha, n_groups, e_per_group, tm):
    T, D = x2.shape
    A = o.shape[1]
    C = hc.shape[1]
    assert T % tm == 0 and D == 2 * SUBLANES * LANES, "packed rows are stored as one (8, 128) tile"
    assert n_groups * (1 + e_per_group) <= LANES
    const = lambda i: (0, 0)
    resident = lambda shape: pl.BlockSpec(shape, const, pipeline_mode=pl.Buffered(1))
    body = functools.partial(_post_body, alpha=alpha, eps=LN_EPS, n_groups=n_groups,
                             e_per_group=e_per_group)
    n = T // tm
    cur = lambda i: jnp.minimum(i, n - 1)
    prev = lambda i: jnp.maximum(i - 2, 0)
    return pl.pallas_call(
        body,
        grid=(n + 2,),
        in_specs=[
            pl.BlockSpec((tm, A), lambda i: (cur(i), 0)),
            pl.BlockSpec((tm, C), lambda i: (cur(i), 0)),
            pl.BlockSpec((tm, D), lambda i: (cur(i), gate_block)),
            pl.BlockSpec((tm, D), lambda i: (cur(i), gate_block + 1)),
            pl.BlockSpec((tm, D), lambda i: (cur(i), 0)),
            resident((A, D)),
            resident((C, D)),
            resident((1, D)),
            resident((D, D)),
            resident((1, D)),
            resident((1, D)),
            resident((D, LANES)),
            resident((D, 2 * LANES)),
            resident((1, LANES)),
        ],
        out_specs=[
            pl.BlockSpec((tm, D), lambda i: (prev(i), 0)),
            pl.BlockSpec((tm * SUBLANES, LANES), lambda i: (prev(i), 0)),
            pl.BlockSpec((tm, LANES), lambda i: (prev(i), 0)),
        ],
        out_shape=[
            jax.ShapeDtypeStruct((T, D), F32),
            jax.ShapeDtypeStruct((T * SUBLANES, LANES), U32),
            jax.ShapeDtypeStruct((T, LANES), F32),
        ],
        scratch_shapes=[pltpu.VMEM((2, tm, D), BF16), pltpu.VMEM((3, tm, D), F32)],
        compiler_params=_cparams(("arbitrary",)),
        name="merge_project_route",
    )(o, hc, u, u, x2, wo, wp, bp, wout, g1, b1, wr_hi, wr_lo, br)


def _row_copy(src_ref, src_row8, dst_ref, dst_row8, sem):
    aligned = lambda v: v if isinstance(v, int) else pl.multiple_of(v, SUBLANES)
    return pltpu.make_async_copy(src_ref.at[pl.ds(aligned(src_row8), SUBLANES)],
                                 dst_ref.at[pl.ds(aligned(dst_row8), SUBLANES)], sem)


def _moe_body(es_ref, ec_ref, src_ref, dst_ref, hp_ref, w1_ref, w3_ref, w2_ref, y2_ref,
              w1b, w3b, w2b, xbuf, ybuf, src, dst, gsem, ssem, src_sem, dst_sem,
              *, n_experts, nch, bm, spare_row0):
    g = pl.program_id(0)
    c = pl.program_id(1)
    fc = w1_ref.shape[1]
    half = SUBLANES * LANES
    n_total = es_ref[n_experts - 1] + ec_ref[n_experts - 1]
    n_xbuf = xbuf.shape[0]
    ahead = n_xbuf - 1

    def src_copy(q):
        return pltpu.make_async_copy(src_ref.at[q], src.at[q % 2], src_sem.at[q % 2])

    def dst_copy(q):
        return pltpu.make_async_copy(dst_ref.at[q + 1], dst.at[(q + 1) % 2], dst_sem.at[(q + 1) % 2])

    def gather_row(q, r):
        return _row_copy(hp_ref, src[q % 2, 0, r], xbuf.at[q % n_xbuf], r * SUBLANES,
                         gsem.at[q % n_xbuf])

    def scatter_row(q, r):
        return _row_copy(ybuf.at[(q + 3) % 3], r * SUBLANES, y2_ref, dst[(q + 1) % 2, 0, r],
                         ssem.at[(q + 3) % 3])

    def gather_wait(q):
        for r in range(bm):
            _row_copy(hp_ref, 0, xbuf.at[q % n_xbuf], r * SUBLANES, gsem.at[q % n_xbuf]).wait()

    def scatter_wait(q):
        for r in range(bm):
            _row_copy(ybuf.at[(q + 3) % 3], r * SUBLANES, y2_ref, 0, ssem.at[(q + 3) % 3]).wait()

    @pl.when((g == 0) & (c == 0))
    def _prime():
        ybuf[...] = jnp.zeros(ybuf.shape, U32)
        spare = [pltpu.make_async_copy(
            ybuf.at[s], y2_ref.at[pl.ds((spare_row0 + s * bm) * SUBLANES, bm * SUBLANES)], ssem.at[s])
            for s in range(2)]
        for cp in spare:
            cp.start()
        for cp in spare:
            cp.wait()
        dst_copy(-1).start()
        for q in range(ahead):
            src_copy(q).start()
            src_copy(q).wait()
            for r in range(bm):
                gather_row(q, r).start()
        src_copy(ahead).start()

    @pl.when(g < n_experts)
    def _cast_next_expert_chunk():
        slot = g % 2
        w1b[slot, c] = w1_ref[...].astype(BF16)
        w3b[slot, c] = w3_ref[...].astype(BF16)
        w2b[slot, pl.ds(pl.multiple_of(c * fc, fc), fc), :] = w2_ref[...].astype(BF16)

    @pl.when(g >= 1)
    def _compute_previous_expert_share():
        e = g - 1
        slot = e % 2
        nb = ec_ref[e]
        base = es_ref[e]
        n_out = w2b.shape[2] // MOE_OUT_CHUNK

        def block(r, carry):
            q = base + r
            src_copy(q + ahead).wait()
            src_copy(q + ahead + 1).start()
            dst_copy(q - 1).wait()
            dst_copy(q).start()
            gather_wait(q)

            @pl.when(q >= 2)
            def _():
                scatter_wait(q - 3)

            pending = []
            for rr in range(bm):
                pending.append(scatter_row(q - 1, rr))
                pending.append(gather_row(q + ahead, rr))
            per_gap = -(-len(pending) // (4 * nch + n_out))

            def issue_some():
                for _ in range(min(per_gap, len(pending))):
                    pending.pop(0).start()

            u = _load_tile_rows(xbuf, (q % n_xbuf,), bm)
            x_lo = _unpack_lo(u).astype(BF16)
            x_hi = _unpack_hi(u).astype(BF16)
            hs = []
            for cc in range(nch):
                a = jnp.dot(x_lo, w1b[slot, cc, :half, :], preferred_element_type=F32)
                issue_some()
                a = a + jnp.dot(x_hi, w1b[slot, cc, half:, :], preferred_element_type=F32)
                issue_some()
                b = jnp.dot(x_lo, w3b[slot, cc, :half, :], preferred_element_type=F32)
                issue_some()
                b = b + jnp.dot(x_hi, w3b[slot, cc, half:, :], preferred_element_type=F32)
                issue_some()
                hs.append((a * _sigmoid(a) * b).astype(BF16))
            hmid = jnp.concatenate(hs, axis=1)
            ys = []
            for oc in range(n_out):
                cols = slice(oc * MOE_OUT_CHUNK, (oc + 1) * MOE_OUT_CHUNK)
                ys.append(jnp.dot(hmid, w2b[slot, :, cols], preferred_element_type=F32))
                issue_some()
            assert not pending
            y = jnp.concatenate(ys, axis=1)
            _store_tile_rows(ybuf, (q % 3,), _pack_bf16_pair(y[:, :half], y[:, half:]))
            return carry

        lax.fori_loop((nb * c) // nch, (nb * (c + 1)) // nch, block, 0)

    @pl.when((g == n_experts) & (c == nch - 1))
    def _drain():
        dst_copy(n_total - 1).wait()
        for r in range(bm):
            scatter_row(n_total - 1, r).start()

        @pl.when(n_total >= 2)
        def _():
            scatter_wait(n_total - 3)

        @pl.when(n_total >= 1)
        def _():
            scatter_wait(n_total - 2)

        scatter_wait(n_total - 1)
        for k in range(ahead):
            gather_wait(n_total + k)
        src_copy(n_total + ahead).wait()


def _moe_experts(eb_start, eb_count, src_plan, dst_plan, hp, w1, w3, w2, *, n_tokens, bm, nch):
    n_out_rows = 2 * n_tokens + 2 * bm
    E, D, F = w1.shape
    assert D == 2 * SUBLANES * LANES and hp.shape[1] == LANES and D % MOE_OUT_CHUNK == 0
    assert F % nch == 0 and src_plan.shape[1:] == (1, bm) and dst_plan.shape[1:] == (1, bm)
    assert dst_plan.shape[0] == src_plan.shape[0] + 1
    fc = F // nch

    def w_in_idx(g, c, es, ec):
        return (jnp.minimum(g, E - 1), 0, jnp.where(g < E, c, nch - 1))

    def w_out_idx(g, c, es, ec):
        return (jnp.minimum(g, E - 1), jnp.where(g < E, c, nch - 1), 0)

    grid_spec = pltpu.PrefetchScalarGridSpec(
        num_scalar_prefetch=2,
        grid=(E + 1, nch),
        in_specs=[
            pl.BlockSpec(memory_space=pl.ANY),
            pl.BlockSpec(memory_space=pl.ANY),
            pl.BlockSpec(memory_space=pl.ANY),
            pl.BlockSpec((None, D, fc), w_in_idx),
            pl.BlockSpec((None, D, fc), w_in_idx),
            pl.BlockSpec((None, fc, D), w_out_idx),
        ],
        out_specs=pl.BlockSpec(memory_space=pl.ANY),
        scratch_shapes=[
            pltpu.VMEM((2, nch, D, fc), BF16),
            pltpu.VMEM((2, nch, D, fc), BF16),
            pltpu.VMEM((2, F, D), BF16),
            pltpu.VMEM((MOE_GATHER_AHEAD + 1, bm * SUBLANES, LANES), U32),
            pltpu.VMEM((3, bm * SUBLANES, LANES), U32),
            pltpu.SMEM((2, 1, bm), I32),
            pltpu.SMEM((2, 1, bm), I32),
            pltpu.SemaphoreType.DMA((MOE_GATHER_AHEAD + 1,)),
            pltpu.SemaphoreType.DMA((3,)),
            pltpu.SemaphoreType.DMA((2,)),
            pltpu.SemaphoreType.DMA((2,)),
        ],
    )
    body = functools.partial(_moe_body, n_experts=E, nch=nch, bm=bm, spare_row0=2 * n_tokens)
    return pl.pallas_call(
        body,
        grid_spec=grid_spec,
        out_shape=jax.ShapeDtypeStruct((n_out_rows * SUBLANES, LANES), U32),
        compiler_params=_cparams(("arbitrary", "arbitrary")),
        name="moe_experts",
    )(eb_start, eb_count, src_plan, dst_plan, hp, w1, w3, w2)


def _combine_body(h_ref, rt_ref, g_ref, b_ref, y0_ref, y1_ref, o_ref, *, alpha, eps):
    tt = h_ref.shape[0]
    u0 = _load_tile_rows(y0_ref, (), tt)
    u1 = _load_tile_rows(y1_ref, (), tt)
    half = u0.shape[1]
    c0 = rt_ref[:, 2:3]
    c1 = rt_ref[:, 3:4]
    z_lo = alpha * h_ref[:, :half] + (_unpack_lo(u0) * c0 + _unpack_lo(u1) * c1)
    z_hi = alpha * h_ref[:, half:] + (_unpack_hi(u0) * c0 + _unpack_hi(u1) * c1)
    n = 2 * half
    mu = (jnp.sum(z_lo, axis=1, keepdims=True) + jnp.sum(z_hi, axis=1, keepdims=True)) * (1.0 / n)
    d_lo = z_lo - mu
    d_hi = z_hi - mu
    var = (jnp.sum(d_lo * d_lo, axis=1, keepdims=True)
           + jnp.sum(d_hi * d_hi, axis=1, keepdims=True)) * (1.0 / n)
    inv = lax.rsqrt(var + eps)
    o_ref[:, :half] = d_lo * inv * g_ref[:, :half] + b_ref[:, :half]
    o_ref[:, half:] = d_hi * inv * g_ref[:, half:] + b_ref[:, half:]


def _combine(h, rt, y2, g2, b2, *, alpha, tt):
    T, D = h.shape
    assert T % tt == 0 and D == 2 * SUBLANES * LANES and y2.shape[1] == LANES
    body = functools.partial(_combine_body, alpha=alpha, eps=LN_EPS)
    return pl.pallas_call(
        body,
        grid=(T // tt,),
        in_specs=[
            pl.BlockSpec((tt, D), lambda i: (i, 0)),
            pl.BlockSpec((tt, LANES), lambda i: (i, 0)),
            pl.BlockSpec((1, D), lambda i: (0, 0)),
            pl.BlockSpec((1, D), lambda i: (0, 0)),
            pl.BlockSpec((tt * SUBLANES, LANES), lambda i: (i, 0)),
            pl.BlockSpec((tt * SUBLANES, LANES), lambda i: (i + T // tt, 0)),
        ],
        out_specs=pl.BlockSpec((tt, D), lambda i: (i, 0)),
        out_shape=jax.ShapeDtypeStruct((T, D), F32),
        compiler_params=_cparams(("parallel",)),
        name="moe_combine",
    )(h, rt, g2, b2, y2, y2)


def _invert_slots_body(slot_ref, spare_ref, dst_ref, *, n_tokens):
    pltpu.sync_copy(spare_ref, dst_ref)

    def place(a, carry):
        dst_ref[slot_ref[a]] = (a & 1) * n_tokens + lax.shift_right_logical(a, 1)
        return carry

    lax.fori_loop(0, 2 * n_tokens, place, 0, unroll=32)


def _invert_slots(slot, spare, *, n_tokens):
    body = functools.partial(_invert_slots_body, n_tokens=n_tokens)
    return pl.pallas_call(
        body,
        in_specs=[pl.BlockSpec(memory_space=pltpu.SMEM), pl.BlockSpec(memory_space=pl.ANY)],
        out_specs=pl.BlockSpec(memory_space=pltpu.SMEM),
        out_shape=jax.ShapeDtypeStruct(spare.shape, I32),
        name="invert_slots",
    )(slot, spare)


def _slot_plan(expert, n_experts, bm, n_plan_blocks):
    T = expert.shape[0]
    e_flat = expert.reshape(-1)
    onehot = (e_flat[:, None] == jnp.arange(n_experts, dtype=I32)[None, :]).astype(I32)
    csum = jnp.cumsum(onehot, axis=0)
    counts = csum[-1]
    eb_count = (counts + bm - 1) // bm
    eb_start = jnp.cumsum(eb_count) - eb_count
    slot = jnp.sum(onehot * (csum - 1 + (eb_start * bm)[None, :]), axis=1)
    n_slots = n_plan_blocks * bm
    p = jnp.arange(n_slots, dtype=I32)
    spare = 2 * T + ((p // bm) % 2) * bm + p % bm
    dst = _invert_slots(slot.astype(I32), spare, n_tokens=T)
    src = jnp.where(dst < 2 * T, dst % T, 0)
    dst = jnp.concatenate([2 * T + bm + jnp.arange(bm, dtype=I32), dst])
    to_blocks = lambda a: (a * SUBLANES).reshape(-1, 1, bm)
    return eb_start.astype(I32), eb_count.astype(I32), to_blocks(src), to_blocks(dst)


def _layer(x2, p, *, batch, seq, depth, n_heads, head_dim, moba_block, moba_topk,
           n_groups, e_per_group, tiles):
    T, D = x2.shape
    A = n_heads * head_dim
    C = p["w_dw"].shape[1]
    alpha = (2.0 * depth) ** 0.25
    tn = tiles["proj_tn"]
    assert A == C == D, "column-block addressing below assumes equal branch widths"

    w_in, b_in = p["w_in"], p["b_in"]

    half = head_dim // 2
    inv_freq = jnp.power(ROPE_THETA, -jnp.arange(half, dtype=F32) * (2.0 / head_dim))
    ang = jnp.arange(seq, dtype=F32)[:, None] * inv_freq[None, :]
    cos2 = jnp.concatenate([jnp.cos(ang), jnp.cos(ang)], axis=1)
    sin_s = jnp.concatenate([-jnp.sin(ang), jnp.sin(ang)], axis=1)

    u = _in_projection(x2, w_in.astype(BF16), b_in[None, :], cos2, sin_s, seq=seq, attn_w=A, conv_c=C,
                       head_dim=head_dim, tm=tiles["proj_tm"], tn=tn)
    o = _moba_attention(u, batch=batch, seq=seq, n_heads=n_heads, head_dim=head_dim,
                        blk=moba_block, topk=moba_topk, hps=ATTN_HPS)
    hc = _conv_branch(u, p["w_dw"], p["b_dw"], p["conv_ln_g"], p["conv_ln_b"], batch=batch,
                      seq=seq, conv_c=C, col_block=3, ts=tiles["conv_ts"], rt=tiles["conv_rt"],
                      halo=CONV_HALO)

    n_experts = n_groups * e_per_group
    n_route = n_groups + n_experts
    w_route = jnp.concatenate(
        [p["w_rg"], p["w_re"].transpose(1, 0, 2).reshape(D, n_experts),
         jnp.zeros((D, LANES - n_route), F32)], axis=1)
    b_route = jnp.concatenate(
        [p["b_rg"], p["b_re"].reshape(n_experts), jnp.zeros((LANES - n_route,), F32)])[None, :]
    wr_hi = w_route.astype(BF16)
    wr_lo = jnp.concatenate([wr_hi, (w_route - wr_hi.astype(F32)).astype(BF16)], axis=1)

    h, hp, rt = _post_block(
        o, hc, u, x2, p["w_o_attn"].astype(BF16), p["w_pw2"].astype(BF16), p["b_pw2"][None, :],
        p["w_out"].astype(BF16), p["ln1_g"][None, :], p["ln1_b"][None, :], wr_hi, wr_lo, b_route,
        gate_block=4, alpha=alpha, n_groups=n_groups, e_per_group=e_per_group,
        tm=tiles["post_tm"])

    bm = tiles["moe_bm"]
    n_plan_blocks = -(-2 * T // bm) + n_experts + MOE_GATHER_AHEAD + 1
    expert = rt[:, :2].astype(I32)
    eb_start, eb_count, src_plan, dst_plan = _slot_plan(expert, n_experts, bm, n_plan_blocks)
    y2 = _moe_experts(eb_start, eb_count, src_plan, dst_plan, hp, p["w1"], p["w3"], p["w2"],
                      n_tokens=T, bm=bm, nch=tiles["moe_nch"])
    return _combine(h, rt, y2, p["ln2_g"][None, :], p["ln2_b"][None, :], alpha=alpha,
                    tt=tiles["comb_tt"])


_PARAM_NAMES = ("w_in", "b_in", "w_o_attn", "w_dw", "b_dw", "conv_ln_g", "conv_ln_b", "w_pw2",
                "b_pw2", "w_out", "ln1_g", "ln1_b", "w_rg", "b_rg", "w_re", "b_re", "w1", "w3",
                "w2", "ln2_g", "ln2_b")

_TILES = dict(proj_tm=PROJ_TM, proj_tn=PROJ_TN, conv_ts=CONV_TS, conv_rt=CONV_RT,
              post_tm=POST_TM, comb_tt=COMB_TT, moe_bm=MOE_BM, moe_nch=MOE_NCH)


def _forward(x, params, *, n_heads=N_HEADS, head_dim=HEAD_DIM, moba_block=MOBA_BLOCK,
             moba_topk=MOBA_TOPK, n_groups=N_GROUPS, e_per_group=EXPERTS_PER_GROUP, tiles=None):
    tiles = dict(_TILES, **(tiles or {}))
    B, S, D = x.shape
    depth = params["w_in"].shape[0]
    x2 = x.reshape(B * S, D)
    for l in range(depth):
        p = {k: v[l] for k, v in params.items()}
        x2 = _layer(x2, p, batch=B, seq=S, depth=depth, n_heads=n_heads, head_dim=head_dim,
                    moba_block=moba_block, moba_topk=moba_topk, n_groups=n_groups,
                    e_per_group=e_per_group, tiles=tiles)
    return x2.reshape(B, S, D)


def kernel(x, w_in, b_in, w_o_attn, w_dw, b_dw, conv_ln_g, conv_ln_b, w_pw2, b_pw2, w_out, ln1_g,
           ln1_b, w_rg, b_rg, w_re, b_re, w1, w3, w2, ln2_g, ln2_b):
    params = dict(zip(_PARAM_NAMES, (w_in, b_in, w_o_attn, w_dw, b_dw, conv_ln_g, conv_ln_b, w_pw2,
                                     b_pw2, w_out, ln1_g, ln1_b, w_rg, b_rg, w_re, b_re, w1, w3,
                                     w2, ln2_g, ln2_b)))
    return _forward(x, params)
```
